```python
import functools
import jax
import jax.numpy as jnp
from jax import lax
import numpy as np

D_MODEL = 2048
BATCH = 2
SEQ = 4096
DEPTH = 2
DEC_BATCH = 8
DEC_SEQ = 4
PAST_LEN = 16384
PAGE_SIZE = 128

HEAD_DIM = 64
N_MIXERS = 4
GROUP_WIDTH = D_MODEL // N_MIXERS
GROUP_HEADS = GROUP_WIDTH // HEAD_DIM
CONV_A_WIDTH = 31
SHORT_CONV_WIDTH = 3
NSA_HEADS = GROUP_HEADS
NSA_KV_HEADS = 2
NSA_GROUP = NSA_HEADS // NSA_KV_HEADS
CMP_BLOCK = 32
CMP_STRIDE = 16
SLC_BLOCK = 64
CHUNKS_PER_SLC = SLC_BLOCK // CMP_STRIDE
N_SLC = 16
WINDOW = 512
Q_BLOCK = 128
RWKV_HEADS = GROUP_HEADS
W_LORA = 32
A_LORA = 32
G_LORA = 96
RWKV_GN_EPS = 64e-5
N_EXP_GROUPS = 4
EXP_PER_GROUP = 8
N_EXPERTS = N_EXP_GROUPS * EXP_PER_GROUP
D_EXPERT = 512
TOP_K_IN_GROUP = 2
NSA_Q = NSA_HEADS * HEAD_DIM
NSA_KV = NSA_KV_HEADS * HEAD_DIM
N_COLS_A = 2 * GROUP_WIDTH
N_COLS_B = NSA_Q + 6 * NSA_KV + 3 * NSA_HEADS
N_COLS_C = 3 * GROUP_WIDTH
N_COLS_D = 3 * GROUP_WIDTH + W_LORA + A_LORA + G_LORA
N_IN = N_COLS_A + N_COLS_B + N_COLS_C + N_COLS_D
LN_EPS = 1e-5
NEG_INF = -1e30
FORCED_SCORE = 1e4

kernel_name = 'hymba_nsa_rwkv7_conv_hmoe_step'


def split_cols(z, sizes):
    idx = [int(i) for i in np.cumsum(sizes)[:-1]]
    return jnp.split(z, idx, axis=-1)


def layer_norm(x, g, b):
    xf = x.astype(jnp.float32)
    mu = jnp.mean(xf, axis=-1, keepdims=True)
    var = jnp.mean(jnp.square(xf - mu), axis=-1, keepdims=True)
    return ((xf - mu) * lax.rsqrt(var + LN_EPS) * g.astype(jnp.float32) + b.astype(jnp.float32)).astype(x.dtype)


def masked_softmax(s, mask):
    s = jnp.where(mask, s, NEG_INF)
    e = jnp.exp(s - jnp.max(s, axis=-1, keepdims=True)) * mask.astype(s.dtype)
    return e / jnp.maximum(jnp.sum(e, axis=-1, keepdims=True), 1e-30)


def causal_dwconv(u, buf, w):
    full = jnp.concatenate([buf.astype(u.dtype), u], axis=1)
    y = lax.conv_general_dilated(full, w[:, None, :].astype(u.dtype), window_strides=(1,), padding='VALID',
                                 dimension_numbers=('NWC', 'WIO', 'NWC'), feature_group_count=u.shape[-1])
    return y, full[:, full.shape[1] - (w.shape[0] - 1):]


def conformer_conv(za, buf, P, l):
    val, gate = split_cols(za, [GROUP_WIDTH, GROUP_WIDTH])
    u = val * jax.nn.sigmoid(gate)
    c, new_buf = causal_dwconv(u, buf, P['conv_a_w'][l])
    c = layer_norm(c + P['conv_a_b'][l], P['ln_a_g'][l], P['ln_a_b'][l])
    return jax.nn.silu(c), new_buf


def short_conv_mix(zc, buf, P, l):
    bg, cg, h = split_cols(zc, [GROUP_WIDTH] * 3)
    c, new_buf = causal_dwconv(cg * h, buf, P['conv_c_w'][l])
    return bg * c, new_buf


def compress_blocks(x, w, pe):
    B, L = x.shape[:2]
    ch = x.reshape(B, L // CMP_STRIDE, CMP_STRIDE, NSA_KV_HEADS, HEAD_DIM)
    blk = jnp.concatenate([ch[:, :-1], ch[:, 1:]], axis=2)
    return jnp.einsum('bnpgd,pde->bnge', blk + pe[None, None, :, None, :].astype(blk.dtype), w)


def to_sel_blocks(x):
    B, L = x.shape[:2]
    return x.reshape(B, L // SLC_BLOCK, SLC_BLOCK, NSA_KV_HEADS, HEAD_DIM).transpose(0, 3, 1, 2, 4)


def nsa_attend(q, t, ck, cv, cmp_end, kb, vb, kw, vw, s_w, gates):
    f32 = jnp.float32
    B, Q, G, M, _ = q.shape
    qf = q.astype(f32) * (HEAD_DIM ** -0.5)
    sc = jnp.einsum('bqgmd,bngd->bqgmn', qf, ck.astype(f32))
    pc = masked_softmax(sc, (cmp_end[None, :] <= t[:, None])[None, :, None, None, :])
    o_cmp = jnp.einsum('bqgmn,bngd->bqgmd', pc, cv.astype(f32))
    ps = jnp.sum(pc, axis=3)
    cs = 0.5 * (jnp.pad(ps, ((0, 0), (0, 0), (0, 0), (0, 1))) + jnp.pad(ps, ((0, 0), (0, 0), (0, 0), (1, 0))))
    ns = kb.shape[2]
    sel = jnp.sum(cs.reshape(B, Q, G, ns, CHUNKS_PER_SLC), axis=-1)
    j = jnp.arange(ns)[None, :]
    cur = (t // SLC_BLOCK)[:, None]
    valid = j * SLC_BLOCK <= t[:, None]
    forced = (j == 0) | (j == cur) | (j == cur - 1)
    sel = jnp.where(forced[None, :, None], FORCED_SCORE, jnp.where(valid[None, :, None], sel, -FORCED_SCORE))
    n_sel = min(N_SLC, ns)
    _, idx = lax.top_k(sel, n_sel)
    bi = jnp.arange(B)[:, None, None, None]
    gi = jnp.arange(G)[None, None, :, None]
    ksel = kb[bi, gi, idx].reshape(B, Q, G, n_sel * SLC_BLOCK, HEAD_DIM)
    vsel = vb[bi, gi, idx].reshape(B, Q, G, n_sel * SLC_BLOCK, HEAD_DIM)
    pos = (idx[..., None] * SLC_BLOCK + jnp.arange(SLC_BLOCK)).reshape(B, Q, G, n_sel * SLC_BLOCK)
    mask_s = (pos <= t[None, :, None, None])[:, :, :, None, :]
    ss = jnp.einsum('bqgmd,bqgkd->bqgmk', qf, ksel.astype(f32))
    o_slc = jnp.einsum('bqgmk,bqgkd->bqgmd', masked_softmax(ss, mask_s), vsel.astype(f32))
    dt = t[:, None] - s_w[None, :]
    mask_w = ((dt >= 0) & (dt < WINDOW) & (s_w[None, :] >= 0))[None, :, None, None, :]
    sw = jnp.einsum('bqgmd,bwgd->bqgmw', qf, kw.astype(f32))
    o_win = jnp.einsum('bqgmw,bwgd->bqgmd', masked_softmax(sw, mask_w), vw.astype(f32))
    gt = jax.nn.sigmoid(gates.astype(f32))
    return gt[..., 0:1] * o_cmp + gt[..., 1:2] * o_slc + gt[..., 2:3] * o_win


def nsa_split(zb):
    B, T = zb.shape[:2]
    parts = split_cols(zb, [NSA_Q] + [NSA_KV] * 6 + [3 * NSA_HEADS])
    q = parts[0].reshape(B, T, NSA_KV_HEADS, NSA_GROUP, HEAD_DIM)
    kvs = [p.reshape(B, T, NSA_KV_HEADS, HEAD_DIM) for p in parts[1:7]]
    gates = parts[7].reshape(B, T, NSA_KV_HEADS, NSA_GROUP, 3)
    return q, kvs, gates


def nsa_prompt(zb, P, l):
    B, T = zb.shape[:2]
    q, (kc, vc, ks, vs, kw, vw), gates = nsa_split(zb)
    ck = compress_blocks(kc, P['nsa_wc'][l, 0], P['nsa_pe'][l, 0])
    cv = compress_blocks(vc, P['nsa_wc'][l, 1], P['nsa_pe'][l, 1])
    cmp_end = jnp.arange(ck.shape[1]) * CMP_STRIDE + CMP_BLOCK - 1
    kb, vb = to_sel_blocks(ks), to_sel_blocks(vs)
    pad = ((0, 0), (WINDOW, 0), (0, 0), (0, 0))
    kw_pad, vw_pad = jnp.pad(kw, pad), jnp.pad(vw, pad)
    nq = T // Q_BLOCK
    qb = jnp.swapaxes(q.reshape(B, nq, Q_BLOCK, NSA_KV_HEADS, NSA_GROUP, HEAD_DIM), 0, 1)
    gb = jnp.swapaxes(gates.reshape(B, nq, Q_BLOCK, NSA_KV_HEADS, NSA_GROUP, 3), 0, 1)

    def query_block(args):
        q_i, g_i, i = args
        start = i * Q_BLOCK
        t = start + jnp.arange(Q_BLOCK)
        kwin = lax.dynamic_slice_in_dim(kw_pad, start, WINDOW + Q_BLOCK, axis=1)
        vwin = lax.dynamic_slice_in_dim(vw_pad, start, WINDOW + Q_BLOCK, axis=1)
        s_w = start - WINDOW + jnp.arange(WINDOW + Q_BLOCK)
        return nsa_attend(q_i, t, ck, cv, cmp_end, kb, vb, kwin, vwin, s_w, g_i)

    o = lax.map(query_block, (qb, gb, jnp.arange(nq)))
    o = jnp.swapaxes(o, 0, 1).reshape(B, T, NSA_Q)
    rows = jnp.stack([kc, vc, ks, vs], axis=2)
    win = jnp.stack([kw, vw], axis=2)[:, T - min(WINDOW, T):]
    return o, rows, win


def nsa_sample(zb, cache_l, page_table, win_l, P, l):
    B, T = zb.shape[:2]
    q, (kc, vc, ks, vs, kw, vw), gates = nsa_split(zb)
    past_len = page_table.shape[1] * cache_l.shape[1]
    past = cache_l[page_table].reshape(B, past_len, 4, NSA_KV_HEADS, HEAD_DIM)
    rows = jnp.stack([kc, vc, ks, vs], axis=2).astype(past.dtype)
    full = jnp.concatenate([past, rows], axis=1)
    L = past_len + T
    Lp = -(-L // SLC_BLOCK) * SLC_BLOCK
    full = jnp.pad(full, ((0, 0), (0, Lp - L), (0, 0), (0, 0), (0, 0)))
    ck = compress_blocks(full[:, :, 0], P['nsa_wc'][l, 0], P['nsa_pe'][l, 0])
    cv = compress_blocks(full[:, :, 1], P['nsa_wc'][l, 1], P['nsa_pe'][l, 1])
    cmp_end = jnp.arange(ck.shape[1]) * CMP_STRIDE + CMP_BLOCK - 1
    kb, vb = to_sel_blocks(full[:, :, 2]), to_sel_blocks(full[:, :, 3])
    wb = win_l.shape[1]
    wall = jnp.concatenate([win_l, jnp.stack([kw, vw], axis=2).astype(win_l.dtype)], axis=1)
    s_w = past_len - wb + jnp.arange(wb + T)
    t = past_len + jnp.arange(T)
    o = nsa_attend(q, t, ck, cv, cmp_end, kb, vb, wall[:, :, 0], wall[:, :, 1], s_w, gates)
    return o.reshape(B, T, NSA_Q), rows, wall[:, T:]


def rwkv7_mix(zd, shift, s0, P, l):
    B, T = zd.shape[:2]
    f32 = jnp.float32
    prev = jnp.concatenate([shift[:, None].astype(zd.dtype), zd[:, :-1]], axis=1)
    zs = zd + (prev - zd) * P['rw_mu'][l]
    r, k, v, wl, al, gl = split_cols(zs, [GROUP_WIDTH] * 3 + [W_LORA, A_LORA, G_LORA])
    w_log = -jax.nn.softplus(-(P['rw_w0'][l] + jnp.tanh(wl) @ P['rw_w2'][l]).astype(f32)) - 0.5
    decay = jnp.exp(-jnp.exp(w_log))
    a = jax.nn.sigmoid((P['rw_a0'][l] + al @ P['rw_a2'][l]).astype(f32))
    g = jax.nn.sigmoid(gl) @ P['rw_g2'][l]
    r, k, v = r.astype(f32), k.astype(f32), v.astype(f32)
    kk = k * P['rw_kk'][l]
    k = k * (1.0 + (a - 1.0) * P['rw_ka'][l])
    heads = lambda u: u.reshape(B, T, RWKV_HEADS, HEAD_DIM)
    r, k, v, kk, a, decay = map(heads, (r, k, v, kk, a, decay))
    kk = kk / jnp.maximum(jnp.linalg.norm(kk, axis=-1, keepdims=True), 1e-12)
    b = kk * a

    def step(S, inp):
        r_t, w_t, k_t, v_t, kk_t, b_t = inp
        sa = jnp.einsum('bhij,bhj->bhi', S, kk_t)
        S = S * w_t[:, :, None, :] - sa[..., None] * b_t[:, :, None, :] + v_t[..., None] * k_t[:, :, None, :]
        return S, jnp.einsum('bhij,bhj->bhi', S, r_t)

    seq_first = lambda u: jnp.swapaxes(u, 0, 1)
    s_T, y = lax.scan(step, s0.astype(f32), tuple(map(seq_first, (r, decay, k, v, kk, b))))
    y = seq_first(y)
    mu = jnp.mean(y, axis=-1, keepdims=True)
    var = jnp.mean(jnp.square(y - mu), axis=-1, keepdims=True)
    y = ((y - mu) * lax.rsqrt(var + RWKV_GN_EPS)).reshape(B, T, GROUP_WIDTH) * P['rw_lnx_g'][l] + P['rw_lnx_b'][l]
    bonus = jnp.sum(r * k * P['rw_rk'][l], axis=-1, keepdims=True) * v
    out = (y + bonus.reshape(B, T, GROUP_WIDTH)) * g
    return out, s_T, zd[:, -1]


def hier_moe(x, P, l):
    B, T, D = x.shape
    n = B * T
    f32 = jnp.float32
    xf = x.reshape(n, D)
    glog = (xf @ P['moe_w_grp'][l] + P['moe_b_grp'][l]).astype(f32)
    gsel = jnp.argmax(glog, axis=-1)
    gp = jnp.take_along_axis(jax.nn.softmax(glog, axis=-1), gsel[:, None], axis=-1)
    elog = (xf @ P['moe_w_rte'][l] + P['moe_b_rte'][l]).astype(f32).reshape(n, N_EXP_GROUPS, EXP_PER_GROUP)
    elog = jnp.take_along_axis(elog, gsel[:, None, None], axis=1)[:, 0]
    topv, topi = lax.top_k(jax.nn.softmax(elog, axis=-1), TOP_K_IN_GROUP)
    wts = topv / jnp.sum(topv, axis=-1, keepdims=True) * gp
    eid = gsel[:, None] * EXP_PER_GROUP + topi
    comb = jnp.einsum('nk,nke->ne', wts, jax.nn.one_hot(eid, N_EXPERTS, dtype=f32))
    y = jnp.zeros((n, D), f32)
    for grp in range(N_EXP_GROUPS):
        sl = slice(grp * EXP_PER_GROUP, (grp + 1) * EXP_PER_GROUP)
        hg = jax.nn.silu(jnp.einsum('nd,edf->nef', xf, P['moe_w_gate'][l, sl])) * jnp.einsum('nd,edf->nef', xf, P['moe_w_up'][l, sl])
        y = y + jnp.einsum('nef,efd->nd', hg * comb[:, sl, None].astype(hg.dtype), P['moe_w_down'][l, sl])
    return y.reshape(B, T, D)


def trunk_layer(x, l, P, alpha, nsa_fn, conv_a_buf, conv_c_buf, rw_state, rw_shift):
    z = x @ P['w_in'][l] + P['b_in'][l]
    za, zb, zc, zd = split_cols(z, [N_COLS_A, N_COLS_B, N_COLS_C, N_COLS_D])
    ya, conv_a_new = conformer_conv(za, conv_a_buf, P, l)
    yb, nsa_rows, win_new = nsa_fn(zb)
    yc, conv_c_new = short_conv_mix(zc, conv_c_buf, P, l)
    yd, rw_state_new, rw_shift_new = rwkv7_mix(zd, rw_shift, rw_state, P, l)
    mix = jnp.concatenate([ya, yb, yc, yd], axis=-1).astype(x.dtype)
    h = layer_norm(alpha * x + (mix @ P['w_out'][l] + P['b_out'][l]), P['ln1_g'][l], P['ln1_b'][l])
    out = layer_norm(alpha * h + hier_moe(h, P, l), P['ln2_g'][l], P['ln2_b'][l])
    return out, (nsa_rows, win_new, conv_a_new, conv_c_new, rw_state_new, rw_shift_new)


def setup_inputs(seed: int = 0) -> dict:
    key = jax.random.key(seed)
    keys = iter(jax.random.split(key, 48))
    f32 = jnp.float32

    def nrm(shape, scale):
        return jax.random.normal(next(keys), shape, f32) * scale

    n_pages = PAST_LEN // PAGE_SIZE
    n_pool = (5 * DEC_BATCH * n_pages) // 4
    win_rows = min(WINDOW, PAST_LEN)
    beta = (8.0 * DEPTH) ** -0.25
    L, D, GW = DEPTH, D_MODEL, GROUP_WIDTH
    inp = {}
    inp['x_prompt'] = nrm((BATCH, SEQ, D), 1.0)
    inp['x_sample'] = nrm((DEC_BATCH, DEC_SEQ, D), 1.0)
    inp['cache_nsa'] = nrm((L, n_pool, PAGE_SIZE, 4, NSA_KV_HEADS, HEAD_DIM), 1.0)
    inp['cache_win'] = nrm((L, DEC_BATCH, win_rows, 2, NSA_KV_HEADS, HEAD_DIM), 1.0)
    inp['state_conv_a'] = nrm((L, DEC_BATCH, CONV_A_WIDTH - 1, GW), 0.5)
    inp['state_conv_c'] = nrm((L, DEC_BATCH, SHORT_CONV_WIDTH - 1, GW), 0.5)
    inp['state_rwkv'] = nrm((L, DEC_BATCH, RWKV_HEADS, HEAD_DIM, HEAD_DIM), 0.1)
    inp['state_rwkv_shift'] = nrm((L, DEC_BATCH, N_COLS_D), 1.0)
    perm = jax.random.permutation(next(keys), n_pool)
    inp['page_table'] = perm[: DEC_BATCH * n_pages].reshape(DEC_BATCH, n_pages).astype(jnp.int32)
    inp['w_in'] = nrm((L, D, N_IN), D ** -0.5)
    inp['b_in'] = nrm((L, N_IN), 0.01)
    inp['w_out'] = nrm((L, D, D), D ** -0.5 * beta)
    inp['b_out'] = nrm((L, D), 0.01)
    inp['ln1_g'] = 1.0 + nrm((L, D), 0.05)
    inp['ln1_b'] = nrm((L, D), 0.01)
    inp['ln2_g'] = 1.0 + nrm((L, D), 0.05)
    inp['ln2_b'] = nrm((L, D), 0.01)
    inp['conv_a_w'] = nrm((L, CONV_A_WIDTH, GW), CONV_A_WIDTH ** -0.5)
    inp['conv_a_b'] = nrm((L, GW), 0.01)
    inp['ln_a_g'] = 1.0 + nrm((L, GW), 0.05)
    inp['ln_a_b'] = nrm((L, GW), 0.01)
    inp['conv_c_w'] = nrm((L, SHORT_CONV_WIDTH, GW), SHORT_CONV_WIDTH ** -0.5)
    inp['nsa_wc'] = nrm((L, 2, CMP_BLOCK, HEAD_DIM, HEAD_DIM), (CMP_BLOCK * HEAD_DIM) ** -0.5)
    inp['nsa_pe'] = nrm((L, 2, CMP_BLOCK, HEAD_DIM), 0.1)
    inp['rw_mu'] = jax.random.uniform(next(keys), (L, N_COLS_D), f32)
    inp['rw_w0'] = nrm((L, GW), 0.5) - 0.5
    inp['rw_w2'] = nrm((L, W_LORA, GW), 0.1)
    inp['rw_a0'] = nrm((L, GW), 0.1)
    inp['rw_a2'] = nrm((L, A_LORA, GW), 0.1)
    inp['rw_g2'] = nrm((L, G_LORA, GW), G_LORA ** -0.5)
    inp['rw_kk'] = 0.85 + nrm((L, GW), 0.05)
    inp['rw_ka'] = 1.0 + nrm((L, GW), 0.05)
    inp['rw_rk'] = nrm((L, RWKV_HEADS, HEAD_DIM), 0.1)
    inp['rw_lnx_g'] = 1.0 + nrm((L, GW), 0.05)
    inp['rw_lnx_b'] = nrm((L, GW), 0.01)
    inp['moe_w_grp'] = nrm((L, D, N_EXP_GROUPS), D ** -0.5)
    inp['moe_b_grp'] = nrm((L, N_EXP_GROUPS), 0.01)
    inp['moe_w_rte'] = nrm((L, D, N_EXPERTS), D ** -0.5)
    inp['moe_b_rte'] = nrm((L, N_EXPERTS), 0.01)
    inp['moe_w_gate'] = nrm((L, N_EXPERTS, D, D_EXPERT), D ** -0.5)
    inp['moe_w_up'] = nrm((L, N_EXPERTS, D, D_EXPERT), D ** -0.5)
    inp['moe_w_down'] = nrm((L, N_EXPERTS, D_EXPERT, D), D_EXPERT ** -0.5 * beta)
    return inp


def reference(x_prompt, x_sample, cache_nsa, cache_win, state_conv_a, state_conv_c, state_rwkv, state_rwkv_shift, page_table,
              w_in, b_in, w_out, b_out, ln1_g, ln1_b, ln2_g, ln2_b, conv_a_w, conv_a_b, ln_a_g, ln_a_b, conv_c_w,
              nsa_wc, nsa_pe, rw_mu, rw_w0, rw_w2, rw_a0, rw_a2, rw_g2, rw_kk, rw_ka, rw_rk, rw_lnx_g, rw_lnx_b,
              moe_w_grp, moe_b_grp, moe_w_rte, moe_b_rte, moe_w_gate, moe_w_up, moe_w_down):
    P = {'w_in': w_in, 'b_in': b_in, 'w_out': w_out, 'b_out': b_out,
         'ln1_g': ln1_g, 'ln1_b': ln1_b, 'ln2_g': ln2_g, 'ln2_b': ln2_b,
         'conv_a_w': conv_a_w, 'conv_a_b': conv_a_b, 'ln_a_g': ln_a_g, 'ln_a_b': ln_a_b, 'conv_c_w': conv_c_w,
         'nsa_wc': nsa_wc, 'nsa_pe': nsa_pe,
         'rw_mu': rw_mu, 'rw_w0': rw_w0, 'rw_w2': rw_w2, 'rw_a0': rw_a0, 'rw_a2': rw_a2, 'rw_g2': rw_g2,
         'rw_kk': rw_kk, 'rw_ka': rw_ka, 'rw_rk': rw_rk, 'rw_lnx_g': rw_lnx_g, 'rw_lnx_b': rw_lnx_b,
         'moe_w_grp': moe_w_grp, 'moe_b_grp': moe_b_grp, 'moe_w_rte': moe_w_rte, 'moe_b_rte': moe_b_rte,
         'moe_w_gate': moe_w_gate, 'moe_w_up': moe_w_up, 'moe_w_down': moe_w_down}
    alpha = (2.0 * DEPTH) ** 0.25
    xp, xs = x_prompt, x_sample
    bp = xp.shape[0]
    outs_p, outs_s = [], []
    for l in range(DEPTH):
        xp, st_p = trunk_layer(xp, l, P, alpha, functools.partial(nsa_prompt, P=P, l=l),
                               jnp.zeros((bp, CONV_A_WIDTH - 1, GROUP_WIDTH), xp.dtype),
                               jnp.zeros((bp, SHORT_CONV_WIDTH - 1, GROUP_WIDTH), xp.dtype),
                               jnp.zeros((bp, RWKV_HEADS, HEAD_DIM, HEAD_DIM), jnp.float32),
                               jnp.zeros((bp, N_COLS_D), xp.dtype))
        xs, st_s = trunk_layer(xs, l, P, alpha,
                               functools.partial(nsa_sample, cache_l=cache_nsa[l], page_table=page_table, win_l=cache_win[l], P=P, l=l),
                               state_conv_a[l], state_conv_c[l], state_rwkv[l], state_rwkv_shift[l])
        outs_p.append(st_p)
        outs_s.append(st_s)
    stk = lambda outs, i: jnp.stack([o[i] for o in outs], axis=0)
    return (xp, xs, stk(outs_p, 0), stk(outs_s, 0), stk(outs_p, 1), stk(outs_s, 1), stk(outs_p, 2), stk(outs_s, 2),
            stk(outs_p, 3), stk(outs_s, 3), stk(outs_p, 4), stk(outs_s, 4), stk(outs_p, 5), stk(outs_s, 5))
```

```python
import functools

import numpy as np
import jax
import jax.numpy as jnp
from jax import lax
from jax.experimental import pallas as pl
from jax.experimental.pallas import tpu as pltpu

F32 = jnp.float32
BF16 = jnp.bfloat16

D_MODEL = 2048
DEPTH = 2
HEAD_DIM = 64
GW = 512
HEADS = 8
CONV_A_WIDTH = 31
SHORT_CONV_WIDTH = 3
KV_HEADS = 2
Q_PER_KV = 4
CMP_STRIDE = 16
SLC_BLOCK = 64
N_SLC = 16
WINDOW = 512
Q_BLOCK = 128
PAGE = 128
RWKV_GN_EPS = 64e-5
LN_EPS = 1e-5
NEG = -1e30
FORCED = 1e4
N_EXPERTS = 32
EXP_PER_GROUP = 8
N_GROUPS = 4
D_EXPERT = 512
N_IN = 5560
N_COLS_D = 1696

N_Z = 6144
C_VAL, C_GATE = 0, 512
C_Q, C_ROWS, C_KW, C_GATES = 1024, 1536, 2048, 2304
C_BG, C_CG, C_H = 2560, 3072, 3584
C_R, C_K, C_V = 4096, 4608, 5120
C_WL, C_AL, C_GL = 5632, 5760, 5888

VMEM_LIMIT = 56 * 1024 * 1024
MOE_TILE = 256


def _cparams(sem, vmem=VMEM_LIMIT):
    return pltpu.CompilerParams(dimension_semantics=sem, vmem_limit_bytes=vmem)


def _const_spec(shape):
    nd = len(shape)
    return pl.BlockSpec(shape, lambda *_: (0,) * nd)


def _split2(x):
    hi = x.astype(BF16)
    lo = (x - hi.astype(F32)).astype(BF16)
    return hi, lo


def _split3(x):
    hi = x.astype(BF16)
    r1 = x - hi.astype(F32)
    mid = r1.astype(BF16)
    lo = (r1 - mid.astype(F32)).astype(BF16)
    return hi, mid, lo


_NN = (((1,), (0,)), ((), ()))
_NT = (((1,), (1,)), ((), ()))


def _dot(a, b, dims=_NN):
    return lax.dot_general(a, b, dims, preferred_element_type=F32)


def _dot3(a, b, dims=_NN):
    ah, al = _split2(a)
    bh, bl = _split2(b)
    return _dot(ah, bh, dims) + _dot(al, bh, dims) + _dot(ah, bl, dims)


def _dot_exact_rhs(a, b_bf16, dims=_NN):
    hi, mid, lo = _split3(a)
    return _dot(hi, b_bf16, dims) + _dot(mid, b_bf16, dims) + _dot(lo, b_bf16, dims)


def _dot_exact_lhs(a_bf16, b, dims=_NN):
    hi, mid, lo = _split3(b)
    return _dot(a_bf16, hi, dims) + _dot(a_bf16, mid, dims) + _dot(a_bf16, lo, dims)


def _sigmoid(x):
    return 1.0 / (1.0 + jnp.exp(-x))


def _silu(x):
    return x * _sigmoid(x)


def _layer_norm(x, g, b, eps):
    mu = jnp.mean(x, axis=-1, keepdims=True)
    xc = x - mu
    var = jnp.mean(xc * xc, axis=-1, keepdims=True)
    return xc * lax.rsqrt(var + eps) * g + b


def _masked_softmax(s, maskf):
    s = jnp.where(maskf > 0.5, s, NEG)
    e = jnp.exp(s - jnp.max(s, axis=-1, keepdims=True)) * maskf
    return e / jnp.maximum(jnp.sum(e, axis=-1, keepdims=True), 1e-30)


def _proj_in_kernel(x_ref, w_ref, b_ref, o_ref, xb_ref):
    @pl.when(pl.program_id(1) == 0)
    def _():
        xb_ref[...] = x_ref[...].astype(BF16)

    o_ref[...] = _dot(xb_ref[...], w_ref[...]) + b_ref[...]


def _proj_in(x2d, w_bf16, b_row):
    n = x2d.shape[0]
    tm = min(1024, n)
    tn = 512
    return pl.pallas_call(
        _proj_in_kernel,
        grid=(n // tm, N_Z // tn),
        in_specs=[pl.BlockSpec((tm, D_MODEL), lambda i, j: (i, 0)),
                  pl.BlockSpec((D_MODEL, tn), lambda i, j: (0, j)),
                  pl.BlockSpec((1, tn), lambda i, j: (0, j))],
        out_specs=pl.BlockSpec((tm, tn), lambda i, j: (i, j)),
        out_shape=jax.ShapeDtypeStruct((n, N_Z), F32),
        scratch_shapes=[pltpu.VMEM((tm, D_MODEL), BF16)],
        compiler_params=_cparams(("parallel", "arbitrary")),
        name="proj_in",
    )(x2d, w_bf16, b_row)


_CONV_ROWS = 32


def _conv_kernel(val_ref, gate_ref, bg_ref, cg_ref, h_ref, bufa_ref, bufc_ref, wa_ref, ba_ref, lag_ref,
                 lab_ref, wc_ref, ya_ref, yc_ref, na_ref, nc_ref, ea_ref, ec_ref, *, tt, t_last):
    it = pl.program_id(1)
    nt = pl.num_programs(1)
    ha = 32
    hc = 8

    @pl.when(it == 0)
    def _():
        ea_ref[0:ha, :] = jnp.zeros((ha, GW), F32)
        ea_ref[ha - (CONV_A_WIDTH - 1):ha, :] = bufa_ref[0]
        ec_ref[0:hc, :] = jnp.zeros((hc, GW), F32)
        ec_ref[hc - (SHORT_CONV_WIDTH - 1):hc, :] = bufc_ref[0]

    @pl.when(it > 0)
    def _():
        ea_ref[0:ha, :] = ea_ref[tt:tt + ha, :]
        ec_ref[0:hc, :] = ec_ref[tt:tt + hc, :]

    ea_ref[ha:ha + tt, :] = val_ref[0] * _sigmoid(gate_ref[0])
    ec_ref[hc:hc + tt, :] = cg_ref[0] * h_ref[0]

    rc = min(_CONV_ROWS, tt)
    offa = ha - (CONV_A_WIDTH - 1)
    offc = hc - (SHORT_CONV_WIDTH - 1)
    for c in range(tt // rc):
        r0 = c * rc
        acc = jnp.zeros((rc, GW), F32)
        for j in range(CONV_A_WIDTH):
            acc = acc + wa_ref[j:j + 1, :] * ea_ref[r0 + offa + j:r0 + offa + j + rc, :]
        y = _layer_norm(acc + ba_ref[...], lag_ref[...], lab_ref[...], LN_EPS)
        ya_ref[0, r0:r0 + rc, :] = _silu(y)
        accc = jnp.zeros((rc, GW), F32)
        for j in range(SHORT_CONV_WIDTH):
            accc = accc + wc_ref[j:j + 1, :] * ec_ref[r0 + offc + j:r0 + offc + j + rc, :]
        yc_ref[0, r0:r0 + rc, :] = bg_ref[0, r0:r0 + rc, :] * accc

    @pl.when(it == nt - 1)
    def _():
        na_ref[0] = ea_ref[t_last + offa:t_last + ha, :]
        nc_ref[0] = ec_ref[t_last + offc:t_last + hc, :]


def _conv_mix(z, buf_a, buf_c, wa, ba, lag, lab, wc, t_real):
    B, T, _ = z.shape
    tt = min(256, T)
    nt = T // tt
    t_last = t_real - (nt - 1) * tt
    zspec = lambda col: pl.BlockSpec((1, tt, GW), lambda b, i: (b, i, col // GW))
    return pl.pallas_call(
        functools.partial(_conv_kernel, tt=tt, t_last=t_last),
        grid=(B, nt),
        in_specs=[zspec(C_VAL), zspec(C_GATE), zspec(C_BG), zspec(C_CG), zspec(C_H),
                  pl.BlockSpec((1, CONV_A_WIDTH - 1, GW), lambda b, i: (b, 0, 0)),
                  pl.BlockSpec((1, SHORT_CONV_WIDTH - 1, GW), lambda b, i: (b, 0, 0)),
                  _const_spec((CONV_A_WIDTH, GW)), _const_spec((1, GW)), _const_spec((1, GW)),
                  _const_spec((1, GW)), _const_spec((SHORT_CONV_WIDTH, GW))],
        out_specs=[pl.BlockSpec((1, tt, GW), lambda b, i: (b, i, 0)),
                   pl.BlockSpec((1, tt, GW), lambda b, i: (b, i, 0)),
                   pl.BlockSpec((1, CONV_A_WIDTH - 1, GW), lambda b, i: (b, 0, 0)),
                   pl.BlockSpec((1, SHORT_CONV_WIDTH - 1, GW), lambda b, i: (b, 0, 0))],
        out_shape=[jax.ShapeDtypeStruct((B, T, GW), F32), jax.ShapeDtypeStruct((B, T, GW), F32),
                   jax.ShapeDtypeStruct((B, CONV_A_WIDTH - 1, GW), F32),
                   jax.ShapeDtypeStruct((B, SHORT_CONV_WIDTH - 1, GW), F32)],
        scratch_shapes=[pltpu.VMEM((tt + 32, GW), F32), pltpu.VMEM((tt + 8, GW), F32)],
        compiler_params=_cparams(("parallel", "arbitrary")),
        name="conv_mix",
    )(z, z, z, z, z, buf_a, buf_c, wa, ba, lag, lab, wc)


def _rwkv_prep_kernel(r_ref, k_ref, v_ref, wl_ref, al_ref, gl_ref, sh_ref, mu_ref, w0_ref, w2_ref, a0_ref,
                      a2_ref, g2_ref, kkp_ref, kap_ref, rk_ref,
                      ro_ref, wo_ref, ko_ref, vo_ref, kko_ref, bo_ref, bonus_ref, g_ref, carry_ref, *, tt, t_last):
    it = pl.program_id(1)

    @pl.when(it == 0)
    def _():
        carry_ref[...] = sh_ref[0]

    row = lax.broadcasted_iota(jnp.int32, (tt, 1), 0)

    def mixed(x, c0):
        w = x.shape[1]
        prev = pltpu.roll(x, 1, 0) if tt > 1 else x
        prev = jnp.where(row == 0, carry_ref[:, c0:c0 + w], prev)
        return x + (prev - x) * mu_ref[:, c0:c0 + w]

    xr, xk, xv = r_ref[0], k_ref[0], v_ref[0]
    xwl, xal, xgl = wl_ref[0], al_ref[0], gl_ref[0]
    r = mixed(xr, 0)
    k = mixed(xk, GW)
    v = mixed(xv, 2 * GW)
    wl = mixed(xwl, 3 * GW)
    al = mixed(xal, 3 * GW + 128)
    gl = mixed(xgl, 3 * GW + 256)
    lr = t_last - 1
    carry_ref[:, 0:GW] = xr[lr:lr + 1]
    carry_ref[:, GW:2 * GW] = xk[lr:lr + 1]
    carry_ref[:, 2 * GW:3 * GW] = xv[lr:lr + 1]
    carry_ref[:, 3 * GW:3 * GW + 128] = xwl[lr:lr + 1]
    carry_ref[:, 3 * GW + 128:3 * GW + 256] = xal[lr:lr + 1]
    carry_ref[:, 3 * GW + 256:3 * GW + 384] = xgl[lr:lr + 1]

    y = w0_ref[...] + _dot3(jnp.tanh(wl), w2_ref[...])
    u = -y
    w_log = -(jnp.maximum(u, 0.0) + jnp.log(1.0 + jnp.exp(-jnp.abs(u)))) - 0.5
    decay = jnp.exp(-jnp.exp(w_log))
    a = _sigmoid(a0_ref[...] + _dot3(al, a2_ref[...]))
    g_ref[0] = _dot3(_sigmoid(gl), g2_ref[...])
    kk = k * kkp_ref[...]
    k2 = k * (1.0 + (a - 1.0) * kap_ref[...])
    rkk = r * k2 * rk_ref[...]
    for h in range(HEADS):
        sl = slice(h * HEAD_DIM, (h + 1) * HEAD_DIM)
        kk_h = kk[:, sl]
        nrm = jnp.sqrt(jnp.sum(kk_h * kk_h, axis=-1, keepdims=True))
        kk_h = kk_h / jnp.maximum(nrm, 1e-12)
        v_h = v[:, sl]
        ro_ref[0, h] = r[:, sl]
        wo_ref[0, h] = decay[:, sl]
        ko_ref[0, h] = k2[:, sl]
        vo_ref[0, h] = v_h
        kko_ref[0, h] = kk_h
        bo_ref[0, h] = kk_h * a[:, sl]
        bonus_ref[0, h] = jnp.sum(rkk[:, sl], axis=-1, keepdims=True) * v_h


def _rwkv_prep(z, shift, mu, w0, w2p, a0, a2p, g2p, kkp, kap, rk, t_real):
    B, T, _ = z.shape
    tt = min(256, T)
    nt = T // tt
    t_last = t_real - (nt - 1) * tt
    wide = lambda col: pl.BlockSpec((1, tt, GW), lambda b, i: (b, i, col // GW))
    thin = lambda col: pl.BlockSpec((1, tt, 128), lambda b, i: (b, i, col // 128))
    head_out = pl.BlockSpec((1, HEADS, tt, HEAD_DIM), lambda b, i: (b, 0, i, 0))
    head_shape = jax.ShapeDtypeStruct((B, HEADS, T, HEAD_DIM), F32)
    dw = 3 * GW + 384
    return pl.pallas_call(
        functools.partial(_rwkv_prep_kernel, tt=tt, t_last=t_last),
        grid=(B, nt),
        in_specs=[wide(C_R), wide(C_K), wide(C_V), thin(C_WL), thin(C_AL), thin(C_GL),
                  pl.BlockSpec((1, 1, dw), lambda b, i: (b, 0, 0)),
                  _const_spec((1, dw)), _const_spec((1, GW)), _const_spec((128, GW)), _const_spec((1, GW)),
                  _const_spec((128, GW)), _const_spec((128, GW)), _const_spec((1, GW)), _const_spec((1, GW)),
                  _const_spec((1, GW))],
        out_specs=[head_out] * 7 + [pl.BlockSpec((1, tt, GW), lambda b, i: (b, i, 0))],
        out_shape=[head_shape] * 7 + [jax.ShapeDtypeStruct((B, T, GW), F32)],
        scratch_shapes=[pltpu.VMEM((1, dw), F32)],
        compiler_params=_cparams(("parallel", "arbitrary")),
        name="rwkv_prep",
    )(z, z, z, z, z, z, shift, mu, w0, w2p, a0, a2p, g2p, kkp, kap, rk)


def _rwkv_scan_kernel(r_ref, w_ref, k_ref, v_ref, kk_ref, b_ref, bonus_ref, s0_ref, lg_ref, lb_ref,
                      y_ref, s_ref, *, nb, tt, n_steps):
    it = pl.program_id(1)

    @pl.when(it == 0)
    def _():
        s_ref[...] = s0_ref[...]

    y_ref[...] = jnp.zeros(y_ref.shape, F32)
    ii = lax.broadcasted_iota(jnp.int32, (HEAD_DIM, HEAD_DIM), 0)
    jj = lax.broadcasted_iota(jnp.int32, (HEAD_DIM, HEAD_DIM), 1)
    eye = jnp.where(ii == jj, 1.0, 0.0).astype(F32)

    def step(t, carry):
        for n in range(nb):
            for h in range(HEADS):
                row = lambda ref: ref[n, h, pl.ds(t, 1), :]
                S = s_ref[n, h]
                sa = jnp.sum(S * row(kk_ref), axis=-1, keepdims=True)
                vcol = jnp.sum(eye * row(v_ref), axis=-1, keepdims=True)
                S = S * row(w_ref) - sa * row(b_ref) + vcol * row(k_ref)
                s_ref[n, h] = S
                ycol = jnp.sum(S * row(r_ref), axis=-1, keepdims=True)
                y_ref[n, h, pl.ds(t, 1), :] = jnp.sum(eye * ycol, axis=0, keepdims=True)
        return carry

    lax.fori_loop(0, n_steps, step, 0)

    for n in range(nb):
        for h in range(HEADS):
            y = y_ref[n, h]
            mu = jnp.mean(y, axis=-1, keepdims=True)
            yc = y - mu
            var = jnp.mean(yc * yc, axis=-1, keepdims=True)
            y_ref[n, h] = yc * lax.rsqrt(var + RWKV_GN_EPS) * lg_ref[h:h + 1, :] + lb_ref[h:h + 1, :] + bonus_ref[n, h]


def _rwkv_scan(r, w, k, v, kk, b, bonus, s0, lnx_g, lnx_b, t_real):
    B, H, T, _ = r.shape
    nb = 2
    tt = min(128, T)
    nt = T // tt
    n_steps = min(tt, t_real)
    blk = pl.BlockSpec((nb, H, tt, HEAD_DIM), lambda b_, i: (b_, 0, i, 0))
    sblk = pl.BlockSpec((nb, H, HEAD_DIM, HEAD_DIM), lambda b_, i: (b_, 0, 0, 0))
    return pl.pallas_call(
        functools.partial(_rwkv_scan_kernel, nb=nb, tt=tt, n_steps=n_steps),
        grid=(B // nb, nt),
        in_specs=[blk] * 7 + [sblk, _const_spec((H, HEAD_DIM)), _const_spec((H, HEAD_DIM))],
        out_specs=[blk, sblk],
        out_shape=[jax.ShapeDtypeStruct((B, H, T, HEAD_DIM), F32),
                   jax.ShapeDtypeStruct((B, H, HEAD_DIM, HEAD_DIM), F32)],
        compiler_params=_cparams(("parallel", "arbitrary")),
        name="rwkv_scan",
    )(r, w, k, v, kk, b, bonus, s0, lnx_g, lnx_b)


def _compress_tail(F, cst):
    n = F.shape[0]
    nxt = pltpu.roll(F[:, HEAD_DIM:2 * HEAD_DIM], n - 1, 0)
    const = cst[0:1, 0:HEAD_DIM] + cst[1:2, HEAD_DIM:2 * HEAD_DIM]
    return F[:, 0:HEAD_DIM] + nxt + const


def _compress_kernel(x_ref, w_ref, pe_ref, o_ref):
    F = _dot3(x_ref[0, 0, 0], w_ref[0])
    cst = _dot3(pe_ref[0], w_ref[0])
    o_ref[0, 0, 0] = _compress_tail(F, cst)


def _compress_prompt(cmp_in, wcat, pe2):
    B, _, _, nc, _ = cmp_in.shape
    return pl.pallas_call(
        _compress_kernel,
        grid=(B, 2, 2),
        in_specs=[pl.BlockSpec((1, 1, 1, nc, 1024), lambda b, a, g: (b, a, g, 0, 0)),
                  pl.BlockSpec((1, 1024, 128), lambda b, a, g: (a, 0, 0)),
                  pl.BlockSpec((1, 8, 1024), lambda b, a, g: (a, 0, 0))],
        out_specs=pl.BlockSpec((1, 1, 1, nc, HEAD_DIM), lambda b, a, g: (b, a, g, 0, 0)),
        out_shape=jax.ShapeDtypeStruct((B, 2, 2, nc, HEAD_DIM), F32),
        compiler_params=_cparams(("parallel", "parallel", "parallel")),
        name="nsa_compress",
    )(cmp_in, wcat, pe2)


def _rank_select(selT, tie_lt, rank_ref, n_blocks):
    for j in range(n_blocks):
        rj = selT[j:j + 1, :]
        tie = tie_lt(j)
        gt = jnp.where(selT > rj, 1.0, jnp.where(selT == rj, tie, 0.0))
        rank_ref[j:j + 1, :] = jnp.sum(gt, axis=0, keepdims=True)
    return jnp.where(rank_ref[...] < float(N_SLC), 1.0, 0.0)


def _gated_heads(gts, g, o_cmp, o_slc, o_win, rows):
    out = []
    for m in range(Q_PER_KV):
        c0 = g * 3 * Q_PER_KV + m * 3
        sl = slice(m * rows, (m + 1) * rows)
        out.append(gts[:, c0:c0 + 1] * o_cmp[sl] + gts[:, c0 + 1:c0 + 2] * o_slc[sl] + gts[:, c0 + 2:c0 + 3] * o_win[sl])
    return out


_KT = 512


def _nsa_prompt_kernel(q_ref, gt_ref, ckv_ref, ksT_ref, vs_ref, kwT_ref, vw_ref, mt_ref, o_ref, rank_ref, *, T):
    i = pl.program_id(1)
    start = i * Q_BLOCK
    nc = T // CMP_STRIDE
    ns = T // SLC_BLOCK
    t_tok = start + lax.broadcasted_iota(jnp.int32, (Q_BLOCK, 1), 0)
    t_rows = jnp.concatenate([t_tok] * Q_PER_KV, axis=0)
    gts = _sigmoid(gt_ref[0])
    j_col = lax.broadcasted_iota(jnp.int32, (ns, 1), 0)
    t_lane = start + lax.broadcasted_iota(jnp.int32, (1, Q_BLOCK), 1)
    cur = t_lane >> 6
    forced = jnp.where(j_col == 0, 1.0, jnp.where(j_col == cur, 1.0, jnp.where(j_col == cur - 1, 1.0, 0.0)))
    r128 = lax.broadcasted_iota(jnp.int32, (Q_BLOCK, Q_BLOCK), 0)
    c128 = lax.broadcasted_iota(jnp.int32, (Q_BLOCK, Q_BLOCK), 1)
    eye128 = jnp.where(r128 == c128, 1.0, 0.0).astype(BF16)
    pieces = []
    for g in range(KV_HEADS):
        Q = jnp.concatenate([q_ref[0, :, (g * Q_PER_KV + m) * HEAD_DIM:(g * Q_PER_KV + m + 1) * HEAD_DIM]
                             for m in range(Q_PER_KV)], axis=0) * (HEAD_DIM ** -0.5)
        Qb = Q.astype(BF16)
        sc = _dot3(Q, ckv_ref[0, 0, g], _NT)
        n_lane = lax.broadcasted_iota(jnp.int32, (1, nc), 1)
        mask_c = jnp.where(n_lane * CMP_STRIDE + (2 * CMP_STRIDE - 1) <= t_rows, 1.0, 0.0)
        pc = _masked_softmax(sc, mask_c)
        o_cmp = _dot(pc.astype(BF16), ckv_ref[0, 1, g].astype(BF16))
        ps = pc[0:Q_BLOCK] + pc[Q_BLOCK:2 * Q_BLOCK] + pc[2 * Q_BLOCK:3 * Q_BLOCK] + pc[3 * Q_BLOCK:4 * Q_BLOCK]
        selT = _dot_exact_lhs(mt_ref[...], ps, _NT)
        selT = jnp.where(forced > 0.5, FORCED, jnp.where(j_col * SLC_BLOCK <= t_lane, selT, -FORCED))
        selmT = _rank_select(selT, lambda j: jnp.where(j_col < j, 1.0, 0.0), rank_ref, ns)
        selm = _dot(eye128, selmT.astype(BF16), _NT).astype(BF16)
        nkt = (start + Q_BLOCK + _KT - 1) // _KT

        def body(kt, carry):
            m_run, l_run, acc = carry
            s = _dot(Qb, ksT_ref[0, g, kt])
            jr = lax.broadcasted_iota(jnp.int32, (ns, _KT), 0)
            cl = lax.broadcasted_iota(jnp.int32, (ns, _KT), 1)
            expand = jnp.where(jr == kt * (_KT // SLC_BLOCK) + (cl >> 6), 1.0, 0.0).astype(BF16)
            mv = _dot(selm, expand)
            mv4 = jnp.concatenate([mv] * Q_PER_KV, axis=0)
            pos = kt * _KT + lax.broadcasted_iota(jnp.int32, (1, _KT), 1)
            ok = jnp.where(pos <= t_rows, mv4, 0.0)
            s = jnp.where(ok > 0.5, s, NEG)
            m_new = jnp.maximum(m_run, jnp.max(s, axis=-1, keepdims=True))
            p = jnp.exp(s - m_new) * ok
            corr = jnp.exp(m_run - m_new)
            l_new = corr * l_run + jnp.sum(p, axis=-1, keepdims=True)
            vt = vs_ref[0, g, pl.ds(pl.multiple_of(kt * _KT, _KT), _KT), :]
            acc = corr * acc + _dot(p.astype(BF16), vt)
            return m_new, l_new, acc

        init = (jnp.full((Q_PER_KV * Q_BLOCK, 1), NEG, F32), jnp.zeros((Q_PER_KV * Q_BLOCK, 1), F32),
                jnp.zeros((Q_PER_KV * Q_BLOCK, HEAD_DIM), F32))
        _, l_fin, acc = lax.fori_loop(0, nkt, body, init)
        o_slc = acc / jnp.maximum(l_fin, 1e-30)
        nwt = WINDOW // Q_BLOCK + 1
        wb = jnp.maximum(i - WINDOW // Q_BLOCK, 0)
        sw = jnp.concatenate([_dot(Qb, kwT_ref[0, g, wb + u]) for u in range(nwt)], axis=1)
        spos = wb * Q_BLOCK + lax.broadcasted_iota(jnp.int32, (1, nwt * Q_BLOCK), 1)
        dt = t_rows - spos
        mask_w = jnp.where(dt >= 0, jnp.where(dt < WINDOW, 1.0, 0.0), 0.0)
        pw = _masked_softmax(sw, mask_w)
        vwt = vw_ref[0, g, pl.ds(pl.multiple_of(wb * Q_BLOCK, Q_BLOCK), nwt * Q_BLOCK), :]
        o_win = _dot(pw.astype(BF16), vwt)
        pieces += _gated_heads(gts, g, o_cmp, o_slc, o_win, Q_BLOCK)
    o_ref[0] = jnp.concatenate(pieces, axis=1)


def _sel_matrix(n_rows, n_chunks):
    j = np.arange(n_rows)[:, None]
    c = np.arange(n_chunks)[None, :]
    per = SLC_BLOCK // CMP_STRIDE
    m = 0.5 * ((c // per == j).astype(np.float32) + ((c + 1) // per == j).astype(np.float32))
    return jnp.asarray(m, BF16)


def _nsa_prompt(z, ckv, ksT, vs, kwT, vw):
    B, T, _ = z.shape
    nc, ns = T // CMP_STRIDE, T // SLC_BLOCK
    nq = T // Q_BLOCK
    mt = _sel_matrix(ns, nc)
    per_b = lambda shape: pl.BlockSpec((1,) + shape, lambda b, i: (b,) + (0,) * len(shape))
    return pl.pallas_call(
        functools.partial(_nsa_prompt_kernel, T=T),
        grid=(B, nq),
        in_specs=[pl.BlockSpec((1, Q_BLOCK, GW), lambda b, i: (b, i, C_Q // GW)),
                  pl.BlockSpec((1, Q_BLOCK, 256), lambda b, i: (b, i, C_GATES // 256)),
                  per_b((2, KV_HEADS, nc, HEAD_DIM)),
                  per_b((KV_HEADS, T // _KT, HEAD_DIM, _KT)),
                  per_b((KV_HEADS, T, HEAD_DIM)),
                  per_b((KV_HEADS, nq, HEAD_DIM, Q_BLOCK)),
                  per_b((KV_HEADS, T, HEAD_DIM)),
                  _const_spec((ns, nc))],
        out_specs=pl.BlockSpec((1, Q_BLOCK, GW), lambda b, i: (b, i, 0)),
        out_shape=jax.ShapeDtypeStruct((B, T, GW), F32),
        scratch_shapes=[pltpu.VMEM((ns, Q_BLOCK), F32)],
        compiler_params=_cparams(("parallel", "arbitrary")),
        name="nsa_prompt",
    )(z, z, ckv, ksT, vs, kwT, vw, mt)


_PPS = 8
_SEL_ROWS = 384


def _nsa_sample_a_kernel(pt_ref, *refs, n_pages, past_len, t_pad):
    pages = refs[:_PPS]
    w_ref, pe_ref, q_ref, mt_ref = refs[_PPS:_PPS + 4]
    ocmp_ref, selm_ref = refs[_PPS + 4:_PPS + 6]
    x_ref, rank_ref = refs[_PPS + 6:]
    s = pl.program_id(1)
    cpp = PAGE // CMP_STRIDE
    for k in range(_PPS):
        x_ref[:, pl.ds(pl.multiple_of((s * _PPS + k) * cpp, cpp), cpp), :] = pages[k][0]

    @pl.when(s == pl.num_programs(1) - 1)
    def _():
        nc = n_pages * cpp
        n_blocks = past_len // SLC_BLOCK + 1
        lanes = 128
        ps_rows = []
        for g in range(KV_HEADS):
            ck = _compress_tail(_dot3(x_ref[g], w_ref[0]), _dot3(pe_ref[0], w_ref[0]))
            cv = _compress_tail(_dot3(x_ref[KV_HEADS + g], w_ref[1]), _dot3(pe_ref[1], w_ref[1]))
            Q = jnp.concatenate([q_ref[0, :, (g * Q_PER_KV + m) * HEAD_DIM:(g * Q_PER_KV + m + 1) * HEAD_DIM]
                                 for m in range(Q_PER_KV)], axis=0) * (HEAD_DIM ** -0.5)
            t_rows = past_len + (lax.broadcasted_iota(jnp.int32, (Q_PER_KV * t_pad, 1), 0) & (t_pad - 1))
            sc = _dot3(Q, ck, _NT)
            n_lane = lax.broadcasted_iota(jnp.int32, (1, nc), 1)
            mask_c = jnp.where(n_lane * CMP_STRIDE + (2 * CMP_STRIDE - 1) <= t_rows, 1.0, 0.0)
            pc = _masked_softmax(sc, mask_c)
            ocmp_ref[0, g] = _dot(pc.astype(BF16), cv.astype(BF16))
            ps_rows.append(pc[0:t_pad] + pc[t_pad:2 * t_pad] + pc[2 * t_pad:3 * t_pad] + pc[3 * t_pad:4 * t_pad])
        ps = jnp.concatenate(ps_rows + [jnp.zeros((lanes - KV_HEADS * t_pad, nc), F32)], axis=0)
        selT = _dot_exact_lhs(mt_ref[...], ps, _NT)
        j_col = lax.broadcasted_iota(jnp.int32, (_SEL_ROWS, 1), 0)
        t_lane = past_len + (lax.broadcasted_iota(jnp.int32, (1, lanes), 1) & (t_pad - 1))
        cur = t_lane >> 6
        forced = jnp.where(j_col == 0, 1.0, jnp.where(j_col == cur, 1.0, jnp.where(j_col == cur - 1, 1.0, 0.0)))
        selT = jnp.where(forced > 0.5, FORCED, jnp.where(j_col * SLC_BLOCK <= t_lane, selT, -FORCED))
        selT = jnp.where(j_col < n_blocks, selT, -3.0 * FORCED)
        rank_ref[...] = jnp.full(rank_ref.shape, float(_SEL_ROWS), F32)
        selmT = _rank_select(selT, lambda j: jnp.where(j_col < j, 1.0, 0.0), rank_ref, n_blocks)
        r = lax.broadcasted_iota(jnp.int32, (lanes, lanes), 0)
        c = lax.broadcasted_iota(jnp.int32, (lanes, lanes), 1)
        eye = jnp.where(r == c, 1.0, 0.0).astype(BF16)
        selm = _dot(eye, selmT.astype(BF16), _NT)
        selm_ref[0] = selm[0:KV_HEADS * t_pad]


def _nsa_sample_a(page_table, cmp_pages, wcat, pe2, z, past_len):
    B, t_pad, _ = z.shape
    n_pages = page_table.shape[1]
    nc = n_pages * (PAGE // CMP_STRIDE)
    mt = _sel_matrix(_SEL_ROWS, nc)
    page_spec = lambda k: pl.BlockSpec((1, 2 * KV_HEADS, PAGE // CMP_STRIDE, 1024),
                                       lambda b, s, pt: (pt[b, s * _PPS + k], 0, 0, 0))
    cst = lambda shape: pl.BlockSpec(shape, lambda b, s, pt: (0,) * len(shape))
    grid_spec = pltpu.PrefetchScalarGridSpec(
        num_scalar_prefetch=1,
        grid=(B, n_pages // _PPS),
        in_specs=[page_spec(k) for k in range(_PPS)] + [
            cst((2, 1024, 128)), cst((2, 8, 1024)),
            pl.BlockSpec((1, t_pad, GW), lambda b, s, pt: (b, 0, C_Q // GW)),
            cst((_SEL_ROWS, nc))],
        out_specs=[pl.BlockSpec((1, KV_HEADS, Q_PER_KV * t_pad, HEAD_DIM), lambda b, s, pt: (b, 0, 0, 0)),
                   pl.BlockSpec((1, KV_HEADS * t_pad, _SEL_ROWS), lambda b, s, pt: (b, 0, 0))],
        scratch_shapes=[pltpu.VMEM((2 * KV_HEADS, nc, 1024), F32), pltpu.VMEM((_SEL_ROWS, 128), F32)])
    return pl.pallas_call(
        functools.partial(_nsa_sample_a_kernel, n_pages=n_pages, past_len=past_len, t_pad=t_pad),
        grid_spec=grid_spec,
        out_shape=[jax.ShapeDtypeStruct((B, KV_HEADS, Q_PER_KV * t_pad, HEAD_DIM), F32),
                   jax.ShapeDtypeStruct((B, KV_HEADS * t_pad, _SEL_ROWS), F32)],
        compiler_params=_cparams(("parallel", "arbitrary")),
        name="nsa_sample_select",
    )(page_table, *([cmp_pages] * _PPS), wcat, pe2, z, mt)


def _nsa_sample_b_kernel(pt_ref, *refs, n_pages, past_len, t_pad, t_real):
    kpages = refs[:_PPS]
    vpages = refs[_PPS:2 * _PPS]
    q_ref, rows_ref, kwvw_ref, gt_ref, selm_ref, ocmp_ref, cwk_ref, cwv_ref = refs[2 * _PPS:2 * _PPS + 8]
    o_ref = refs[2 * _PPS + 8]
    m_ref, l_ref, acc_ref = refs[2 * _PPS + 9:]
    s = pl.program_id(1)
    n_steps = n_pages // _PPS
    rows = Q_PER_KV * t_pad
    t_rows = past_len + (lax.broadcasted_iota(jnp.int32, (rows, 1), 0) & (t_pad - 1))

    @pl.when(s == 0)
    def _():
        m_ref[...] = jnp.full(m_ref.shape, NEG, F32)
        l_ref[...] = jnp.zeros(l_ref.shape, F32)
        acc_ref[...] = jnp.zeros(acc_ref.shape, F32)

    def q_rows(g):
        Q = jnp.concatenate([q_ref[0, :, (g * Q_PER_KV + m) * HEAD_DIM:(g * Q_PER_KV + m + 1) * HEAD_DIM]
                             for m in range(Q_PER_KV)], axis=0) * (HEAD_DIM ** -0.5)
        return Q.astype(BF16)

    def online(g, sc, ok, values):
        sc = jnp.where(ok > 0.5, sc, NEG)
        m_run = m_ref[g]
        m_new = jnp.maximum(m_run, jnp.max(sc, axis=-1, keepdims=True))
        p = jnp.exp(sc - m_new) * ok
        corr = jnp.exp(m_run - m_new)
        l_ref[g] = corr * l_ref[g] + jnp.sum(p, axis=-1, keepdims=True)
        acc_ref[g] = corr * acc_ref[g] + values(p.astype(BF16))
        m_ref[g] = m_new

    @pl.when(s < n_steps)
    def _():
        keys = _PPS * PAGE
        jr = lax.broadcasted_iota(jnp.int32, (_SEL_ROWS, keys), 0)
        cl = lax.broadcasted_iota(jnp.int32, (_SEL_ROWS, keys), 1)
        expand = jnp.where(jr == s * (keys // SLC_BLOCK) + (cl >> 6), 1.0, 0.0).astype(BF16)
        mv = _dot(selm_ref[0].astype(BF16), expand)
        pos = s * keys + lax.broadcasted_iota(jnp.int32, (1, keys), 1)
        for g in range(KV_HEADS):
            Qb = q_rows(g)
            sc = jnp.concatenate([_dot(Qb, kpages[k][0, g]) for k in range(_PPS)], axis=1)
            mvg = jnp.concatenate([mv[g * t_pad:(g + 1) * t_pad]] * Q_PER_KV, axis=0)
            ok = jnp.where(pos <= t_rows, mvg, 0.0)

            def values(pb, g=g):
                out = _dot(pb[:, 0:PAGE], vpages[0][0, g])
                for k in range(1, _PPS):
                    out = out + _dot(pb[:, k * PAGE:(k + 1) * PAGE], vpages[k][0, g])
                return out

            online(g, sc, ok, values)

    @pl.when(s == n_steps)
    def _():
        gts = _sigmoid(gt_ref[0])
        last_blk = past_len // SLC_BLOCK
        r_lane = lax.broadcasted_iota(jnp.int32, (1, t_pad), 1)
        pieces = []
        for g in range(KV_HEADS):
            Qb = q_rows(g)
            k_new = rows_ref[0, :, (2 * KV_HEADS + g) * HEAD_DIM:(2 * KV_HEADS + g + 1) * HEAD_DIM]
            v_new = rows_ref[0, :, (3 * KV_HEADS + g) * HEAD_DIM:(3 * KV_HEADS + g + 1) * HEAD_DIM]
            sc = _dot(Qb, k_new.astype(BF16), _NT)
            sel_last = jnp.concatenate([selm_ref[0, g * t_pad:(g + 1) * t_pad, last_blk:last_blk + 1]] * Q_PER_KV, axis=0)
            vis = jnp.where(past_len + r_lane <= t_rows, jnp.where(r_lane < t_real, 1.0, 0.0), 0.0)
            online(g, sc, vis * sel_last, lambda pb, v_new=v_new: _dot(pb, v_new.astype(BF16)))
            o_slc = acc_ref[g] / jnp.maximum(l_ref[g], 1e-30)
            wrows = cwk_ref.shape[2]
            kw_new = kwvw_ref[0, :, g * HEAD_DIM:(g + 1) * HEAD_DIM]
            vw_new = kwvw_ref[0, :, (KV_HEADS + g) * HEAD_DIM:(KV_HEADS + g + 1) * HEAD_DIM]
            sw = jnp.concatenate([_dot(Qb, cwk_ref[0, g].astype(BF16), _NT),
                                  _dot(Qb, kw_new.astype(BF16), _NT)], axis=1)
            idx = lax.broadcasted_iota(jnp.int32, (1, wrows + t_pad), 1)
            spos = past_len - wrows + idx
            dt = t_rows - spos
            real = jnp.where(idx < wrows + t_real, 1.0, 0.0)
            mask_w = jnp.where(dt >= 0, jnp.where(dt < WINDOW, real, 0.0), 0.0)
            pw = _masked_softmax(sw, mask_w).astype(BF16)
            o_win = _dot(pw[:, 0:wrows], cwv_ref[0, g].astype(BF16)) + _dot(pw[:, wrows:], vw_new.astype(BF16))
            pieces += _gated_heads(gts, g, ocmp_ref[0, g], o_slc, o_win, t_pad)
        o_ref[0] = jnp.concatenate(pieces, axis=1)


def _nsa_sample_b(page_table, ksT_pages, vs_pages, z, selm, ocmp, cwk, cwv, past_len, t_real):
    B, t_pad, _ = z.shape
    n_pages = page_table.shape[1]
    n_steps = n_pages // _PPS
    rows = Q_PER_KV * t_pad
    wrows = cwk.shape[2]
    pidx = lambda b, s, pt, k: pt[b, jnp.minimum(s, n_steps - 1) * _PPS + k]
    kspec = lambda k: pl.BlockSpec((1, KV_HEADS, HEAD_DIM, PAGE), lambda b, s, pt: (pidx(b, s, pt, k), 0, 0, 0))
    vspec = lambda k: pl.BlockSpec((1, KV_HEADS, PAGE, HEAD_DIM), lambda b, s, pt: (pidx(b, s, pt, k), 0, 0, 0))
    zspec = lambda col, w: pl.BlockSpec((1, t_pad, w), lambda b, s, pt: (b, 0, col // w))
    per_b = lambda shape: pl.BlockSpec((1,) + shape, lambda b, s, pt: (b,) + (0,) * len(shape))
    grid_spec = pltpu.PrefetchScalarGridSpec(
        num_scalar_prefetch=1,
        grid=(B, n_steps + 1),
        in_specs=[kspec(k) for k in range(_PPS)] + [vspec(k) for k in range(_PPS)] + [
            zspec(C_Q, GW), zspec(C_ROWS, GW), zspec(C_KW, 256), zspec(C_GATES, 256),
            per_b((KV_HEADS * t_pad, _SEL_ROWS)), per_b((KV_HEADS, rows, HEAD_DIM)),
            per_b((KV_HEADS, wrows, HEAD_DIM)), per_b((KV_HEADS, wrows, HEAD_DIM))],
        out_specs=pl.BlockSpec((1, t_pad, GW), lambda b, s, pt: (b, 0, 0)),
        scratch_shapes=[pltpu.VMEM((KV_HEADS, rows, 1), F32), pltpu.VMEM((KV_HEADS, rows, 1), F32),
                        pltpu.VMEM((KV_HEADS, rows, HEAD_DIM), F32)])
    return pl.pallas_call(
        functools.partial(_nsa_sample_b_kernel, n_pages=n_pages, past_len=past_len, t_pad=t_pad, t_real=t_real),
        grid_spec=grid_spec,
        out_shape=jax.ShapeDtypeStruct((B, t_pad, GW), F32),
        compiler_params=_cparams(("parallel", "arbitrary")),
        name="nsa_sample_attend",
    )(page_table, *([ksT_pages] * _PPS), *([vs_pages] * _PPS), z, z, z, z, selm, ocmp, cwk, cwv)


def _proj_out_kernel(x_ref, ya_ref, yb_ref, yc_ref, yd_ref, g_ref, w_ref, b_ref, lg_ref, lb_ref, h_ref, *, alpha):
    yd = yd_ref[...] * g_ref[...]
    acc = _dot(ya_ref[...].astype(BF16), w_ref[0:GW, :])
    acc = acc + _dot(yb_ref[...].astype(BF16), w_ref[GW:2 * GW, :])
    acc = acc + _dot(yc_ref[...].astype(BF16), w_ref[2 * GW:3 * GW, :])
    acc = acc + _dot(yd.astype(BF16), w_ref[3 * GW:4 * GW, :])
    h_ref[...] = _layer_norm(alpha * x_ref[...] + acc + b_ref[...], lg_ref[...], lb_ref[...], LN_EPS)


def _proj_out(x2d, ya, yb, yc, yd, g, w_bf16, b, lg, lb, alpha):
    n = x2d.shape[0]
    tm = min(256, n)
    row = lambda w: pl.BlockSpec((tm, w), lambda i: (i, 0))
    return pl.pallas_call(
        functools.partial(_proj_out_kernel, alpha=alpha),
        grid=(n // tm,),
        in_specs=[row(D_MODEL), row(GW), row(GW), row(GW), row(GW), row(GW),
                  _const_spec((D_MODEL, D_MODEL)), _const_spec((1, D_MODEL)), _const_spec((1, D_MODEL)),
                  _const_spec((1, D_MODEL))],
        out_specs=row(D_MODEL),
        out_shape=jax.ShapeDtypeStruct((n, D_MODEL), F32),
        compiler_params=_cparams(("parallel",)),
        name="proj_out_ln1",
    )(x2d, ya, yb, yc, yd, g, w_bf16, b, lg, lb)


def _router_kernel(h_ref, w_ref, b_ref, o_ref):
    logits = _dot3(h_ref[...], w_ref[...]) + b_ref[...]
    lane = lax.broadcasted_iota(jnp.int32, logits.shape, 1).astype(F32)
    big = 1e6
    is_g = lane < N_GROUPS
    gl = jnp.where(is_g, logits, NEG)
    gmax = jnp.max(gl, axis=-1, keepdims=True)
    gsel = jnp.min(jnp.where(is_g, jnp.where(gl == gmax, lane, big), big), axis=-1, keepdims=True)
    gp = 1.0 / jnp.sum(jnp.where(is_g, jnp.exp(gl - gmax), 0.0), axis=-1, keepdims=True)
    lo = N_GROUPS + gsel * EXP_PER_GROUP
    in_grp = jnp.where(lane >= lo, jnp.where(lane < lo + EXP_PER_GROUP, 1.0, 0.0), 0.0)
    el = jnp.where(in_grp > 0.5, logits, NEG)
    e = jnp.exp(el - jnp.max(el, axis=-1, keepdims=True)) * in_grp
    p = e / jnp.sum(e, axis=-1, keepdims=True)
    pm = jnp.where(in_grp > 0.5, p, -1.0)
    v1 = jnp.max(pm, axis=-1, keepdims=True)
    i1 = jnp.min(jnp.where(pm == v1, lane, big), axis=-1, keepdims=True)
    pm2 = jnp.where(lane == i1, -1.0, pm)
    v2 = jnp.max(pm2, axis=-1, keepdims=True)
    i2 = jnp.min(jnp.where(pm2 == v2, lane, big), axis=-1, keepdims=True)
    tot = v1 + v2
    out = jnp.where(lane == 0, i1 - N_GROUPS,
                    jnp.where(lane == 1, i2 - N_GROUPS,
                              jnp.where(lane == 2, v1 / tot * gp, jnp.where(lane == 3, v2 / tot * gp, 0.0))))
    o_ref[...] = out


def _router(h, w_cat, b_cat):
    n = h.shape[0]
    tm = min(256, n)
    return pl.pallas_call(
        _router_kernel,
        grid=(n // tm,),
        in_specs=[pl.BlockSpec((tm, D_MODEL), lambda i: (i, 0)), _const_spec((D_MODEL, 128)), _const_spec((1, 128))],
        out_specs=pl.BlockSpec((tm, 128), lambda i: (i, 0)),
        out_shape=jax.ShapeDtypeStruct((n, 128), F32),
        compiler_params=_cparams(("parallel",)),
        name="moe_router",
    )(h, w_cat, b_cat)


def _gather_kernel(idx_ref, src_ref, o_ref, sem, *, tile):
    base = pl.program_id(0) * tile

    def row_copy(r):
        return pltpu.make_async_copy(src_ref.at[pl.ds(idx_ref[base + r], 1), :], o_ref.at[pl.ds(r, 1), :], sem)

    def start(r, c):
        row_copy(r).start()
        return c

    def wait(r, c):
        row_copy(r).wait()
        return c

    lax.fori_loop(0, tile, start, 0)
    lax.fori_loop(0, tile, wait, 0)


def _gather_rows(src, idx, tile):
    n_out = idx.shape[0]
    d = src.shape[1]
    grid_spec = pltpu.PrefetchScalarGridSpec(
        num_scalar_prefetch=1,
        grid=(n_out // tile,),
        in_specs=[pl.BlockSpec(memory_space=pl.ANY)],
        out_specs=pl.BlockSpec((tile, d), lambda i, idx_ref: (i, 0)),
        scratch_shapes=[pltpu.SemaphoreType.DMA(())])
    return pl.pallas_call(
        functools.partial(_gather_kernel, tile=tile),
        grid_spec=grid_spec,
        out_shape=jax.ShapeDtypeStruct((n_out, d), src.dtype),
        compiler_params=_cparams(("arbitrary",)),
        name="gather_rows",
    )(idx, src)


def _moe_kernel(te_ref, nu_ref, xs_ref, wg_ref, wu_ref, wd_ref, rw_ref, o_ref, wgb_ref, wub_ref, wdb_ref):
    t = pl.program_id(0)
    changed = jnp.logical_or(t == 0, te_ref[t] != te_ref[jnp.maximum(t - 1, 0)])

    @pl.when(changed)
    def _():
        wgb_ref[...] = wg_ref[0].astype(BF16)
        wub_ref[...] = wu_ref[0].astype(BF16)
        wdb_ref[...] = wd_ref[0].astype(BF16)

    @pl.when(t < nu_ref[0])
    def _():
        xb = xs_ref[...].astype(BF16)
        hg = _silu(_dot(xb, wgb_ref[...])) * _dot(xb, wub_ref[...]) * rw_ref[...]
        o_ref[...] = _dot(hg.astype(BF16), wdb_ref[...])

    @pl.when(t >= nu_ref[0])
    def _():
        o_ref[...] = jnp.zeros(o_ref.shape, F32)


def _moe_experts(tile_expert, n_used, xs, w_gate, w_up, w_down, row_w):
    r = xs.shape[0]
    tm = MOE_TILE
    wspec = lambda shape: pl.BlockSpec((1,) + shape, lambda t, te, nu: (te[t], 0, 0))
    grid_spec = pltpu.PrefetchScalarGridSpec(
        num_scalar_prefetch=2,
        grid=(r // tm,),
        in_specs=[pl.BlockSpec((tm, D_MODEL), lambda t, te, nu: (t, 0)),
                  wspec((D_MODEL, D_EXPERT)), wspec((D_MODEL, D_EXPERT)), wspec((D_EXPERT, D_MODEL)),
                  pl.BlockSpec((tm, 1), lambda t, te, nu: (t, 0))],
        out_specs=pl.BlockSpec((tm, D_MODEL), lambda t, te, nu: (t, 0)),
        scratch_shapes=[pltpu.VMEM((D_MODEL, D_EXPERT), BF16), pltpu.VMEM((D_MODEL, D_EXPERT), BF16),
                        pltpu.VMEM((D_EXPERT, D_MODEL), BF16)])
    return pl.pallas_call(
        _moe_kernel,
        grid_spec=grid_spec,
        out_shape=jax.ShapeDtypeStruct((r, D_MODEL), F32),
        compiler_params=_cparams(("arbitrary",)),
        name="moe_experts",
    )(tile_expert, n_used, xs, w_gate, w_up, w_down, row_w)


def _moe_schedule(eid, wts):
    n = eid.shape[0]
    tm = MOE_TILE
    n_rows = ((2 * n + N_EXPERTS * (tm - 1)) // tm + 1) * tm
    flat_e = eid.reshape(-1)
    order = jnp.argsort(flat_e, stable=True).astype(jnp.int32)
    e_sorted = flat_e[order]
    counts = jnp.sum((flat_e[:, None] == jnp.arange(N_EXPERTS, dtype=jnp.int32)[None, :]).astype(jnp.int32), axis=0)
    padded = (counts + tm - 1) // tm * tm
    pad_end = jnp.cumsum(padded)
    pos_sorted = jnp.arange(2 * n, dtype=jnp.int32) - (jnp.cumsum(counts) - counts)[e_sorted] + (pad_end - padded)[e_sorted]
    row_token = jnp.zeros((n_rows,), jnp.int32).at[pos_sorted].set(order // 2)
    row_w = jnp.zeros((n_rows,), F32).at[pos_sorted].set(wts.reshape(-1)[order])
    pos = jnp.zeros((2 * n,), jnp.int32).at[order].set(pos_sorted).reshape(n, 2)
    n_used = (pad_end[-1] // tm).astype(jnp.int32)
    tile_start = jnp.arange(n_rows // tm, dtype=jnp.int32) * tm
    tile_expert = jnp.minimum(jnp.searchsorted(pad_end, tile_start, side="right"), N_EXPERTS - 1).astype(jnp.int32)
    last_e = tile_expert[jnp.maximum(n_used - 1, 0)]
    tile_expert = jnp.where(tile_start // tm < n_used, tile_expert, last_e)
    return row_token, row_w.reshape(n_rows, 1), pos, tile_expert, n_used.reshape(1)


def _ln2_kernel(h_ref, y0_ref, y1_ref, g_ref, b_ref, o_ref, *, alpha):
    o_ref[...] = _layer_norm(alpha * h_ref[...] + (y0_ref[...] + y1_ref[...]), g_ref[...], b_ref[...], LN_EPS)


def _combine_ln2(h, y01, g, b, alpha):
    n = h.shape[0]
    tm = min(256, n)
    nt = n // tm
    return pl.pallas_call(
        functools.partial(_ln2_kernel, alpha=alpha),
        grid=(nt,),
        in_specs=[pl.BlockSpec((tm, D_MODEL), lambda i: (i, 0)),
                  pl.BlockSpec((tm, D_MODEL), lambda i: (i, 0)),
                  pl.BlockSpec((tm, D_MODEL), lambda i: (i + nt, 0)),
                  _const_spec((1, D_MODEL)), _const_spec((1, D_MODEL))],
        out_specs=pl.BlockSpec((tm, D_MODEL), lambda i: (i, 0)),
        out_shape=jax.ShapeDtypeStruct((n, D_MODEL), F32),
        compiler_params=_cparams(("parallel",)),
        name="combine_ln2",
    )(h, y01, y01, g, b)


def _relayout_cols(w):
    pad = lambda n: jnp.zeros(w.shape[:-1] + (n,), w.dtype)
    d0 = 3864
    return jnp.concatenate([
        w[..., 0:2328], pad(C_BG - 2328),
        w[..., 2328:3864],
        w[..., d0:d0 + 1536],
        w[..., d0 + 1536:d0 + 1568], pad(96),
        w[..., d0 + 1568:d0 + 1600], pad(96),
        w[..., d0 + 1600:d0 + 1696], pad(N_Z - C_GL - 96)], axis=-1)


def _relayout_d(v):
    pad = lambda n: jnp.zeros(v.shape[:-1] + (n,), v.dtype)
    return jnp.concatenate([v[..., 0:1536], v[..., 1536:1568], pad(96), v[..., 1568:1600], pad(96),
                            v[..., 1600:1696], pad(32)], axis=-1)


def _shift_cols(zrow):
    return jnp.concatenate([zrow[..., C_R:C_R + 1536], zrow[..., C_WL:C_WL + 32], zrow[..., C_AL:C_AL + 32],
                            zrow[..., C_GL:C_GL + 96]], axis=-1)


def _pad_rows(w, rows):
    return jnp.concatenate([w, jnp.zeros((rows - w.shape[0],) + w.shape[1:], w.dtype)], axis=0)


def _layer_params(l, P):
    wc = P["nsa_wc"][l]
    half = 2 * CMP_STRIDE // 2
    wcat = jnp.concatenate([wc[:, :half].reshape(2, half * HEAD_DIM, HEAD_DIM),
                            wc[:, half:].reshape(2, half * HEAD_DIM, HEAD_DIM)], axis=-1)
    pe = P["nsa_pe"][l]
    pe2 = jnp.stack([pe[:, :half].reshape(2, half * HEAD_DIM), pe[:, half:].reshape(2, half * HEAD_DIM)], axis=1)
    pe2 = jnp.concatenate([pe2, jnp.zeros((2, 6, half * HEAD_DIM), F32)], axis=1)
    router_w = jnp.concatenate([P["moe_w_grp"][l], P["moe_w_rte"][l],
                                jnp.zeros((D_MODEL, 128 - N_GROUPS - N_EXPERTS), F32)], axis=1)
    router_b = jnp.concatenate([P["moe_b_grp"][l], P["moe_b_rte"][l],
                                jnp.zeros((128 - N_GROUPS - N_EXPERTS,), F32)])[None, :]
    row = lambda name: P[name][l][None, :]
    return dict(
        w_in=_relayout_cols(P["w_in"][l]).astype(BF16), b_in=_relayout_cols(P["b_in"][l])[None, :],
        w_out=P["w_out"][l].astype(BF16), b_out=row("b_out"),
        ln1_g=row("ln1_g"), ln1_b=row("ln1_b"), ln2_g=row("ln2_g"), ln2_b=row("ln2_b"),
        conv_a_w=P["conv_a_w"][l], conv_a_b=row("conv_a_b"), ln_a_g=row("ln_a_g"), ln_a_b=row("ln_a_b"),
        conv_c_w=P["conv_c_w"][l], wcat=wcat, pe2=pe2,
        rw_mu=_relayout_d(P["rw_mu"][l])[None, :], rw_w0=row("rw_w0"), rw_w2=_pad_rows(P["rw_w2"][l], 128),
        rw_a0=row("rw_a0"), rw_a2=_pad_rows(P["rw_a2"][l], 128), rw_g2=_pad_rows(P["rw_g2"][l], 128),
        rw_kk=row("rw_kk"), rw_ka=row("rw_ka"), rw_rk=P["rw_rk"][l].reshape(1, GW),
        rw_lnx_g=P["rw_lnx_g"][l].reshape(HEADS, HEAD_DIM), rw_lnx_b=P["rw_lnx_b"][l].reshape(HEADS, HEAD_DIM),
        router_w=router_w, router_b=router_b,
        w_gate=P["moe_w_gate"][l], w_up=P["moe_w_up"][l], w_down=P["moe_w_down"][l])


def _mix_rwkv(z, shift, s0, lp, t_real):
    B, T, _ = z.shape
    r, w, k, v, kk, b, bonus, g = _rwkv_prep(z, shift, lp["rw_mu"], lp["rw_w0"], lp["rw_w2"], lp["rw_a0"],
                                             lp["rw_a2"], lp["rw_g2"], lp["rw_kk"], lp["rw_ka"], lp["rw_rk"], t_real)
    y, s_new = _rwkv_scan(r, w, k, v, kk, b, bonus, s0, lp["rw_lnx_g"], lp["rw_lnx_b"], t_real)
    yd = jnp.transpose(y, (0, 2, 1, 3)).reshape(B, T, GW)
    return yd, g, s_new


def _nsa_prompt_group(z, lp):
    B, T, _ = z.shape
    kv6 = z[:, :, C_ROWS:C_GATES].reshape(B, T, 6, KV_HEADS, HEAD_DIM)
    nc = T // CMP_STRIDE
    cmp_in = kv6[:, :, 0:2].reshape(B, nc, CMP_STRIDE, 2, KV_HEADS, HEAD_DIM)
    cmp_in = jnp.transpose(cmp_in, (0, 3, 4, 1, 2, 5)).reshape(B, 2, KV_HEADS, nc, CMP_STRIDE * HEAD_DIM)
    ckv = _compress_prompt(cmp_in, lp["wcat"], lp["pe2"])
    keyT = lambda x, tile: jnp.transpose(x.astype(BF16).reshape(B, T // tile, tile, KV_HEADS, HEAD_DIM), (0, 3, 1, 4, 2))
    val = lambda x: jnp.transpose(x.astype(BF16), (0, 2, 1, 3))
    yb = _nsa_prompt(z, ckv, keyT(kv6[:, :, 2], _KT), val(kv6[:, :, 3]), keyT(kv6[:, :, 4], Q_BLOCK), val(kv6[:, :, 5]))
    rows = kv6[:, :, 0:4]
    win = kv6[:, T - min(WINDOW, T):, 4:6]
    return yb, rows, win


def _nsa_sample_group(z, cache_l, page_table, win_l, lp, t_real):
    B, t_pad, _ = z.shape
    n_pool = cache_l.shape[0]
    past_len = page_table.shape[1] * PAGE
    cpp = PAGE // CMP_STRIDE
    cmp_pages = cache_l[:, :, 0:2].reshape(n_pool, cpp, CMP_STRIDE, 2, KV_HEADS, HEAD_DIM)
    cmp_pages = jnp.transpose(cmp_pages, (0, 3, 4, 1, 2, 5)).reshape(n_pool, 2 * KV_HEADS, cpp, CMP_STRIDE * HEAD_DIM)
    ksT_pages = jnp.transpose(cache_l[:, :, 2].astype(BF16), (0, 2, 3, 1))
    vs_pages = jnp.transpose(cache_l[:, :, 3].astype(BF16), (0, 2, 1, 3))
    ocmp, selm = _nsa_sample_a(page_table, cmp_pages, lp["wcat"], lp["pe2"], z, past_len)
    cwk = jnp.transpose(win_l[:, :, 0], (0, 2, 1, 3))
    cwv = jnp.transpose(win_l[:, :, 1], (0, 2, 1, 3))
    yb = _nsa_sample_b(page_table, ksT_pages, vs_pages, z, selm, ocmp, cwk, cwv, past_len, t_real)
    zr = z[:, :t_real]
    rows = zr[:, :, C_ROWS:C_KW].reshape(B, t_real, 4, KV_HEADS, HEAD_DIM)
    new_win = zr[:, :, C_KW:C_GATES].reshape(B, t_real, 2, KV_HEADS, HEAD_DIM)
    win = jnp.concatenate([win_l, new_win], axis=1)[:, t_real:]
    return yb, rows, win


def _group_layer(x, lp, alpha, nsa_fn, conv_a_buf, conv_c_buf, rw_state, rw_shift, t_real):
    B, T, _ = x.shape
    z = _proj_in(x.reshape(B * T, D_MODEL), lp["w_in"], lp["b_in"]).reshape(B, T, N_Z)
    ya, yc, conv_a_new, conv_c_new = _conv_mix(z, conv_a_buf, conv_c_buf, lp["conv_a_w"], lp["conv_a_b"],
                                               lp["ln_a_g"], lp["ln_a_b"], lp["conv_c_w"], t_real)
    yb, nsa_rows, win_new = nsa_fn(z)
    yd, g, rw_state_new = _mix_rwkv(z, _relayout_d(rw_shift)[:, None, :], rw_state, lp, t_real)
    flat = lambda a: a.reshape(B * T, a.shape[-1])
    h = _proj_out(flat(x), flat(ya), flat(yb), flat(yc), flat(yd), flat(g), lp["w_out"], lp["b_out"],
                  lp["ln1_g"], lp["ln1_b"], alpha)
    shift_new = _shift_cols(z[:, t_real - 1])
    return h, (nsa_rows, win_new, conv_a_new, conv_c_new, rw_state_new, shift_new)


def _moe_ln2(h_list, lp, alpha):
    routed = [_router(h, lp["router_w"], lp["router_b"]) for h in h_list]
    r_all = jnp.concatenate(routed, axis=0)
    eid = r_all[:, 0:2].astype(jnp.int32)
    wts = r_all[:, 2:4]
    h_all = jnp.concatenate(h_list, axis=0)
    row_token, row_w, pos, tile_expert, n_used = _moe_schedule(eid, wts)
    xs = _gather_rows(h_all, row_token, MOE_TILE)
    ys = _moe_experts(tile_expert, n_used, xs, lp["w_gate"], lp["w_up"], lp["w_down"], row_w)
    outs = []
    off = 0
    for h in h_list:
        n = h.shape[0]
        p = pos[off:off + n]
        y01 = _gather_rows(ys, jnp.concatenate([p[:, 0], p[:, 1]]), min(256, n))
        outs.append(_combine_ln2(h, y01, lp["ln2_g"], lp["ln2_b"], alpha))
        off += n
    return outs


def kernel(x_prompt, x_sample, cache_nsa, cache_win, state_conv_a, state_conv_c, state_rwkv, state_rwkv_shift, page_table, w_in, b_in, w_out, b_out, ln1_g, ln1_b, ln2_g, ln2_b, conv_a_w, conv_a_b, ln_a_g, ln_a_b, conv_c_w, nsa_wc, nsa_pe, rw_mu, rw_w0, rw_w2, rw_a0, rw_a2, rw_g2, rw_kk, rw_ka, rw_rk, rw_lnx_g, rw_lnx_b, moe_w_grp, moe_b_grp, moe_w_rte, moe_b_rte, moe_w_gate, moe_w_up, moe_w_down):
    P = dict(w_in=w_in, b_in=b_in, w_out=w_out, b_out=b_out, ln1_g=ln1_g, ln1_b=ln1_b, ln2_g=ln2_g, ln2_b=ln2_b,
             conv_a_w=conv_a_w, conv_a_b=conv_a_b, ln_a_g=ln_a_g, ln_a_b=ln_a_b, conv_c_w=conv_c_w,
             nsa_wc=nsa_wc, nsa_pe=nsa_pe, rw_mu=rw_mu, rw_w0=rw_w0, rw_w2=rw_w2, rw_a0=rw_a0, rw_a2=rw_a2,
             rw_g2=rw_g2, rw_kk=rw_kk, rw_ka=rw_ka, rw_rk=rw_rk, rw_lnx_g=rw_lnx_g, rw_lnx_b=rw_lnx_b,
             moe_w_grp=moe_w_grp, moe_b_grp=moe_b_grp, moe_w_rte=moe_w_rte, moe_b_rte=moe_b_rte,
             moe_w_gate=moe_w_gate, moe_w_up=moe_w_up, moe_w_down=moe_w_down)
    depth = w_in.shape[0]
    alpha = (2.0 * depth) ** 0.25
    bp, tp, _ = x_prompt.shape
    bs, ts, _ = x_sample.shape
    ts_pad = 8
    xp = x_prompt
    xs = jnp.concatenate([x_sample, jnp.zeros((bs, ts_pad - ts, D_MODEL), F32)], axis=1)
    outs_p, outs_s = [], []
    for l in range(depth):
        lp = _layer_params(l, P)
        h_p, st_p = _group_layer(xp, lp, alpha, functools.partial(_nsa_prompt_group, lp=lp),
                                 jnp.zeros((bp, CONV_A_WIDTH - 1, GW), F32), jnp.zeros((bp, SHORT_CONV_WIDTH - 1, GW), F32),
                                 jnp.zeros((bp, HEADS, HEAD_DIM, HEAD_DIM), F32), jnp.zeros((bp, N_COLS_D), F32), tp)
        h_s, st_s = _group_layer(xs, lp, alpha,
                                 functools.partial(_nsa_sample_group, cache_l=cache_nsa[l], page_table=page_table,
                                                   win_l=cache_win[l], lp=lp, t_real=ts),
                                 state_conv_a[l], state_conv_c[l], state_rwkv[l], state_rwkv_shift[l], ts)
        o_p, o_s = _moe_ln2([h_p, h_s], lp, alpha)
        xp = o_p.reshape(bp, tp, D_MODEL)
        xs = o_s.reshape(bs, ts_pad, D_MODEL)
        outs_p.append(st_p)
        outs_s.append(st_s)
    stk = lambda outs, i: jnp.stack([o[i] for o in outs], axis=0)
    return (xp, xs[:, :ts], stk(outs_p, 0), stk(outs_s, 0), stk(outs_p, 1), stk(outs_s, 1), stk(outs_p, 2), stk(outs_s, 2),
            stk(outs_p, 3), stk(outs_s, 3), stk(outs_p, 4), stk(outs_s, 4), stk(outs_p, 5), stk(outs_s, 5))
```

```python
import functools

import numpy as np
import jax
import jax.numpy as jnp
from jax import lax
from jax.experimental import pallas as pl
from jax.experimental.pallas import tpu as pltpu

F32 = jnp.float32
BF16 = jnp.bfloat16

D_MODEL = 2048
DEPTH = 2
HEAD_DIM = 64
GW = 512
HEADS = 8
CONV_A_WIDTH = 31
SHORT_CONV_WIDTH = 3
KV_HEADS = 2
Q_PER_KV = 4
CMP_STRIDE = 16
SLC_BLOCK = 64
N_SLC = 16
WINDOW = 512
Q_BLOCK = 128
PAGE = 128
RWKV_GN_EPS = 64e-5
LN_EPS = 1e-5
NEG = -1e30
FORCED = 1e4
N_EXPERTS = 32
EXP_PER_GROUP = 8
N_GROUPS = 4
D_EXPERT = 512
N_IN = 5560
N_COLS_D = 1696

N_Z = 6144
C_VAL, C_GATE = 0, 512
C_Q, C_ROWS, C_KW, C_GATES = 1024, 1536, 2048, 2304
C_BG, C_CG, C_H = 2560, 3072, 3584
C_R, C_K, C_V = 4096, 4608, 5120
C_WL, C_AL, C_GL = 5632, 5760, 5888

VMEM_LIMIT = 56 * 1024 * 1024
MOE_TILE = 256


def _cparams(sem, vmem=VMEM_LIMIT, disable_bounds_checks=False):
    return pltpu.CompilerParams(dimension_semantics=sem, vmem_limit_bytes=vmem,
                                disable_bounds_checks=disable_bounds_checks)


def _const_spec(shape):
    nd = len(shape)
    return pl.BlockSpec(shape, lambda *_: (0,) * nd)


def _split2(x):
    hi = x.astype(BF16)
    lo = (x - hi.astype(F32)).astype(BF16)
    return hi, lo


def _split3(x):
    hi = x.astype(BF16)
    r1 = x - hi.astype(F32)
    mid = r1.astype(BF16)
    lo = (r1 - mid.astype(F32)).astype(BF16)
    return hi, mid, lo


_NN = (((1,), (0,)), ((), ()))
_NT = (((1,), (1,)), ((), ()))


def _dot(a, b, dims=_NN):
    return lax.dot_general(a, b, dims, preferred_element_type=F32)


def _dot3(a, b, dims=_NN):
    ah, al = _split2(a)
    bh, bl = _split2(b)
    return _dot(ah, bh, dims) + _dot(al, bh, dims) + _dot(ah, bl, dims)


def _dot_exact_rhs(a, b_bf16, dims=_NN):
    hi, mid, lo = _split3(a)
    return _dot(hi, b_bf16, dims) + _dot(mid, b_bf16, dims) + _dot(lo, b_bf16, dims)


def _dot_exact_lhs(a_bf16, b, dims=_NN):
    hi, mid, lo = _split3(b)
    return _dot(a_bf16, hi, dims) + _dot(a_bf16, mid, dims) + _dot(a_bf16, lo, dims)


def _sigmoid(x):
    return 1.0 / (1.0 + jnp.exp(-x))


def _silu(x):
    return x * _sigmoid(x)


def _layer_norm(x, g, b, eps):
    mu = jnp.mean(x, axis=-1, keepdims=True)
    xc = x - mu
    var = jnp.mean(xc * xc, axis=-1, keepdims=True)
    return xc * lax.rsqrt(var + eps) * g + b


def _masked_softmax(s, maskf):
    s = jnp.where(maskf > 0.5, s, NEG)
    e = jnp.exp(s - jnp.max(s, axis=-1, keepdims=True)) * maskf
    return e / jnp.maximum(jnp.sum(e, axis=-1, keepdims=True), 1e-30)


def _proj_in_kernel(x_ref, w_ref, b_ref, o_ref, xb_ref):
    @pl.when(pl.program_id(1) == 0)
    def _():
        xb_ref[...] = x_ref[...].astype(BF16)

    o_ref[...] = _dot(xb_ref[...], w_ref[...]) + b_ref[...]


def _proj_in(x2d, w_bf16, b_row):
    n = x2d.shape[0]
    tm = min(1024, n)
    tn = 512
    return pl.pallas_call(
        _proj_in_kernel,
        grid=(n // tm, N_Z // tn),
        in_specs=[pl.BlockSpec((tm, D_MODEL), lambda i, j: (i, 0)),
                  pl.BlockSpec((D_MODEL, tn), lambda i, j: (0, j)),
                  pl.BlockSpec((1, tn), lambda i, j: (0, j))],
        out_specs=pl.BlockSpec((tm, tn), lambda i, j: (i, j)),
        out_shape=jax.ShapeDtypeStruct((n, N_Z), F32),
        scratch_shapes=[pltpu.VMEM((tm, D_MODEL), BF16)],
        compiler_params=_cparams(("parallel", "arbitrary")),
        name="proj_in",
    )(x2d, w_bf16, b_row)


_CONV_ROWS = 32


def _conv_kernel(val_ref, gate_ref, bg_ref, cg_ref, h_ref, bufa_ref, bufc_ref, wa_ref, ba_ref, lag_ref,
                 lab_ref, wc_ref, ya_ref, yc_ref, na_ref, nc_ref, ea_ref, ec_ref, *, tt, t_last):
    it = pl.program_id(1)
    nt = pl.num_programs(1)
    ha = 32
    hc = 8

    @pl.when(it == 0)
    def _():
        ea_ref[0:ha, :] = jnp.zeros((ha, GW), F32)
        ea_ref[ha - (CONV_A_WIDTH - 1):ha, :] = bufa_ref[0]
        ec_ref[0:hc, :] = jnp.zeros((hc, GW), F32)
        ec_ref[hc - (SHORT_CONV_WIDTH - 1):hc, :] = bufc_ref[0]

    @pl.when(it > 0)
    def _():
        ea_ref[0:ha, :] = ea_ref[tt:tt + ha, :]
        ec_ref[0:hc, :] = ec_ref[tt:tt + hc, :]

    ea_ref[ha:ha + tt, :] = val_ref[0] * _sigmoid(gate_ref[0])
    ec_ref[hc:hc + tt, :] = cg_ref[0] * h_ref[0]

    rc = min(_CONV_ROWS, tt)
    offa = ha - (CONV_A_WIDTH - 1)
    offc = hc - (SHORT_CONV_WIDTH - 1)
    for c in range(tt // rc):
        r0 = c * rc
        acc = jnp.zeros((rc, GW), F32)
        for j in range(CONV_A_WIDTH):
            acc = acc + wa_ref[j:j + 1, :] * ea_ref[r0 + offa + j:r0 + offa + j + rc, :]
        y = _layer_norm(acc + ba_ref[...], lag_ref[...], lab_ref[...], LN_EPS)
        ya_ref[0, r0:r0 + rc, :] = _silu(y)
        accc = jnp.zeros((rc, GW), F32)
        for j in range(SHORT_CONV_WIDTH):
            accc = accc + wc_ref[j:j + 1, :] * ec_ref[r0 + offc + j:r0 + offc + j + rc, :]
        yc_ref[0, r0:r0 + rc, :] = bg_ref[0, r0:r0 + rc, :] * accc

    @pl.when(it == nt - 1)
    def _():
        na_ref[0] = ea_ref[t_last + offa:t_last + ha, :]
        nc_ref[0] = ec_ref[t_last + offc:t_last + hc, :]


def _conv_mix(z, buf_a, buf_c, wa, ba, lag, lab, wc, t_real):
    B, T, _ = z.shape
    tt = min(256, T)
    nt = T // tt
    t_last = t_real - (nt - 1) * tt
    zspec = lambda col: pl.BlockSpec((1, tt, GW), lambda b, i: (b, i, col // GW))
    return pl.pallas_call(
        functools.partial(_conv_kernel, tt=tt, t_last=t_last),
        grid=(B, nt),
        in_specs=[zspec(C_VAL), zspec(C_GATE), zspec(C_BG), zspec(C_CG), zspec(C_H),
                  pl.BlockSpec((1, CONV_A_WIDTH - 1, GW), lambda b, i: (b, 0, 0)),
                  pl.BlockSpec((1, SHORT_CONV_WIDTH - 1, GW), lambda b, i: (b, 0, 0)),
                  _const_spec((CONV_A_WIDTH, GW)), _const_spec((1, GW)), _const_spec((1, GW)),
                  _const_spec((1, GW)), _const_spec((SHORT_CONV_WIDTH, GW))],
        out_specs=[pl.BlockSpec((1, tt, GW), lambda b, i: (b, i, 0)),
                   pl.BlockSpec((1, tt, GW), lambda b, i: (b, i, 0)),
                   pl.BlockSpec((1, CONV_A_WIDTH - 1, GW), lambda b, i: (b, 0, 0)),
                   pl.BlockSpec((1, SHORT_CONV_WIDTH - 1, GW), lambda b, i: (b, 0, 0))],
        out_shape=[jax.ShapeDtypeStruct((B, T, GW), F32), jax.ShapeDtypeStruct((B, T, GW), F32),
                   jax.ShapeDtypeStruct((B, CONV_A_WIDTH - 1, GW), F32),
                   jax.ShapeDtypeStruct((B, SHORT_CONV_WIDTH - 1, GW), F32)],
        scratch_shapes=[pltpu.VMEM((tt + 32, GW), F32), pltpu.VMEM((tt + 8, GW), F32)],
        compiler_params=_cparams(("parallel", "arbitrary")),
        name="conv_mix",
    )(z, z, z, z, z, buf_a, buf_c, wa, ba, lag, lab, wc)


def _rwkv_prep_kernel(r_ref, k_ref, v_ref, wl_ref, al_ref, gl_ref, sh_ref, mu_ref, w0_ref, w2_ref, a0_ref,
                      a2_ref, g2_ref, kkp_ref, kap_ref, rk_ref,
                      ro_ref, wo_ref, ko_ref, vo_ref, kko_ref, bo_ref, bonus_ref, g_ref, carry_ref, *, tt, t_last):
    it = pl.program_id(1)

    @pl.when(it == 0)
    def _():
        carry_ref[...] = sh_ref[0]

    row = lax.broadcasted_iota(jnp.int32, (tt, 1), 0)

    def mixed(x, c0):
        w = x.shape[1]
        prev = pltpu.roll(x, 1, 0) if tt > 1 else x
        prev = jnp.where(row == 0, carry_ref[:, c0:c0 + w], prev)
        return x + (prev - x) * mu_ref[:, c0:c0 + w]

    xr, xk, xv = r_ref[0], k_ref[0], v_ref[0]
    xwl, xal, xgl = wl_ref[0], al_ref[0], gl_ref[0]
    r = mixed(xr, 0)
    k = mixed(xk, GW)
    v = mixed(xv, 2 * GW)
    wl = mixed(xwl, 3 * GW)
    al = mixed(xal, 3 * GW + 128)
    gl = mixed(xgl, 3 * GW + 256)
    lr = t_last - 1
    carry_ref[:, 0:GW] = xr[lr:lr + 1]
    carry_ref[:, GW:2 * GW] = xk[lr:lr + 1]
    carry_ref[:, 2 * GW:3 * GW] = xv[lr:lr + 1]
    carry_ref[:, 3 * GW:3 * GW + 128] = xwl[lr:lr + 1]
    carry_ref[:, 3 * GW + 128:3 * GW + 256] = xal[lr:lr + 1]
    carry_ref[:, 3 * GW + 256:3 * GW + 384] = xgl[lr:lr + 1]

    y = w0_ref[...] + _dot3(jnp.tanh(wl), w2_ref[...])
    u = -y
    w_log = -(jnp.maximum(u, 0.0) + jnp.log(1.0 + jnp.exp(-jnp.abs(u)))) - 0.5
    log_decay = -jnp.exp(w_log)
    a = _sigmoid(a0_ref[...] + _dot3(al, a2_ref[...]))
    g_ref[0] = _dot3(_sigmoid(gl), g2_ref[...])
    kk = k * kkp_ref[...]
    k2 = k * (1.0 + (a - 1.0) * kap_ref[...])
    rkk = r * k2 * rk_ref[...]
    for h in range(HEADS):
        sl = slice(h * HEAD_DIM, (h + 1) * HEAD_DIM)
        kk_h = kk[:, sl]
        nrm = jnp.sqrt(jnp.sum(kk_h * kk_h, axis=-1, keepdims=True))
        kk_h = kk_h / jnp.maximum(nrm, 1e-12)
        v_h = v[:, sl]
        ro_ref[0, h] = r[:, sl]
        wo_ref[0, h] = log_decay[:, sl]
        ko_ref[0, h] = k2[:, sl]
        vo_ref[0, h] = v_h
        kko_ref[0, h] = kk_h
        bo_ref[0, h] = kk_h * a[:, sl]
        bonus_ref[0, h] = jnp.sum(rkk[:, sl], axis=-1, keepdims=True) * v_h


def _rwkv_prep(z, shift, mu, w0, w2p, a0, a2p, g2p, kkp, kap, rk, t_real):
    B, T, _ = z.shape
    tt = min(256, T)
    nt = T // tt
    t_last = t_real - (nt - 1) * tt
    wide = lambda col: pl.BlockSpec((1, tt, GW), lambda b, i: (b, i, col // GW))
    thin = lambda col: pl.BlockSpec((1, tt, 128), lambda b, i: (b, i, col // 128))
    head_out = pl.BlockSpec((1, HEADS, tt, HEAD_DIM), lambda b, i: (b, 0, i, 0))
    head_shape = jax.ShapeDtypeStruct((B, HEADS, T, HEAD_DIM), F32)
    dw = 3 * GW + 384
    return pl.pallas_call(
        functools.partial(_rwkv_prep_kernel, tt=tt, t_last=t_last),
        grid=(B, nt),
        in_specs=[wide(C_R), wide(C_K), wide(C_V), thin(C_WL), thin(C_AL), thin(C_GL),
                  pl.BlockSpec((1, 1, dw), lambda b, i: (b, 0, 0)),
                  _const_spec((1, dw)), _const_spec((1, GW)), _const_spec((128, GW)), _const_spec((1, GW)),
                  _const_spec((128, GW)), _const_spec((128, GW)), _const_spec((1, GW)), _const_spec((1, GW)),
                  _const_spec((1, GW))],
        out_specs=[head_out] * 7 + [pl.BlockSpec((1, tt, GW), lambda b, i: (b, i, 0))],
        out_shape=[head_shape] * 7 + [jax.ShapeDtypeStruct((B, T, GW), F32)],
        scratch_shapes=[pltpu.VMEM((1, dw), F32)],
        compiler_params=_cparams(("parallel", "arbitrary")),
        name="rwkv_prep",
    )(z, z, z, z, z, z, shift, mu, w0, w2p, a0, a2p, g2p, kkp, kap, rk)


def _rwkv_scan_kernel(r_ref, w_ref, k_ref, v_ref, kk_ref, b_ref, bonus_ref, s0_ref, lg_ref, lb_ref,
                      y_ref, s_ref, *, nb, tt, n_steps):
    it = pl.program_id(1)

    @pl.when(it == 0)
    def _():
        s_ref[...] = s0_ref[...]

    y_ref[...] = jnp.zeros(y_ref.shape, F32)
    ii = lax.broadcasted_iota(jnp.int32, (HEAD_DIM, HEAD_DIM), 0)
    jj = lax.broadcasted_iota(jnp.int32, (HEAD_DIM, HEAD_DIM), 1)
    eye = jnp.where(ii == jj, 1.0, 0.0).astype(F32)

    def step(t, carry):
        for n in range(nb):
            for h in range(HEADS):
                row = lambda ref: ref[n, h, pl.ds(t, 1), :]
                S = s_ref[n, h]
                sa = jnp.sum(S * row(kk_ref), axis=-1, keepdims=True)
                vcol = jnp.sum(eye * row(v_ref), axis=-1, keepdims=True)
                S = S * jnp.exp(row(w_ref)) - sa * row(b_ref) + vcol * row(k_ref)
                s_ref[n, h] = S
                ycol = jnp.sum(S * row(r_ref), axis=-1, keepdims=True)
                y_ref[n, h, pl.ds(t, 1), :] = jnp.sum(eye * ycol, axis=0, keepdims=True)
        return carry

    lax.fori_loop(0, n_steps, step, 0)

    for n in range(nb):
        for h in range(HEADS):
            y = y_ref[n, h]
            mu = jnp.mean(y, axis=-1, keepdims=True)
            yc = y - mu
            var = jnp.mean(yc * yc, axis=-1, keepdims=True)
            y_ref[n, h] = yc * lax.rsqrt(var + RWKV_GN_EPS) * lg_ref[h:h + 1, :] + lb_ref[h:h + 1, :] + bonus_ref[n, h]


def _rwkv_scan(r, w, k, v, kk, b, bonus, s0, lnx_g, lnx_b, t_real):
    B, H, T, _ = r.shape
    nb = 2
    tt = min(128, T)
    nt = T // tt
    n_steps = min(tt, t_real)
    blk = pl.BlockSpec((nb, H, tt, HEAD_DIM), lambda b_, i: (b_, 0, i, 0))
    sblk = pl.BlockSpec((nb, H, HEAD_DIM, HEAD_DIM), lambda b_, i: (b_, 0, 0, 0))
    return pl.pallas_call(
        functools.partial(_rwkv_scan_kernel, nb=nb, tt=tt, n_steps=n_steps),
        grid=(B // nb, nt),
        in_specs=[blk] * 7 + [sblk, _const_spec((H, HEAD_DIM)), _const_spec((H, HEAD_DIM))],
        out_specs=[blk, sblk],
        out_shape=[jax.ShapeDtypeStruct((B, H, T, HEAD_DIM), F32),
                   jax.ShapeDtypeStruct((B, H, HEAD_DIM, HEAD_DIM), F32)],
        compiler_params=_cparams(("parallel", "arbitrary")),
        name="rwkv_scan",
    )(r, w, k, v, kk, b, bonus, s0, lnx_g, lnx_b)


_RW_CHUNK = 64

_TN = (((0,), (0,)), ((), ()))


def _rwkv_chunk_kernel(r_ref, lw_ref, k_ref, v_ref, kk_ref, b_ref, bonus_ref, s0_ref, lg_ref, lb_ref, y_ref, s_ref):
    C = r_ref.shape[2]
    it = pl.program_id(1)

    @pl.when(it == 0)
    def _():
        s_ref[...] = s0_ref[...]

    ti = lax.broadcasted_iota(jnp.int32, (C, C), 0)
    si = lax.broadcasted_iota(jnp.int32, (C, C), 1)
    incl = ti >= si
    strict = ti > si
    tri_b = jnp.where(incl, 1.0, 0.0).astype(BF16)
    n_fac = C.bit_length() - 1
    hs = range(HEADS)
    v = [v_ref[0, h] for h in hs]
    cs = [_dot_exact_lhs(tri_b, lw_ref[0, h]) for h in hs]
    cs_last = [c[C - 1:C, :] for c in cs]
    AR = [jnp.concatenate([kk_ref[0, h] * jnp.exp(cs[h] - lw_ref[0, h]), r_ref[0, h] * jnp.exp(cs[h])], axis=0)
          for h in hs]
    BK = []
    for h in hs:
        g_inv = jnp.exp(-cs[h])
        BK.append(jnp.concatenate([b_ref[0, h] * g_inv, k_ref[0, h] * g_inv], axis=0))
    G = [_dot3(AR[h], BK[h], _NT) for h in hs]
    S0 = [s_ref[0, h] for h in hs]
    ARS = [_dot3(AR[h], S0[h], _NT) for h in hs]
    LMV = [_dot3(jnp.concatenate([jnp.where(strict, G[h][0:C, C:2 * C], 0.0),
                                  jnp.where(incl, G[h][C:2 * C, C:2 * C], 0.0)], axis=0), v[h]) for h in hs]
    X = [ARS[h][0:C] + LMV[h][0:C] for h in hs]
    N = [-jnp.where(strict, G[h][0:C, 0:C], 0.0) for h in hs]
    for f in range(n_fac):
        X = [X[h] + _dot3(N[h], X[h]) for h in hs]
        if f + 1 < n_fac:
            N = [_dot3(N[h], N[h]) for h in hs]
    Y = [ARS[h][C:2 * C] + LMV[h][C:2 * C] - _dot3(jnp.where(incl, G[h][C:2 * C, 0:C], 0.0), X[h]) for h in hs]
    for h in hs:
        g_end = jnp.exp(cs_last[h] - cs[h])
        UV = jnp.concatenate([-X[h], v[h]], axis=0)
        BKg = jnp.concatenate([b_ref[0, h] * g_end, k_ref[0, h] * g_end], axis=0)
        s_ref[0, h] = S0[h] * jnp.exp(cs_last[h]) + _dot3(UV, BKg, _TN)
    for h in hs:
        mu = jnp.mean(Y[h], axis=-1, keepdims=True)
        yc = Y[h] - mu
        var = jnp.mean(yc * yc, axis=-1, keepdims=True)
        y_ref[0, h] = yc * lax.rsqrt(var + RWKV_GN_EPS) * lg_ref[h:h + 1, :] + lb_ref[h:h + 1, :] + bonus_ref[0, h]


def _rwkv_chunked(r, lw, k, v, kk, b, bonus, s0, lnx_g, lnx_b):
    B, H, T, _ = r.shape
    C = _RW_CHUNK
    blk = pl.BlockSpec((1, H, C, HEAD_DIM), lambda b_, i: (b_, 0, i, 0))
    sblk = pl.BlockSpec((1, H, HEAD_DIM, HEAD_DIM), lambda b_, i: (b_, 0, 0, 0))
    return pl.pallas_call(
        _rwkv_chunk_kernel,
        grid=(B, T // C),
        in_specs=[blk] * 7 + [sblk, _const_spec((H, HEAD_DIM)), _const_spec((H, HEAD_DIM))],
        out_specs=[blk, sblk],
        out_shape=[jax.ShapeDtypeStruct((B, H, T, HEAD_DIM), F32),
                   jax.ShapeDtypeStruct((B, H, HEAD_DIM, HEAD_DIM), F32)],
        compiler_params=_cparams(("parallel", "arbitrary")),
        name="rwkv_chunked",
    )(r, lw, k, v, kk, b, bonus, s0, lnx_g, lnx_b)


def _compress_tail(F, cst):
    n = F.shape[0]
    nxt = pltpu.roll(F[:, HEAD_DIM:2 * HEAD_DIM], n - 1, 0)
    const = cst[0:1, 0:HEAD_DIM] + cst[1:2, HEAD_DIM:2 * HEAD_DIM]
    return F[:, 0:HEAD_DIM] + nxt + const


def _compress_kernel(x_ref, w_ref, pe_ref, o_ref):
    F = _dot3(x_ref[0, 0, 0], w_ref[0])
    cst = _dot3(pe_ref[0], w_ref[0])
    o_ref[0, 0, 0] = _compress_tail(F, cst)


def _compress_prompt(cmp_in, wcat, pe2):
    B, _, _, nc, _ = cmp_in.shape
    return pl.pallas_call(
        _compress_kernel,
        grid=(B, 2, 2),
        in_specs=[pl.BlockSpec((1, 1, 1, nc, 1024), lambda b, a, g: (b, a, g, 0, 0)),
                  pl.BlockSpec((1, 1024, 128), lambda b, a, g: (a, 0, 0)),
                  pl.BlockSpec((1, 8, 1024), lambda b, a, g: (a, 0, 0))],
        out_specs=pl.BlockSpec((1, 1, 1, nc, HEAD_DIM), lambda b, a, g: (b, a, g, 0, 0)),
        out_shape=jax.ShapeDtypeStruct((B, 2, 2, nc, HEAD_DIM), F32),
        compiler_params=_cparams(("parallel", "parallel", "parallel")),
        name="nsa_compress",
    )(cmp_in, wcat, pe2)


def _rank_select(selT, tie_lt, rank_ref, n_blocks):
    for j in range(n_blocks):
        rj = selT[j:j + 1, :]
        tie = tie_lt(j)
        gt = jnp.where(selT > rj, 1.0, jnp.where(selT == rj, tie, 0.0))
        rank_ref[j:j + 1, :] = jnp.sum(gt, axis=0, keepdims=True)
    return jnp.where(rank_ref[...] < float(N_SLC), 1.0, 0.0)


def _gated_heads(gts, g, o_cmp, o_slc, o_win, rows):
    out = []
    for m in range(Q_PER_KV):
        c0 = g * 3 * Q_PER_KV + m * 3
        sl = slice(m * rows, (m + 1) * rows)
        out.append(gts[:, c0:c0 + 1] * o_cmp[sl] + gts[:, c0 + 1:c0 + 2] * o_slc[sl] + gts[:, c0 + 2:c0 + 3] * o_win[sl])
    return out


_KT = 512


def _nsa_prompt_kernel(q_ref, gt_ref, ckv_ref, ksT_ref, vs_ref, kwT_ref, vw_ref, mt_ref, o_ref, rank_ref, *, T):
    i = pl.program_id(1)
    start = i * Q_BLOCK
    nc = T // CMP_STRIDE
    ns = T // SLC_BLOCK
    t_tok = start + lax.broadcasted_iota(jnp.int32, (Q_BLOCK, 1), 0)
    t_rows = jnp.concatenate([t_tok] * Q_PER_KV, axis=0)
    gts = _sigmoid(gt_ref[0])
    j_col = lax.broadcasted_iota(jnp.int32, (ns, 1), 0)
    t_lane = start + lax.broadcasted_iota(jnp.int32, (1, Q_BLOCK), 1)
    cur = t_lane >> 6
    forced = jnp.where(j_col == 0, 1.0, jnp.where(j_col == cur, 1.0, jnp.where(j_col == cur - 1, 1.0, 0.0)))
    r128 = lax.broadcasted_iota(jnp.int32, (Q_BLOCK, Q_BLOCK), 0)
    c128 = lax.broadcasted_iota(jnp.int32, (Q_BLOCK, Q_BLOCK), 1)
    eye128 = jnp.where(r128 == c128, 1.0, 0.0).astype(BF16)
    pieces = []
    for g in range(KV_HEADS):
        Q = jnp.concatenate([q_ref[0, :, (g * Q_PER_KV + m) * HEAD_DIM:(g * Q_PER_KV + m + 1) * HEAD_DIM]
                             for m in range(Q_PER_KV)], axis=0) * (HEAD_DIM ** -0.5)
        Qb = Q.astype(BF16)
        sc = _dot3(Q, ckv_ref[0, 0, g], _NT)
        n_lane = lax.broadcasted_iota(jnp.int32, (1, nc), 1)
        mask_c = jnp.where(n_lane * CMP_STRIDE + (2 * CMP_STRIDE - 1) <= t_rows, 1.0, 0.0)
        pc = _masked_softmax(sc, mask_c)
        o_cmp = _dot(pc.astype(BF16), ckv_ref[0, 1, g].astype(BF16))
        ps = pc[0:Q_BLOCK] + pc[Q_BLOCK:2 * Q_BLOCK] + pc[2 * Q_BLOCK:3 * Q_BLOCK] + pc[3 * Q_BLOCK:4 * Q_BLOCK]
        selT = _dot_exact_lhs(mt_ref[...], ps, _NT)
        selT = jnp.where(forced > 0.5, FORCED, jnp.where(j_col * SLC_BLOCK <= t_lane, selT, -FORCED))
        selmT = _rank_select(selT, lambda j: jnp.where(j_col < j, 1.0, 0.0), rank_ref, ns)
        selm = _dot(eye128, selmT.astype(BF16), _NT).astype(BF16)
        nkt = (start + Q_BLOCK + _KT - 1) // _KT

        def body(kt, carry):
            m_run, l_run, acc = carry
            s = _dot(Qb, ksT_ref[0, g, kt])
            jr = lax.broadcasted_iota(jnp.int32, (ns, _KT), 0)
            cl = lax.broadcasted_iota(jnp.int32, (ns, _KT), 1)
            expand = jnp.where(jr == kt * (_KT // SLC_BLOCK) + (cl >> 6), 1.0, 0.0).astype(BF16)
            mv = _dot(selm, expand)
            mv4 = jnp.concatenate([mv] * Q_PER_KV, axis=0)
            pos = kt * _KT + lax.broadcasted_iota(jnp.int32, (1, _KT), 1)
            ok = jnp.where(pos <= t_rows, mv4, 0.0)
            s = jnp.where(ok > 0.5, s, NEG)
            m_new = jnp.maximum(m_run, jnp.max(s, axis=-1, keepdims=True))
            p = jnp.exp(s - m_new) * ok
            corr = jnp.exp(m_run - m_new)
            l_new = corr * l_run + jnp.sum(p, axis=-1, keepdims=True)
            vt = vs_ref[0, g, pl.ds(pl.multiple_of(kt * _KT, _KT), _KT), :]
            acc = corr * acc + _dot(p.astype(BF16), vt)
            return m_new, l_new, acc

        init = (jnp.full((Q_PER_KV * Q_BLOCK, 1), NEG, F32), jnp.zeros((Q_PER_KV * Q_BLOCK, 1), F32),
                jnp.zeros((Q_PER_KV * Q_BLOCK, HEAD_DIM), F32))
        _, l_fin, acc = lax.fori_loop(0, nkt, body, init)
        o_slc = acc / jnp.maximum(l_fin, 1e-30)
        nwt = WINDOW // Q_BLOCK + 1
        wb = jnp.maximum(i - WINDOW // Q_BLOCK, 0)
        sw = jnp.concatenate([_dot(Qb, kwT_ref[0, g, wb + u]) for u in range(nwt)], axis=1)
        spos = wb * Q_BLOCK + lax.broadcasted_iota(jnp.int32, (1, nwt * Q_BLOCK), 1)
        dt = t_rows - spos
        mask_w = jnp.where(dt >= 0, jnp.where(dt < WINDOW, 1.0, 0.0), 0.0)
        pw = _masked_softmax(sw, mask_w)
        vwt = vw_ref[0, g, pl.ds(pl.multiple_of(wb * Q_BLOCK, Q_BLOCK), nwt * Q_BLOCK), :]
        o_win = _dot(pw.astype(BF16), vwt)
        pieces += _gated_heads(gts, g, o_cmp, o_slc, o_win, Q_BLOCK)
    o_ref[0] = jnp.concatenate(pieces, axis=1)


def _sel_matrix(n_rows, n_chunks):
    j = np.arange(n_rows)[:, None]
    c = np.arange(n_chunks)[None, :]
    per = SLC_BLOCK // CMP_STRIDE
    m = 0.5 * ((c // per == j).astype(np.float32) + ((c + 1) // per == j).astype(np.float32))
    return jnp.asarray(m, BF16)


def _nsa_prompt(z, ckv, ksT, vs, kwT, vw):
    B, T, _ = z.shape
    nc, ns = T // CMP_STRIDE, T // SLC_BLOCK
    nq = T // Q_BLOCK
    mt = _sel_matrix(ns, nc)
    per_b = lambda shape: pl.BlockSpec((1,) + shape, lambda b, i: (b,) + (0,) * len(shape))
    return pl.pallas_call(
        functools.partial(_nsa_prompt_kernel, T=T),
        grid=(B, nq),
        in_specs=[pl.BlockSpec((1, Q_BLOCK, GW), lambda b, i: (b, i, C_Q // GW)),
                  pl.BlockSpec((1, Q_BLOCK, 256), lambda b, i: (b, i, C_GATES // 256)),
                  per_b((2, KV_HEADS, nc, HEAD_DIM)),
                  per_b((KV_HEADS, T // _KT, HEAD_DIM, _KT)),
                  per_b((KV_HEADS, T, HEAD_DIM)),
                  per_b((KV_HEADS, nq, HEAD_DIM, Q_BLOCK)),
                  per_b((KV_HEADS, T, HEAD_DIM)),
                  _const_spec((ns, nc))],
        out_specs=pl.BlockSpec((1, Q_BLOCK, GW), lambda b, i: (b, i, 0)),
        out_shape=jax.ShapeDtypeStruct((B, T, GW), F32),
        scratch_shapes=[pltpu.VMEM((ns, Q_BLOCK), F32)],
        compiler_params=_cparams(("parallel", "arbitrary")),
        name="nsa_prompt",
    )(z, z, ckv, ksT, vs, kwT, vw, mt)


_PPS = 8
_SEL_ROWS = 384


def _nsa_sample_a_kernel(pt_ref, *refs, n_pages, past_len, t_pad):
    pages = refs[:_PPS]
    w_ref, pe_ref, q_ref, mt_ref = refs[_PPS:_PPS + 4]
    ocmp_ref, selm_ref = refs[_PPS + 4:_PPS + 6]
    x_ref, rank_ref = refs[_PPS + 6:]
    s = pl.program_id(1)
    cpp = PAGE // CMP_STRIDE
    for k in range(_PPS):
        x_ref[:, pl.ds(pl.multiple_of((s * _PPS + k) * cpp, cpp), cpp), :] = pages[k][0]

    @pl.when(s == pl.num_programs(1) - 1)
    def _():
        nc = n_pages * cpp
        n_blocks = past_len // SLC_BLOCK + 1
        lanes = 128
        ps_rows = []
        for g in range(KV_HEADS):
            ck = _compress_tail(_dot3(x_ref[g], w_ref[0]), _dot3(pe_ref[0], w_ref[0]))
            cv = _compress_tail(_dot3(x_ref[KV_HEADS + g], w_ref[1]), _dot3(pe_ref[1], w_ref[1]))
            Q = jnp.concatenate([q_ref[0, :, (g * Q_PER_KV + m) * HEAD_DIM:(g * Q_PER_KV + m + 1) * HEAD_DIM]
                                 for m in range(Q_PER_KV)], axis=0) * (HEAD_DIM ** -0.5)
            t_rows = past_len + (lax.broadcasted_iota(jnp.int32, (Q_PER_KV * t_pad, 1), 0) & (t_pad - 1))
            sc = _dot3(Q, ck, _NT)
            n_lane = lax.broadcasted_iota(jnp.int32, (1, nc), 1)
            mask_c = jnp.where(n_lane * CMP_STRIDE + (2 * CMP_STRIDE - 1) <= t_rows, 1.0, 0.0)
            pc = _masked_softmax(sc, mask_c)
            ocmp_ref[0, g] = _dot(pc.astype(BF16), cv.astype(BF16))
            ps_rows.append(pc[0:t_pad] + pc[t_pad:2 * t_pad] + pc[2 * t_pad:3 * t_pad] + pc[3 * t_pad:4 * t_pad])
        ps = jnp.concatenate(ps_rows + [jnp.zeros((lanes - KV_HEADS * t_pad, nc), F32)], axis=0)
        selT = _dot_exact_lhs(mt_ref[...], ps, _NT)
        j_col = lax.broadcasted_iota(jnp.int32, (_SEL_ROWS, 1), 0)
        t_lane = past_len + (lax.broadcasted_iota(jnp.int32, (1, lanes), 1) & (t_pad - 1))
        cur = t_lane >> 6
        forced = jnp.where(j_col == 0, 1.0, jnp.where(j_col == cur, 1.0, jnp.where(j_col == cur - 1, 1.0, 0.0)))
        selT = jnp.where(forced > 0.5, FORCED, jnp.where(j_col * SLC_BLOCK <= t_lane, selT, -FORCED))
        selT = jnp.where(j_col < n_blocks, selT, -3.0 * FORCED)
        rank_ref[...] = jnp.full(rank_ref.shape, float(_SEL_ROWS), F32)
        selmT = _rank_select(selT, lambda j: jnp.where(j_col < j, 1.0, 0.0), rank_ref, n_blocks)
        r = lax.broadcasted_iota(jnp.int32, (lanes, lanes), 0)
        c = lax.broadcasted_iota(jnp.int32, (lanes, lanes), 1)
        eye = jnp.where(r == c, 1.0, 0.0).astype(BF16)
        selm = _dot(eye, selmT.astype(BF16), _NT)
        selm_ref[0] = selm[0:KV_HEADS * t_pad]


def _nsa_sample_a(page_table, cmp_pages, layer, wcat, pe2, z, past_len):
    B, t_pad, _ = z.shape
    n_pages = page_table.shape[1]
    nc = n_pages * (PAGE // CMP_STRIDE)
    mt = _sel_matrix(_SEL_ROWS, nc)
    page_spec = lambda k: pl.BlockSpec((None, 1, 2 * KV_HEADS, PAGE // CMP_STRIDE, 1024),
                                       lambda b, s, pt: (layer, pt[b, s * _PPS + k], 0, 0, 0))
    cst = lambda shape: pl.BlockSpec(shape, lambda b, s, pt: (0,) * len(shape))
    grid_spec = pltpu.PrefetchScalarGridSpec(
        num_scalar_prefetch=1,
        grid=(B, n_pages // _PPS),
        in_specs=[page_spec(k) for k in range(_PPS)] + [
            cst((2, 1024, 128)), cst((2, 8, 1024)),
            pl.BlockSpec((1, t_pad, GW), lambda b, s, pt: (b, 0, C_Q // GW)),
            cst((_SEL_ROWS, nc))],
        out_specs=[pl.BlockSpec((1, KV_HEADS, Q_PER_KV * t_pad, HEAD_DIM), lambda b, s, pt: (b, 0, 0, 0)),
                   pl.BlockSpec((1, KV_HEADS * t_pad, _SEL_ROWS), lambda b, s, pt: (b, 0, 0))],
        scratch_shapes=[pltpu.VMEM((2 * KV_HEADS, nc, 1024), F32), pltpu.VMEM((_SEL_ROWS, 128), F32)])
    return pl.pallas_call(
        functools.partial(_nsa_sample_a_kernel, n_pages=n_pages, past_len=past_len, t_pad=t_pad),
        grid_spec=grid_spec,
        out_shape=[jax.ShapeDtypeStruct((B, KV_HEADS, Q_PER_KV * t_pad, HEAD_DIM), F32),
                   jax.ShapeDtypeStruct((B, KV_HEADS * t_pad, _SEL_ROWS), F32)],
        compiler_params=_cparams(("parallel", "arbitrary")),
        name="nsa_sample_select",
    )(page_table, *([cmp_pages] * _PPS), wcat, pe2, z, mt)


def _nsa_sample_b_kernel(pt_ref, *refs, n_pages, past_len, t_pad, t_real):
    kpages = refs[:_PPS]
    vpages = refs[_PPS:2 * _PPS]
    q_ref, rows_ref, kwvw_ref, gt_ref, selm_ref, ocmp_ref, cw_ref = refs[2 * _PPS:2 * _PPS + 7]
    o_ref = refs[2 * _PPS + 7]
    m_ref, l_ref, acc_ref = refs[2 * _PPS + 8:]
    s = pl.program_id(1)
    n_steps = n_pages // _PPS
    rows = Q_PER_KV * t_pad
    t_rows = past_len + (lax.broadcasted_iota(jnp.int32, (rows, 1), 0) & (t_pad - 1))

    @pl.when(s == 0)
    def _():
        m_ref[...] = jnp.full(m_ref.shape, NEG, F32)
        l_ref[...] = jnp.zeros(l_ref.shape, F32)
        acc_ref[...] = jnp.zeros(acc_ref.shape, F32)

    def q_rows(g):
        Q = jnp.concatenate([q_ref[0, :, (g * Q_PER_KV + m) * HEAD_DIM:(g * Q_PER_KV + m + 1) * HEAD_DIM]
                             for m in range(Q_PER_KV)], axis=0) * (HEAD_DIM ** -0.5)
        return Q.astype(BF16)

    def online(g, sc, ok, values):
        sc = jnp.where(ok > 0.5, sc, NEG)
        m_run = m_ref[g]
        m_new = jnp.maximum(m_run, jnp.max(sc, axis=-1, keepdims=True))
        p = jnp.exp(sc - m_new) * ok
        corr = jnp.exp(m_run - m_new)
        l_ref[g] = corr * l_ref[g] + jnp.sum(p, axis=-1, keepdims=True)
        acc_ref[g] = corr * acc_ref[g] + values(p.astype(BF16))
        m_ref[g] = m_new

    @pl.when(s < n_steps)
    def _():
        keys = _PPS * PAGE
        jr = lax.broadcasted_iota(jnp.int32, (_SEL_ROWS, keys), 0)
        cl = lax.broadcasted_iota(jnp.int32, (_SEL_ROWS, keys), 1)
        expand = jnp.where(jr == s * (keys // SLC_BLOCK) + (cl >> 6), 1.0, 0.0).astype(BF16)
        mv = _dot(selm_ref[0].astype(BF16), expand)
        pos = s * keys + lax.broadcasted_iota(jnp.int32, (1, keys), 1)
        for g in range(KV_HEADS):
            Qb = q_rows(g)
            sc = jnp.concatenate([_dot(Qb, kpages[k][g].astype(BF16)) for k in range(_PPS)], axis=1)
            mvg = jnp.concatenate([mv[g * t_pad:(g + 1) * t_pad]] * Q_PER_KV, axis=0)
            ok = jnp.where(pos <= t_rows, mvg, 0.0)

            def values(pb, g=g):
                out = _dot(pb[:, 0:PAGE], vpages[0][g].astype(BF16), _NT)
                for k in range(1, _PPS):
                    out = out + _dot(pb[:, k * PAGE:(k + 1) * PAGE], vpages[k][g].astype(BF16), _NT)
                return out

            online(g, sc, ok, values)

    @pl.when(s == n_steps)
    def _():
        gts = _sigmoid(gt_ref[0])
        last_blk = past_len // SLC_BLOCK
        r_lane = lax.broadcasted_iota(jnp.int32, (1, t_pad), 1)
        pieces = []
        for g in range(KV_HEADS):
            Qb = q_rows(g)
            k_new = rows_ref[0, :, (2 * KV_HEADS + g) * HEAD_DIM:(2 * KV_HEADS + g + 1) * HEAD_DIM]
            v_new = rows_ref[0, :, (3 * KV_HEADS + g) * HEAD_DIM:(3 * KV_HEADS + g + 1) * HEAD_DIM]
            sc = _dot(Qb, k_new.astype(BF16), _NT)
            sel_last = jnp.concatenate([selm_ref[0, g * t_pad:(g + 1) * t_pad, last_blk:last_blk + 1]] * Q_PER_KV, axis=0)
            vis = jnp.where(past_len + r_lane <= t_rows, jnp.where(r_lane < t_real, 1.0, 0.0), 0.0)
            online(g, sc, vis * sel_last, lambda pb, v_new=v_new: _dot(pb, v_new.astype(BF16)))
            o_slc = acc_ref[g] / jnp.maximum(l_ref[g], 1e-30)
            wrows = cw_ref.shape[3]
            kw_new = kwvw_ref[0, :, g * HEAD_DIM:(g + 1) * HEAD_DIM]
            vw_new = kwvw_ref[0, :, (KV_HEADS + g) * HEAD_DIM:(KV_HEADS + g + 1) * HEAD_DIM]
            sw = jnp.concatenate([_dot(Qb, cw_ref[0, g].astype(BF16)),
                                  _dot(Qb, kw_new.astype(BF16), _NT)], axis=1)
            idx = lax.broadcasted_iota(jnp.int32, (1, wrows + t_pad), 1)
            spos = past_len - wrows + idx
            dt = t_rows - spos
            real = jnp.where(idx < wrows + t_real, 1.0, 0.0)
            mask_w = jnp.where(dt >= 0, jnp.where(dt < WINDOW, real, 0.0), 0.0)
            pw = _masked_softmax(sw, mask_w).astype(BF16)
            o_win = _dot(pw[:, 0:wrows], cw_ref[1, g].astype(BF16), _NT) + _dot(pw[:, wrows:], vw_new.astype(BF16))
            pieces += _gated_heads(gts, g, ocmp_ref[0, g], o_slc, o_win, t_pad)
        o_ref[0] = jnp.concatenate(pieces, axis=1)


def _nsa_sample_b(page_table, cache_t, win_t, layer, z, selm, ocmp, past_len, t_real):
    B, t_pad, _ = z.shape
    n_pages = page_table.shape[1]
    n_steps = n_pages // _PPS
    rows = Q_PER_KV * t_pad
    wrows = win_t.shape[5]
    pidx = lambda b, s, pt, k: pt[b, jnp.minimum(s, n_steps - 1) * _PPS + k]
    page = lambda which, k: pl.BlockSpec((None, None, None, KV_HEADS, HEAD_DIM, PAGE),
                                         lambda b, s, pt: (layer, pidx(b, s, pt, k), which, 0, 0, 0))
    kspec = lambda k: page(2, k)
    vspec = lambda k: page(3, k)
    zspec = lambda col, w: pl.BlockSpec((1, t_pad, w), lambda b, s, pt: (b, 0, col // w))
    per_b = lambda shape: pl.BlockSpec((1,) + shape, lambda b, s, pt: (b,) + (0,) * len(shape))
    grid_spec = pltpu.PrefetchScalarGridSpec(
        num_scalar_prefetch=1,
        grid=(B, n_steps + 1),
        in_specs=[kspec(k) for k in range(_PPS)] + [vspec(k) for k in range(_PPS)] + [
            zspec(C_Q, GW), zspec(C_ROWS, GW), zspec(C_KW, 256), zspec(C_GATES, 256),
            per_b((KV_HEADS * t_pad, _SEL_ROWS)), per_b((KV_HEADS, rows, HEAD_DIM)),
            pl.BlockSpec((None, None, 2, KV_HEADS, HEAD_DIM, wrows), lambda b, s, pt: (layer, b, 0, 0, 0, 0))],
        out_specs=pl.BlockSpec((1, t_pad, GW), lambda b, s, pt: (b, 0, 0)),
        scratch_shapes=[pltpu.VMEM((KV_HEADS, rows, 1), F32), pltpu.VMEM((KV_HEADS, rows, 1), F32),
                        pltpu.VMEM((KV_HEADS, rows, HEAD_DIM), F32)])
    return pl.pallas_call(
        functools.partial(_nsa_sample_b_kernel, n_pages=n_pages, past_len=past_len, t_pad=t_pad, t_real=t_real),
        grid_spec=grid_spec,
        out_shape=jax.ShapeDtypeStruct((B, t_pad, GW), F32),
        compiler_params=_cparams(("parallel", "arbitrary")),
        name="nsa_sample_attend",
    )(page_table, *([cache_t] * (2 * _PPS)), z, z, z, z, selm, ocmp, win_t)


def _proj_out_kernel(x_ref, ya_ref, yb_ref, yc_ref, yd_ref, g_ref, w_ref, b_ref, lg_ref, lb_ref, h_ref, *, alpha):
    yd = yd_ref[...] * g_ref[...]
    acc = _dot(ya_ref[...].astype(BF16), w_ref[0:GW, :])
    acc = acc + _dot(yb_ref[...].astype(BF16), w_ref[GW:2 * GW, :])
    acc = acc + _dot(yc_ref[...].astype(BF16), w_ref[2 * GW:3 * GW, :])
    acc = acc + _dot(yd.astype(BF16), w_ref[3 * GW:4 * GW, :])
    h_ref[...] = _layer_norm(alpha * x_ref[...] + acc + b_ref[...], lg_ref[...], lb_ref[...], LN_EPS)


def _proj_out(x2d, ya, yb, yc, yd, g, w_bf16, b, lg, lb, alpha):
    n = x2d.shape[0]
    tm = min(256, n)
    row = lambda w: pl.BlockSpec((tm, w), lambda i: (i, 0))
    return pl.pallas_call(
        functools.partial(_proj_out_kernel, alpha=alpha),
        grid=(n // tm,),
        in_specs=[row(D_MODEL), row(GW), row(GW), row(GW), row(GW), row(GW),
                  _const_spec((D_MODEL, D_MODEL)), _const_spec((1, D_MODEL)), _const_spec((1, D_MODEL)),
                  _const_spec((1, D_MODEL))],
        out_specs=row(D_MODEL),
        out_shape=jax.ShapeDtypeStruct((n, D_MODEL), F32),
        compiler_params=_cparams(("parallel",)),
        name="proj_out_ln1",
    )(x2d, ya, yb, yc, yd, g, w_bf16, b, lg, lb)


def _router_kernel(h_ref, w_ref, b_ref, o_ref):
    logits = _dot3(h_ref[...], w_ref[...]) + b_ref[...]
    lane = lax.broadcasted_iota(jnp.int32, logits.shape, 1).astype(F32)
    big = 1e6
    is_g = lane < N_GROUPS
    gl = jnp.where(is_g, logits, NEG)
    gmax = jnp.max(gl, axis=-1, keepdims=True)
    gsel = jnp.min(jnp.where(is_g, jnp.where(gl == gmax, lane, big), big), axis=-1, keepdims=True)
    gp = 1.0 / jnp.sum(jnp.where(is_g, jnp.exp(gl - gmax), 0.0), axis=-1, keepdims=True)
    lo = N_GROUPS + gsel * EXP_PER_GROUP
    in_grp = jnp.where(lane >= lo, jnp.where(lane < lo + EXP_PER_GROUP, 1.0, 0.0), 0.0)
    el = jnp.where(in_grp > 0.5, logits, NEG)
    e = jnp.exp(el - jnp.max(el, axis=-1, keepdims=True)) * in_grp
    p = e / jnp.sum(e, axis=-1, keepdims=True)
    pm = jnp.where(in_grp > 0.5, p, -1.0)
    v1 = jnp.max(pm, axis=-1, keepdims=True)
    i1 = jnp.min(jnp.where(pm == v1, lane, big), axis=-1, keepdims=True)
    pm2 = jnp.where(lane == i1, -1.0, pm)
    v2 = jnp.max(pm2, axis=-1, keepdims=True)
    i2 = jnp.min(jnp.where(pm2 == v2, lane, big), axis=-1, keepdims=True)
    tot = v1 + v2
    out = jnp.where(lane == 0, i1 - N_GROUPS,
                    jnp.where(lane == 1, i2 - N_GROUPS,
                              jnp.where(lane == 2, v1 / tot * gp, jnp.where(lane == 3, v2 / tot * gp, 0.0))))
    o_ref[...] = out


def _router(h, w_cat, b_cat):
    n = h.shape[0]
    tm = min(256, n)
    return pl.pallas_call(
        _router_kernel,
        grid=(n // tm,),
        in_specs=[pl.BlockSpec((tm, D_MODEL), lambda i: (i, 0)), _const_spec((D_MODEL, 128)), _const_spec((1, 128))],
        out_specs=pl.BlockSpec((tm, 128), lambda i: (i, 0)),
        out_shape=jax.ShapeDtypeStruct((n, 128), F32),
        compiler_params=_cparams(("parallel",)),
        name="moe_router",
    )(h, w_cat, b_cat)


def _issue_rows(idx_ref, base, n_rows, src_ref, dst_ref, sem):
    def body(r, c):
        pltpu.make_async_copy(src_ref.at[pl.ds(idx_ref[base + r], 1), :], dst_ref.at[pl.ds(r, 1), :], sem).start()
        return c

    lax.fori_loop(0, n_rows, body, 0, unroll=8)


def _wait_rows(n_rows, src_ref, dst_ref, sem):
    def body(r, c):
        pltpu.make_async_copy(src_ref.at[pl.ds(0, 1), :], dst_ref.at[pl.ds(r, 1), :], sem).wait()
        return c

    lax.fori_loop(0, n_rows, body, 0, unroll=8)


def _moe_kernel(te_ref, nu_ref, rt_ref, h_ref, wg_ref, wu_ref, wd_ref, o_ref, xbuf_ref, sems, wgb_ref, wub_ref, wdb_ref):
    t = pl.program_id(0)
    tm = o_ref.shape[0]
    slot = t % 2
    n_used = nu_ref[0]

    @pl.when(t == 0)
    def _():
        _issue_rows(rt_ref, 0, tm, h_ref, xbuf_ref.at[0], sems.at[0])

    @pl.when(t + 1 < n_used)
    def _():
        _issue_rows(rt_ref, (t + 1) * tm, tm, h_ref, xbuf_ref.at[1 - slot], sems.at[1 - slot])

    changed = jnp.logical_or(t == 0, te_ref[t] != te_ref[jnp.maximum(t - 1, 0)])

    @pl.when(changed)
    def _():
        wgb_ref[...] = wg_ref[0].astype(BF16)
        wub_ref[...] = wu_ref[0].astype(BF16)
        wdb_ref[...] = wd_ref[0].astype(BF16)

    @pl.when(jnp.logical_or(t == 0, t < n_used))
    def _():
        _wait_rows(tm, h_ref, xbuf_ref.at[slot], sems.at[slot])
        xb = xbuf_ref[slot].astype(BF16)
        hg = _silu(_dot(xb, wgb_ref[...])) * _dot(xb, wub_ref[...])
        o_ref[...] = _dot(hg.astype(BF16), wdb_ref[...])

    @pl.when(jnp.logical_and(t > 0, t >= n_used))
    def _():
        o_ref[...] = jnp.zeros(o_ref.shape, F32)


def _moe_experts(tile_expert, n_used, row_token, h, w_gate, w_up, w_down, layer):
    tm = MOE_TILE
    r = row_token.shape[0]
    wspec = lambda shape: pl.BlockSpec((None, 1) + shape, lambda t, te, nu, rt: (layer, te[t], 0, 0))
    grid_spec = pltpu.PrefetchScalarGridSpec(
        num_scalar_prefetch=3,
        grid=(r // tm,),
        in_specs=[pl.BlockSpec(memory_space=pl.ANY),
                  wspec((D_MODEL, D_EXPERT)), wspec((D_MODEL, D_EXPERT)), wspec((D_EXPERT, D_MODEL))],
        out_specs=pl.BlockSpec((tm, D_MODEL), lambda t, te, nu, rt: (t, 0)),
        scratch_shapes=[pltpu.VMEM((2, tm, D_MODEL), F32), pltpu.SemaphoreType.DMA((2,)),
                        pltpu.VMEM((D_MODEL, D_EXPERT), BF16), pltpu.VMEM((D_MODEL, D_EXPERT), BF16),
                        pltpu.VMEM((D_EXPERT, D_MODEL), BF16)])
    return pl.pallas_call(
        _moe_kernel,
        grid_spec=grid_spec,
        out_shape=jax.ShapeDtypeStruct((r, D_MODEL), F32),
        compiler_params=_cparams(("arbitrary",), disable_bounds_checks=True),
        name="moe_experts",
    )(tile_expert, n_used, row_token, h, w_gate, w_up, w_down)


def _moe_schedule(eid):
    n = eid.shape[0]
    tm = MOE_TILE
    n_rows = ((2 * n + N_EXPERTS * (tm - 1)) // tm + 1) * tm
    flat_e = eid.reshape(-1)
    order = jnp.argsort(flat_e, stable=True).astype(jnp.int32)
    rank = jnp.argsort(order).astype(jnp.int32)
    counts = jnp.sum((flat_e[:, None] == jnp.arange(N_EXPERTS, dtype=jnp.int32)[None, :]).astype(jnp.int32), axis=0)
    padded = (counts + tm - 1) // tm * tm
    pad_end = jnp.cumsum(padded)
    g_start = pad_end - padded
    u_start = jnp.cumsum(counts) - counts
    pos = (rank - u_start[flat_e] + g_start[flat_e]).reshape(n, 2)
    n_used = (pad_end[-1] // tm).astype(jnp.int32)
    tile_id = jnp.arange(n_rows // tm, dtype=jnp.int32)
    tile_expert = jnp.minimum(jnp.searchsorted(pad_end, tile_id * tm, side="right"), N_EXPERTS - 1).astype(jnp.int32)
    tile_expert = jnp.where(tile_id < n_used, tile_expert, tile_expert[jnp.maximum(n_used - 1, 0)])
    rows = jnp.arange(n_rows, dtype=jnp.int32)
    e_row = tile_expert[rows // tm]
    off = rows - g_start[e_row]
    src = jnp.clip(u_start[e_row] + off, 0, 2 * n - 1)
    row_token = jnp.where(off < counts[e_row], order[src] // 2, 0).astype(jnp.int32)
    return row_token, pos, tile_expert, n_used.reshape(1)


def _ln2_kernel(idx_ref, h_ref, rt_ref, ys_ref, g_ref, b_ref, o_ref, ybuf_ref, sems, *, alpha):
    i = pl.program_id(0)
    tm = h_ref.shape[0]
    slot = i % 2

    @pl.when(i == 0)
    def _():
        _issue_rows(idx_ref, 0, 2 * tm, ys_ref, ybuf_ref.at[0], sems.at[0])

    @pl.when(i + 1 < pl.num_programs(0))
    def _():
        _issue_rows(idx_ref, (i + 1) * 2 * tm, 2 * tm, ys_ref, ybuf_ref.at[1 - slot], sems.at[1 - slot])

    _wait_rows(2 * tm, ys_ref, ybuf_ref.at[slot], sems.at[slot])
    w0 = rt_ref[:, 2:3]
    w1 = rt_ref[:, 3:4]
    moe = w0 * ybuf_ref[slot, 0:tm, :] + w1 * ybuf_ref[slot, tm:2 * tm, :]
    o_ref[...] = _layer_norm(alpha * h_ref[...] + moe, g_ref[...], b_ref[...], LN_EPS)


def _combine_ln2(h, routed, pos, ys, g, b, alpha):
    n = h.shape[0]
    tm = min(128, n)
    nt = n // tm
    idx = jnp.transpose(pos.reshape(nt, tm, 2), (0, 2, 1)).reshape(-1)
    grid_spec = pltpu.PrefetchScalarGridSpec(
        num_scalar_prefetch=1,
        grid=(nt,),
        in_specs=[pl.BlockSpec((tm, D_MODEL), lambda i, idx_ref: (i, 0)),
                  pl.BlockSpec((tm, 128), lambda i, idx_ref: (i, 0)),
                  pl.BlockSpec(memory_space=pl.ANY),
                  pl.BlockSpec((1, D_MODEL), lambda i, idx_ref: (0, 0)),
                  pl.BlockSpec((1, D_MODEL), lambda i, idx_ref: (0, 0))],
        out_specs=pl.BlockSpec((tm, D_MODEL), lambda i, idx_ref: (i, 0)),
        scratch_shapes=[pltpu.VMEM((2, 2 * tm, D_MODEL), F32), pltpu.SemaphoreType.DMA((2,))])
    return pl.pallas_call(
        functools.partial(_ln2_kernel, alpha=alpha),
        grid_spec=grid_spec,
        out_shape=jax.ShapeDtypeStruct((n, D_MODEL), F32),
        compiler_params=_cparams(("arbitrary",), disable_bounds_checks=True),
        name="combine_ln2",
    )(idx, h, routed, ys, g, b)


def _relayout_cols(w):
    pad = lambda n: jnp.zeros(w.shape[:-1] + (n,), w.dtype)
    d0 = 3864
    return jnp.concatenate([
        w[..., 0:2328], pad(C_BG - 2328),
        w[..., 2328:3864],
        w[..., d0:d0 + 1536],
        w[..., d0 + 1536:d0 + 1568], pad(96),
        w[..., d0 + 1568:d0 + 1600], pad(96),
        w[..., d0 + 1600:d0 + 1696], pad(N_Z - C_GL - 96)], axis=-1)


def _relayout_d(v):
    pad = lambda n: jnp.zeros(v.shape[:-1] + (n,), v.dtype)
    return jnp.concatenate([v[..., 0:1536], v[..., 1536:1568], pad(96), v[..., 1568:1600], pad(96),
                            v[..., 1600:1696], pad(32)], axis=-1)


def _shift_cols(zrow):
    return jnp.concatenate([zrow[..., C_R:C_R + 1536], zrow[..., C_WL:C_WL + 32], zrow[..., C_AL:C_AL + 32],
                            zrow[..., C_GL:C_GL + 96]], axis=-1)


def _pad_rows(w, rows):
    return jnp.concatenate([w, jnp.zeros((rows - w.shape[0],) + w.shape[1:], w.dtype)], axis=0)


def _layer_params(l, P):
    wc = P["nsa_wc"][l]
    half = 2 * CMP_STRIDE // 2
    wcat = jnp.concatenate([wc[:, :half].reshape(2, half * HEAD_DIM, HEAD_DIM),
                            wc[:, half:].reshape(2, half * HEAD_DIM, HEAD_DIM)], axis=-1)
    pe = P["nsa_pe"][l]
    pe2 = jnp.stack([pe[:, :half].reshape(2, half * HEAD_DIM), pe[:, half:].reshape(2, half * HEAD_DIM)], axis=1)
    pe2 = jnp.concatenate([pe2, jnp.zeros((2, 6, half * HEAD_DIM), F32)], axis=1)
    router_w = jnp.concatenate([P["moe_w_grp"][l], P["moe_w_rte"][l],
                                jnp.zeros((D_MODEL, 128 - N_GROUPS - N_EXPERTS), F32)], axis=1)
    router_b = jnp.concatenate([P["moe_b_grp"][l], P["moe_b_rte"][l],
                                jnp.zeros((128 - N_GROUPS - N_EXPERTS,), F32)])[None, :]
    row = lambda name: P[name][l][None, :]
    return dict(
        w_in=_relayout_cols(P["w_in"][l]).astype(BF16), b_in=_relayout_cols(P["b_in"][l])[None, :],
        w_out=P["w_out"][l].astype(BF16), b_out=row("b_out"),
        ln1_g=row("ln1_g"), ln1_b=row("ln1_b"), ln2_g=row("ln2_g"), ln2_b=row("ln2_b"),
        conv_a_w=P["conv_a_w"][l], conv_a_b=row("conv_a_b"), ln_a_g=row("ln_a_g"), ln_a_b=row("ln_a_b"),
        conv_c_w=P["conv_c_w"][l], wcat=wcat, pe2=pe2,
        rw_mu=_relayout_d(P["rw_mu"][l])[None, :], rw_w0=row("rw_w0"), rw_w2=_pad_rows(P["rw_w2"][l], 128),
        rw_a0=row("rw_a0"), rw_a2=_pad_rows(P["rw_a2"][l], 128), rw_g2=_pad_rows(P["rw_g2"][l], 128),
        rw_kk=row("rw_kk"), rw_ka=row("rw_ka"), rw_rk=P["rw_rk"][l].reshape(1, GW),
        rw_lnx_g=P["rw_lnx_g"][l].reshape(HEADS, HEAD_DIM), rw_lnx_b=P["rw_lnx_b"][l].reshape(HEADS, HEAD_DIM),
        router_w=router_w, router_b=router_b)


def _mix_rwkv(z, shift, s0, lp, t_real):
    B, T, _ = z.shape
    r, w, k, v, kk, b, bonus, g = _rwkv_prep(z, shift, lp["rw_mu"], lp["rw_w0"], lp["rw_w2"], lp["rw_a0"],
                                             lp["rw_a2"], lp["rw_g2"], lp["rw_kk"], lp["rw_ka"], lp["rw_rk"], t_real)
    if t_real == T and T % _RW_CHUNK == 0:
        y, s_new = _rwkv_chunked(r, w, k, v, kk, b, bonus, s0, lp["rw_lnx_g"], lp["rw_lnx_b"])
    else:
        y, s_new = _rwkv_scan(r, w, k, v, kk, b, bonus, s0, lp["rw_lnx_g"], lp["rw_lnx_b"], t_real)
    yd = jnp.transpose(y, (0, 2, 1, 3)).reshape(B, T, GW)
    return yd, g, s_new


def _nsa_prompt_group(z, lp):
    B, T, _ = z.shape
    kv6 = z[:, :, C_ROWS:C_GATES].reshape(B, T, 6, KV_HEADS, HEAD_DIM)
    nc = T // CMP_STRIDE
    cmp_in = kv6[:, :, 0:2].reshape(B, nc, CMP_STRIDE, 2, KV_HEADS, HEAD_DIM)
    cmp_in = jnp.transpose(cmp_in, (0, 3, 4, 1, 2, 5)).reshape(B, 2, KV_HEADS, nc, CMP_STRIDE * HEAD_DIM)
    ckv = _compress_prompt(cmp_in, lp["wcat"], lp["pe2"])
    keyT = lambda x, tile: jnp.transpose(x.astype(BF16).reshape(B, T // tile, tile, KV_HEADS, HEAD_DIM), (0, 3, 1, 4, 2))
    val = lambda x: jnp.transpose(x.astype(BF16), (0, 2, 1, 3))
    yb = _nsa_prompt(z, ckv, keyT(kv6[:, :, 2], _KT), val(kv6[:, :, 3]), keyT(kv6[:, :, 4], Q_BLOCK), val(kv6[:, :, 5]))
    rows = kv6[:, :, 0:4]
    win = kv6[:, T - min(WINDOW, T):, 4:6]
    return yb, rows, win


def _cache_views(cache_nsa, cache_win):
    L, n_pool = cache_nsa.shape[0], cache_nsa.shape[1]
    cpp = PAGE // CMP_STRIDE
    cache_t = jnp.transpose(cache_nsa, (0, 1, 3, 4, 5, 2))
    win_t = jnp.transpose(cache_win, (0, 1, 3, 4, 5, 2))
    cmp_pages = cache_t[:, :, 0:2].reshape(L, n_pool, 2, KV_HEADS, HEAD_DIM, cpp, CMP_STRIDE)
    cmp_pages = jnp.transpose(cmp_pages, (0, 1, 2, 3, 5, 6, 4)).reshape(L, n_pool, 2 * KV_HEADS, cpp, CMP_STRIDE * HEAD_DIM)
    return cache_t, win_t, cmp_pages


def _nsa_sample_group(z, views, layer, page_table, win_l, lp, t_real):
    B, t_pad, _ = z.shape
    cache_t, win_t, cmp_pages = views
    past_len = page_table.shape[1] * PAGE
    ocmp, selm = _nsa_sample_a(page_table, cmp_pages, layer, lp["wcat"], lp["pe2"], z, past_len)
    yb = _nsa_sample_b(page_table, cache_t, win_t, layer, z, selm, ocmp, past_len, t_real)
    zr = z[:, :t_real]
    rows = zr[:, :, C_ROWS:C_KW].reshape(B, t_real, 4, KV_HEADS, HEAD_DIM)
    new_win = zr[:, :, C_KW:C_GATES].reshape(B, t_real, 2, KV_HEADS, HEAD_DIM)
    win = jnp.concatenate([win_l, new_win], axis=1)[:, t_real:]
    return yb, rows, win


def _group_layer(x, lp, alpha, nsa_fn, conv_a_buf, conv_c_buf, rw_state, rw_shift, t_real):
    B, T, _ = x.shape
    z = _proj_in(x.reshape(B * T, D_MODEL), lp["w_in"], lp["b_in"]).reshape(B, T, N_Z)
    ya, yc, conv_a_new, conv_c_new = _conv_mix(z, conv_a_buf, conv_c_buf, lp["conv_a_w"], lp["conv_a_b"],
                                               lp["ln_a_g"], lp["ln_a_b"], lp["conv_c_w"], t_real)
    yb, nsa_rows, win_new = nsa_fn(z)
    yd, g, rw_state_new = _mix_rwkv(z, _relayout_d(rw_shift)[:, None, :], rw_state, lp, t_real)
    flat = lambda a: a.reshape(B * T, a.shape[-1])
    h = _proj_out(flat(x), flat(ya), flat(yb), flat(yc), flat(yd), flat(g), lp["w_out"], lp["b_out"],
                  lp["ln1_g"], lp["ln1_b"], alpha)
    shift_new = _shift_cols(z[:, t_real - 1])
    return h, (nsa_rows, win_new, conv_a_new, conv_c_new, rw_state_new, shift_new)


def _moe_ln2(h_list, lp, P, layer, alpha):
    routed = [_router(h, lp["router_w"], lp["router_b"]) for h in h_list]
    eid = jnp.concatenate([r[:, 0:2] for r in routed], axis=0).astype(jnp.int32)
    h_all = jnp.concatenate(h_list, axis=0)
    row_token, pos, tile_expert, n_used = _moe_schedule(eid)
    ys = _moe_experts(tile_expert, n_used, row_token, h_all, P["moe_w_gate"], P["moe_w_up"], P["moe_w_down"], layer)
    outs = []
    off = 0
    for h, rt in zip(h_list, routed):
        n = h.shape[0]
        outs.append(_combine_ln2(h, rt, pos[off:off + n], ys, lp["ln2_g"], lp["ln2_b"], alpha))
        off += n
    return outs


def kernel(x_prompt, x_sample, cache_nsa, cache_win, state_conv_a, state_conv_c, state_rwkv, state_rwkv_shift, page_table, w_in, b_in, w_out, b_out, ln1_g, ln1_b, ln2_g, ln2_b, conv_a_w, conv_a_b, ln_a_g, ln_a_b, conv_c_w, nsa_wc, nsa_pe, rw_mu, rw_w0, rw_w2, rw_a0, rw_a2, rw_g2, rw_kk, rw_ka, rw_rk, rw_lnx_g, rw_lnx_b, moe_w_grp, moe_b_grp, moe_w_rte, moe_b_rte, moe_w_gate, moe_w_up, moe_w_down):
    P = dict(w_in=w_in, b_in=b_in, w_out=w_out, b_out=b_out, ln1_g=ln1_g, ln1_b=ln1_b, ln2_g=ln2_g, ln2_b=ln2_b,
             conv_a_w=conv_a_w, conv_a_b=conv_a_b, ln_a_g=ln_a_g, ln_a_b=ln_a_b, conv_c_w=conv_c_w,
             nsa_wc=nsa_wc, nsa_pe=nsa_pe, rw_mu=rw_mu, rw_w0=rw_w0, rw_w2=rw_w2, rw_a0=rw_a0, rw_a2=rw_a2,
             rw_g2=rw_g2, rw_kk=rw_kk, rw_ka=rw_ka, rw_rk=rw_rk, rw_lnx_g=rw_lnx_g, rw_lnx_b=rw_lnx_b,
             moe_w_grp=moe_w_grp, moe_b_grp=moe_b_grp, moe_w_rte=moe_w_rte, moe_b_rte=moe_b_rte,
             moe_w_gate=moe_w_gate, moe_w_up=moe_w_up, moe_w_down=moe_w_down)
    depth = w_in.shape[0]
    alpha = (2.0 * depth) ** 0.25
    bp, tp, _ = x_prompt.shape
    bs, ts, _ = x_sample.shape
    ts_pad = 8
    xp = x_prompt
    xs = jnp.concatenate([x_sample, jnp.zeros((bs, ts_pad - ts, D_MODEL), F32)], axis=1)
    outs_p, outs_s = [], []
    views = _cache_views(cache_nsa, cache_win)
    for l in range(depth):
        lp = _layer_params(l, P)
        h_p, st_p = _group_layer(xp, lp, alpha, functools.partial(_nsa_prompt_group, lp=lp),
                                 jnp.zeros((bp, CONV_A_WIDTH - 1, GW), F32), jnp.zeros((bp, SHORT_CONV_WIDTH - 1, GW), F32),
                                 jnp.zeros((bp, HEADS, HEAD_DIM, HEAD_DIM), F32), jnp.zeros((bp, N_COLS_D), F32), tp)
        h_s, st_s = _group_layer(xs, lp, alpha,
                                 functools.partial(_nsa_sample_group, views=views, layer=l, page_table=page_table,
                                                   win_l=cache_win[l], lp=lp, t_real=ts),
                                 state_conv_a[l], state_conv_c[l], state_rwkv[l], state_rwkv_shift[l], ts)
        o_p, o_s = _moe_ln2([h_p, h_s], lp, P, l, alpha)
        xp = o_p.reshape(bp, tp, D_MODEL)
        xs = o_s.reshape(bs, ts_pad, D_MODEL)
        outs_p.append(st_p)
        outs_s.append(st_s)
    stk = lambda outs, i: jnp.stack([o[i] for o in outs], axis=0)
    return (xp, xs[:, :ts], stk(outs_p, 0), stk(outs_s, 0), stk(outs_p, 1), stk(outs_s, 1), stk(outs_p, 2), stk(outs_s, 2),
            stk(outs_p, 3), stk(outs_s, 3), stk(outs_p, 4), stk(outs_s, 4), stk(outs_p, 5), stk(outs_s, 5))
```

```python
import functools

import numpy as np
import jax
import jax.numpy as jnp
from jax import lax
from jax.experimental import pallas as pl
from jax.experimental.pallas import tpu as pltpu

F32 = jnp.float32
BF16 = jnp.bfloat16

D_MODEL = 2048
DEPTH = 2
HEAD_DIM = 64
GW = 512
HEADS = 8
CONV_A_WIDTH = 31
SHORT_CONV_WIDTH = 3
KV_HEADS = 2
Q_PER_KV = 4
CMP_STRIDE = 16
SLC_BLOCK = 64
N_SLC = 16
WINDOW = 512
Q_BLOCK = 128
PAGE = 128
RWKV_GN_EPS = 64e-5
LN_EPS = 1e-5
NEG = -1e30
FORCED = 1e4
N_EXPERTS = 32
EXP_PER_GROUP = 8
N_GROUPS = 4
D_EXPERT = 512
N_IN = 5560
N_COLS_D = 1696

N_Z = 6144
C_VAL, C_GATE = 0, 512
C_Q, C_ROWS, C_KW, C_GATES = 1024, 1536, 2048, 2304
C_BG, C_CG, C_H = 2560, 3072, 3584
C_R, C_K, C_V = 4096, 4608, 5120
C_WL, C_AL, C_GL = 5632, 5760, 5888

VMEM_LIMIT = 56 * 1024 * 1024
MOE_TILE = 256


def _cparams(sem, vmem=VMEM_LIMIT, disable_bounds_checks=False):
    return pltpu.CompilerParams(dimension_semantics=sem, vmem_limit_bytes=vmem,
                                disable_bounds_checks=disable_bounds_checks)


def _const_spec(shape):
    nd = len(shape)
    return pl.BlockSpec(shape, lambda *_: (0,) * nd)


def _split2(x):
    hi = x.astype(BF16)
    lo = (x - hi.astype(F32)).astype(BF16)
    return hi, lo


def _split3(x):
    hi = x.astype(BF16)
    r1 = x - hi.astype(F32)
    mid = r1.astype(BF16)
    lo = (r1 - mid.astype(F32)).astype(BF16)
    return hi, mid, lo


_NN = (((1,), (0,)), ((), ()))
_NT = (((1,), (1,)), ((), ()))


def _dot(a, b, dims=_NN):
    return lax.dot_general(a, b, dims, preferred_element_type=F32)


def _dot3(a, b, dims=_NN):
    ah, al = _split2(a)
    bh, bl = _split2(b)
    return _dot(ah, bh, dims) + _dot(al, bh, dims) + _dot(ah, bl, dims)


def _dot_exact_rhs(a, b_bf16, dims=_NN):
    hi, mid, lo = _split3(a)
    return _dot(hi, b_bf16, dims) + _dot(mid, b_bf16, dims) + _dot(lo, b_bf16, dims)


def _dot_exact_lhs(a_bf16, b, dims=_NN):
    hi, mid, lo = _split3(b)
    return _dot(a_bf16, hi, dims) + _dot(a_bf16, mid, dims) + _dot(a_bf16, lo, dims)


def _sigmoid(x):
    return 1.0 / (1.0 + jnp.exp(-x))


def _silu(x):
    return x * _sigmoid(x)


def _layer_norm(x, g, b, eps):
    mu = jnp.mean(x, axis=-1, keepdims=True)
    xc = x - mu
    var = jnp.mean(xc * xc, axis=-1, keepdims=True)
    return xc * lax.rsqrt(var + eps) * g + b


def _masked_softmax(s, maskf):
    s = jnp.where(maskf > 0.5, s, NEG)
    e = jnp.exp(s - jnp.max(s, axis=-1, keepdims=True)) * maskf
    return e / jnp.maximum(jnp.sum(e, axis=-1, keepdims=True), 1e-30)


def _proj_in_kernel(x_ref, w_ref, b_ref, o_ref, xb_ref):
    @pl.when(pl.program_id(1) == 0)
    def _():
        xb_ref[...] = x_ref[...].astype(BF16)

    o_ref[...] = _dot(xb_ref[...], w_ref[...]) + b_ref[...]


def _proj_in(x2d, w_bf16, b_row):
    n = x2d.shape[0]
    tm = min(1024, n)
    tn = 512
    return pl.pallas_call(
        _proj_in_kernel,
        grid=(n // tm, N_Z // tn),
        in_specs=[pl.BlockSpec((tm, D_MODEL), lambda i, j: (i, 0)),
                  pl.BlockSpec((D_MODEL, tn), lambda i, j: (0, j)),
                  pl.BlockSpec((1, tn), lambda i, j: (0, j))],
        out_specs=pl.BlockSpec((tm, tn), lambda i, j: (i, j)),
        out_shape=jax.ShapeDtypeStruct((n, N_Z), F32),
        scratch_shapes=[pltpu.VMEM((tm, D_MODEL), BF16)],
        compiler_params=_cparams(("parallel", "arbitrary")),
        name="proj_in",
    )(x2d, w_bf16, b_row)


_CONV_ROWS = 32


def _conv_kernel(val_ref, gate_ref, bg_ref, cg_ref, h_ref, bufa_ref, bufc_ref, wa_ref, ba_ref, lag_ref,
                 lab_ref, wc_ref, ya_ref, yc_ref, na_ref, nc_ref, ea_ref, ec_ref, *, tt, t_last):
    it = pl.program_id(1)
    nt = pl.num_programs(1)
    ha = 32
    hc = 8

    @pl.when(it == 0)
    def _():
        ea_ref[0:ha, :] = jnp.zeros((ha, GW), F32)
        ea_ref[ha - (CONV_A_WIDTH - 1):ha, :] = bufa_ref[0]
        ec_ref[0:hc, :] = jnp.zeros((hc, GW), F32)
        ec_ref[hc - (SHORT_CONV_WIDTH - 1):hc, :] = bufc_ref[0]

    @pl.when(it > 0)
    def _():
        ea_ref[0:ha, :] = ea_ref[tt:tt + ha, :]
        ec_ref[0:hc, :] = ec_ref[tt:tt + hc, :]

    ea_ref[ha:ha + tt, :] = val_ref[0] * _sigmoid(gate_ref[0])
    ec_ref[hc:hc + tt, :] = cg_ref[0] * h_ref[0]

    rc = min(_CONV_ROWS, tt)
    offa = ha - (CONV_A_WIDTH - 1)
    offc = hc - (SHORT_CONV_WIDTH - 1)
    for c in range(tt // rc):
        r0 = c * rc
        acc = jnp.zeros((rc, GW), F32)
        for j in range(CONV_A_WIDTH):
            acc = acc + wa_ref[j:j + 1, :] * ea_ref[r0 + offa + j:r0 + offa + j + rc, :]
        y = _layer_norm(acc + ba_ref[...], lag_ref[...], lab_ref[...], LN_EPS)
        ya_ref[0, r0:r0 + rc, :] = _silu(y)
        accc = jnp.zeros((rc, GW), F32)
        for j in range(SHORT_CONV_WIDTH):
            accc = accc + wc_ref[j:j + 1, :] * ec_ref[r0 + offc + j:r0 + offc + j + rc, :]
        yc_ref[0, r0:r0 + rc, :] = bg_ref[0, r0:r0 + rc, :] * accc

    @pl.when(it == nt - 1)
    def _():
        na_ref[0] = ea_ref[t_last + offa:t_last + ha, :]
        nc_ref[0] = ec_ref[t_last + offc:t_last + hc, :]


def _conv_mix(z, buf_a, buf_c, wa, ba, lag, lab, wc, t_real):
    B, T, _ = z.shape
    tt = min(256, T)
    nt = T // tt
    t_last = t_real - (nt - 1) * tt
    zspec = lambda col: pl.BlockSpec((1, tt, GW), lambda b, i: (b, i, col // GW))
    return pl.pallas_call(
        functools.partial(_conv_kernel, tt=tt, t_last=t_last),
        grid=(B, nt),
        in_specs=[zspec(C_VAL), zspec(C_GATE), zspec(C_BG), zspec(C_CG), zspec(C_H),
                  pl.BlockSpec((1, CONV_A_WIDTH - 1, GW), lambda b, i: (b, 0, 0)),
                  pl.BlockSpec((1, SHORT_CONV_WIDTH - 1, GW), lambda b, i: (b, 0, 0)),
                  _const_spec((CONV_A_WIDTH, GW)), _const_spec((1, GW)), _const_spec((1, GW)),
                  _const_spec((1, GW)), _const_spec((SHORT_CONV_WIDTH, GW))],
        out_specs=[pl.BlockSpec((1, tt, GW), lambda b, i: (b, i, 0)),
                   pl.BlockSpec((1, tt, GW), lambda b, i: (b, i, 0)),
                   pl.BlockSpec((1, CONV_A_WIDTH - 1, GW), lambda b, i: (b, 0, 0)),
                   pl.BlockSpec((1, SHORT_CONV_WIDTH - 1, GW), lambda b, i: (b, 0, 0))],
        out_shape=[jax.ShapeDtypeStruct((B, T, GW), F32), jax.ShapeDtypeStruct((B, T, GW), F32),
                   jax.ShapeDtypeStruct((B, CONV_A_WIDTH - 1, GW), F32),
                   jax.ShapeDtypeStruct((B, SHORT_CONV_WIDTH - 1, GW), F32)],
        scratch_shapes=[pltpu.VMEM((tt + 32, GW), F32), pltpu.VMEM((tt + 8, GW), F32)],
        compiler_params=_cparams(("parallel", "arbitrary")),
        name="conv_mix",
    )(z, z, z, z, z, buf_a, buf_c, wa, ba, lag, lab, wc)


def _rwkv_prep_kernel(r_ref, k_ref, v_ref, wl_ref, al_ref, gl_ref, sh_ref, mu_ref, w0_ref, w2_ref, a0_ref,
                      a2_ref, g2_ref, kkp_ref, kap_ref, rk_ref,
                      ro_ref, wo_ref, ko_ref, vo_ref, kko_ref, bo_ref, bonus_ref, g_ref, carry_ref, *, tt, t_last):
    it = pl.program_id(1)

    @pl.when(it == 0)
    def _():
        carry_ref[...] = sh_ref[0]

    row = lax.broadcasted_iota(jnp.int32, (tt, 1), 0)

    def mixed(x, c0):
        w = x.shape[1]
        prev = pltpu.roll(x, 1, 0) if tt > 1 else x
        prev = jnp.where(row == 0, carry_ref[:, c0:c0 + w], prev)
        return x + (prev - x) * mu_ref[:, c0:c0 + w]

    xr, xk, xv = r_ref[0], k_ref[0], v_ref[0]
    xwl, xal, xgl = wl_ref[0], al_ref[0], gl_ref[0]
    r = mixed(xr, 0)
    k = mixed(xk, GW)
    v = mixed(xv, 2 * GW)
    wl = mixed(xwl, 3 * GW)
    al = mixed(xal, 3 * GW + 128)
    gl = mixed(xgl, 3 * GW + 256)
    lr = t_last - 1
    carry_ref[:, 0:GW] = xr[lr:lr + 1]
    carry_ref[:, GW:2 * GW] = xk[lr:lr + 1]
    carry_ref[:, 2 * GW:3 * GW] = xv[lr:lr + 1]
    carry_ref[:, 3 * GW:3 * GW + 128] = xwl[lr:lr + 1]
    carry_ref[:, 3 * GW + 128:3 * GW + 256] = xal[lr:lr + 1]
    carry_ref[:, 3 * GW + 256:3 * GW + 384] = xgl[lr:lr + 1]

    y = w0_ref[...] + _dot3(jnp.tanh(wl), w2_ref[...])
    u = -y
    w_log = -(jnp.maximum(u, 0.0) + jnp.log(1.0 + jnp.exp(-jnp.abs(u)))) - 0.5
    log_decay = -jnp.exp(w_log)
    a = _sigmoid(a0_ref[...] + _dot3(al, a2_ref[...]))
    g_ref[0] = _dot3(_sigmoid(gl), g2_ref[...])
    kk = k * kkp_ref[...]
    k2 = k * (1.0 + (a - 1.0) * kap_ref[...])
    rkk = r * k2 * rk_ref[...]
    for h in range(HEADS):
        sl = slice(h * HEAD_DIM, (h + 1) * HEAD_DIM)
        kk_h = kk[:, sl]
        nrm = jnp.sqrt(jnp.sum(kk_h * kk_h, axis=-1, keepdims=True))
        kk_h = kk_h / jnp.maximum(nrm, 1e-12)
        v_h = v[:, sl]
        ro_ref[0, h] = r[:, sl]
        wo_ref[0, h] = log_decay[:, sl]
        ko_ref[0, h] = k2[:, sl]
        vo_ref[0, h] = v_h
        kko_ref[0, h] = kk_h
        bo_ref[0, h] = kk_h * a[:, sl]
        bonus_ref[0, h] = jnp.sum(rkk[:, sl], axis=-1, keepdims=True) * v_h


def _rwkv_prep(z, shift, mu, w0, w2p, a0, a2p, g2p, kkp, kap, rk, t_real):
    B, T, _ = z.shape
    tt = min(256, T)
    nt = T // tt
    t_last = t_real - (nt - 1) * tt
    wide = lambda col: pl.BlockSpec((1, tt, GW), lambda b, i: (b, i, col // GW))
    thin = lambda col: pl.BlockSpec((1, tt, 128), lambda b, i: (b, i, col // 128))
    head_out = pl.BlockSpec((1, HEADS, tt, HEAD_DIM), lambda b, i: (b, 0, i, 0))
    head_shape = jax.ShapeDtypeStruct((B, HEADS, T, HEAD_DIM), F32)
    dw = 3 * GW + 384
    return pl.pallas_call(
        functools.partial(_rwkv_prep_kernel, tt=tt, t_last=t_last),
        grid=(B, nt),
        in_specs=[wide(C_R), wide(C_K), wide(C_V), thin(C_WL), thin(C_AL), thin(C_GL),
                  pl.BlockSpec((1, 1, dw), lambda b, i: (b, 0, 0)),
                  _const_spec((1, dw)), _const_spec((1, GW)), _const_spec((128, GW)), _const_spec((1, GW)),
                  _const_spec((128, GW)), _const_spec((128, GW)), _const_spec((1, GW)), _const_spec((1, GW)),
                  _const_spec((1, GW))],
        out_specs=[head_out] * 7 + [pl.BlockSpec((1, tt, GW), lambda b, i: (b, i, 0))],
        out_shape=[head_shape] * 7 + [jax.ShapeDtypeStruct((B, T, GW), F32)],
        scratch_shapes=[pltpu.VMEM((1, dw), F32)],
        compiler_params=_cparams(("parallel", "arbitrary")),
        name="rwkv_prep",
    )(z, z, z, z, z, z, shift, mu, w0, w2p, a0, a2p, g2p, kkp, kap, rk)


def _rwkv_scan_kernel(r_ref, w_ref, k_ref, v_ref, kk_ref, b_ref, bonus_ref, s0_ref, lg_ref, lb_ref,
                      y_ref, s_ref, *, nb, tt, n_steps):
    it = pl.program_id(1)

    @pl.when(it == 0)
    def _():
        s_ref[...] = s0_ref[...]

    y_ref[...] = jnp.zeros(y_ref.shape, F32)
    ii = lax.broadcasted_iota(jnp.int32, (HEAD_DIM, HEAD_DIM), 0)
    jj = lax.broadcasted_iota(jnp.int32, (HEAD_DIM, HEAD_DIM), 1)
    eye = jnp.where(ii == jj, 1.0, 0.0).astype(F32)

    def step(t, carry):
        for n in range(nb):
            for h in range(HEADS):
                row = lambda ref: ref[n, h, pl.ds(t, 1), :]
                S = s_ref[n, h]
                sa = jnp.sum(S * row(kk_ref), axis=-1, keepdims=True)
                vcol = jnp.sum(eye * row(v_ref), axis=-1, keepdims=True)
                S = S * jnp.exp(row(w_ref)) - sa * row(b_ref) + vcol * row(k_ref)
                s_ref[n, h] = S
                ycol = jnp.sum(S * row(r_ref), axis=-1, keepdims=True)
                y_ref[n, h, pl.ds(t, 1), :] = jnp.sum(eye * ycol, axis=0, keepdims=True)
        return carry

    lax.fori_loop(0, n_steps, step, 0)

    for n in range(nb):
        for h in range(HEADS):
            y = y_ref[n, h]
            mu = jnp.mean(y, axis=-1, keepdims=True)
            yc = y - mu
            var = jnp.mean(yc * yc, axis=-1, keepdims=True)
            y_ref[n, h] = yc * lax.rsqrt(var + RWKV_GN_EPS) * lg_ref[h:h + 1, :] + lb_ref[h:h + 1, :] + bonus_ref[n, h]


def _rwkv_scan(r, w, k, v, kk, b, bonus, s0, lnx_g, lnx_b, t_real):
    B, H, T, _ = r.shape
    nb = 2
    tt = min(128, T)
    nt = T // tt
    n_steps = min(tt, t_real)
    blk = pl.BlockSpec((nb, H, tt, HEAD_DIM), lambda b_, i: (b_, 0, i, 0))
    sblk = pl.BlockSpec((nb, H, HEAD_DIM, HEAD_DIM), lambda b_, i: (b_, 0, 0, 0))
    return pl.pallas_call(
        functools.partial(_rwkv_scan_kernel, nb=nb, tt=tt, n_steps=n_steps),
        grid=(B // nb, nt),
        in_specs=[blk] * 7 + [sblk, _const_spec((H, HEAD_DIM)), _const_spec((H, HEAD_DIM))],
        out_specs=[blk, sblk],
        out_shape=[jax.ShapeDtypeStruct((B, H, T, HEAD_DIM), F32),
                   jax.ShapeDtypeStruct((B, H, HEAD_DIM, HEAD_DIM), F32)],
        compiler_params=_cparams(("parallel", "arbitrary")),
        name="rwkv_scan",
    )(r, w, k, v, kk, b, bonus, s0, lnx_g, lnx_b)


_RW_CHUNK = 64

_TN = (((0,), (0,)), ((), ()))


def _rwkv_chunk_kernel(r_ref, lw_ref, k_ref, v_ref, kk_ref, b_ref, bonus_ref, s0_ref, lg_ref, lb_ref, y_ref, s_ref):
    C = r_ref.shape[2]
    it = pl.program_id(1)

    @pl.when(it == 0)
    def _():
        s_ref[...] = s0_ref[...]

    ti = lax.broadcasted_iota(jnp.int32, (C, C), 0)
    si = lax.broadcasted_iota(jnp.int32, (C, C), 1)
    incl = ti >= si
    strict = ti > si
    tri_b = jnp.where(incl, 1.0, 0.0).astype(BF16)
    n_fac = C.bit_length() - 1
    hs = range(HEADS)
    v = [v_ref[0, h] for h in hs]
    cs = [_dot_exact_lhs(tri_b, lw_ref[0, h]) for h in hs]
    cs_last = [c[C - 1:C, :] for c in cs]
    AR = [jnp.concatenate([kk_ref[0, h] * jnp.exp(cs[h] - lw_ref[0, h]), r_ref[0, h] * jnp.exp(cs[h])], axis=0)
          for h in hs]
    BK = []
    for h in hs:
        g_inv = jnp.exp(-cs[h])
        BK.append(jnp.concatenate([b_ref[0, h] * g_inv, k_ref[0, h] * g_inv], axis=0))
    G = [_dot3(AR[h], BK[h], _NT) for h in hs]
    S0 = [s_ref[0, h] for h in hs]
    ARS = [_dot3(AR[h], S0[h], _NT) for h in hs]
    LMV = [_dot3(jnp.concatenate([jnp.where(strict, G[h][0:C, C:2 * C], 0.0),
                                  jnp.where(incl, G[h][C:2 * C, C:2 * C], 0.0)], axis=0), v[h]) for h in hs]
    X = [ARS[h][0:C] + LMV[h][0:C] for h in hs]
    N = [-jnp.where(strict, G[h][0:C, 0:C], 0.0) for h in hs]
    for f in range(n_fac):
        X = [X[h] + _dot3(N[h], X[h]) for h in hs]
        if f + 1 < n_fac:
            N = [_dot3(N[h], N[h]) for h in hs]
    Y = [ARS[h][C:2 * C] + LMV[h][C:2 * C] - _dot3(jnp.where(incl, G[h][C:2 * C, 0:C], 0.0), X[h]) for h in hs]
    for h in hs:
        g_end = jnp.exp(cs_last[h] - cs[h])
        UV = jnp.concatenate([-X[h], v[h]], axis=0)
        BKg = jnp.concatenate([b_ref[0, h] * g_end, k_ref[0, h] * g_end], axis=0)
        s_ref[0, h] = S0[h] * jnp.exp(cs_last[h]) + _dot3(UV, BKg, _TN)
    for h in hs:
        mu = jnp.mean(Y[h], axis=-1, keepdims=True)
        yc = Y[h] - mu
        var = jnp.mean(yc * yc, axis=-1, keepdims=True)
        y_ref[0, h] = yc * lax.rsqrt(var + RWKV_GN_EPS) * lg_ref[h:h + 1, :] + lb_ref[h:h + 1, :] + bonus_ref[0, h]


def _rwkv_chunked(r, lw, k, v, kk, b, bonus, s0, lnx_g, lnx_b):
    B, H, T, _ = r.shape
    C = _RW_CHUNK
    blk = pl.BlockSpec((1, H, C, HEAD_DIM), lambda b_, i: (b_, 0, i, 0))
    sblk = pl.BlockSpec((1, H, HEAD_DIM, HEAD_DIM), lambda b_, i: (b_, 0, 0, 0))
    return pl.pallas_call(
        _rwkv_chunk_kernel,
        grid=(B, T // C),
        in_specs=[blk] * 7 + [sblk, _const_spec((H, HEAD_DIM)), _const_spec((H, HEAD_DIM))],
        out_specs=[blk, sblk],
        out_shape=[jax.ShapeDtypeStruct((B, H, T, HEAD_DIM), F32),
                   jax.ShapeDtypeStruct((B, H, HEAD_DIM, HEAD_DIM), F32)],
        compiler_params=_cparams(("parallel", "arbitrary")),
        name="rwkv_chunked",
    )(r, lw, k, v, kk, b, bonus, s0, lnx_g, lnx_b)


def _compress_tail(F, cst):
    n = F.shape[0]
    nxt = pltpu.roll(F[:, HEAD_DIM:2 * HEAD_DIM], n - 1, 0)
    const = cst[0:1, 0:HEAD_DIM] + cst[1:2, HEAD_DIM:2 * HEAD_DIM]
    return F[:, 0:HEAD_DIM] + nxt + const


def _compress_kernel(x_ref, w_ref, pe_ref, o_ref):
    F = _dot3(x_ref[0, 0, 0], w_ref[0])
    cst = _dot3(pe_ref[0], w_ref[0])
    o_ref[0, 0, 0] = _compress_tail(F, cst)


def _compress_prompt(cmp_in, wcat, pe2):
    B, _, _, nc, _ = cmp_in.shape
    return pl.pallas_call(
        _compress_kernel,
        grid=(B, 2, 2),
        in_specs=[pl.BlockSpec((1, 1, 1, nc, 1024), lambda b, a, g: (b, a, g, 0, 0)),
                  pl.BlockSpec((1, 1024, 128), lambda b, a, g: (a, 0, 0)),
                  pl.BlockSpec((1, 8, 1024), lambda b, a, g: (a, 0, 0))],
        out_specs=pl.BlockSpec((1, 1, 1, nc, HEAD_DIM), lambda b, a, g: (b, a, g, 0, 0)),
        out_shape=jax.ShapeDtypeStruct((B, 2, 2, nc, HEAD_DIM), F32),
        compiler_params=_cparams(("parallel", "parallel", "parallel")),
        name="nsa_compress",
    )(cmp_in, wcat, pe2)


def _top_select(selT, j_f):
    v = selT
    sel = jnp.zeros(selT.shape, F32)
    for _ in range(N_SLC):
        m = jnp.max(v, axis=0, keepdims=True)
        first = jnp.min(jnp.where(v == m, j_f, 1e9), axis=0, keepdims=True)
        hit = j_f == first
        sel = jnp.where(hit, 1.0, sel)
        v = jnp.where(hit, -3e38, v)
    return sel


def _softmax2(s):
    e = jnp.exp2(s - jnp.max(s, axis=-1, keepdims=True))
    return e * (1.0 / jnp.sum(e, axis=-1, keepdims=True))


def _gated_heads(gts, g, o_cmp, o_slc, o_win, rows):
    out = []
    for m in range(Q_PER_KV):
        c0 = g * 3 * Q_PER_KV + m * 3
        sl = slice(m * rows, (m + 1) * rows)
        out.append(gts[:, c0:c0 + 1] * o_cmp[sl] + gts[:, c0 + 1:c0 + 2] * o_slc[sl] + gts[:, c0 + 2:c0 + 3] * o_win[sl])
    return out


_KT = 512


_LOG2E = 1.4426950408889634


def _nsa_prompt_kernel(q_ref, gt_ref, ckv_ref, ksT_ref, vs_ref, kwT_ref, vw_ref, mt_ref, o_ref, *, T):
    i = pl.program_id(1)
    start = i * Q_BLOCK
    nc = T // CMP_STRIDE
    ns = T // SLC_BLOCK
    rows = Q_PER_KV * Q_BLOCK
    gs = range(KV_HEADS)
    t_tok = start + lax.broadcasted_iota(jnp.int32, (Q_BLOCK, 1), 0)
    t_rows = jnp.concatenate([t_tok] * Q_PER_KV, axis=0)
    gts = _sigmoid(gt_ref[0])
    j_col = lax.broadcasted_iota(jnp.int32, (ns, 1), 0)
    j_f = j_col.astype(F32)
    t_lane = start + lax.broadcasted_iota(jnp.int32, (1, Q_BLOCK), 1)
    cur = t_lane >> 6
    forced = jnp.where(j_col == 0, 1.0, jnp.where(j_col == cur, 1.0, jnp.where(j_col == cur - 1, 1.0, 0.0)))
    r128 = lax.broadcasted_iota(jnp.int32, (Q_BLOCK, Q_BLOCK), 0)
    c128 = lax.broadcasted_iota(jnp.int32, (Q_BLOCK, Q_BLOCK), 1)
    eye128 = jnp.where(r128 == c128, 1.0, 0.0).astype(BF16)
    Q = [jnp.concatenate([q_ref[0, :, (g * Q_PER_KV + m) * HEAD_DIM:(g * Q_PER_KV + m + 1) * HEAD_DIM]
                          for m in range(Q_PER_KV)], axis=0) * (HEAD_DIM ** -0.5 * _LOG2E) for g in gs]
    Qb = [q.astype(BF16) for q in Q]

    n_lane = lax.broadcasted_iota(jnp.int32, (1, nc), 1)
    bias_c = jnp.where(n_lane * CMP_STRIDE + (2 * CMP_STRIDE - 1) <= t_rows, 0.0, NEG)
    any_c = jnp.where(t_rows >= 2 * CMP_STRIDE - 1, 1.0, 0.0)
    pc = [_softmax2(_dot3(Q[g], ckv_ref[0, 0, g], _NT) + bias_c) * any_c for g in gs]
    o_cmp = [_dot(pc[g].astype(BF16), ckv_ref[0, 1, g].astype(BF16)) for g in gs]
    selb = []
    for g in gs:
        ps = pc[g][0:Q_BLOCK] + pc[g][Q_BLOCK:2 * Q_BLOCK] + pc[g][2 * Q_BLOCK:3 * Q_BLOCK] + pc[g][3 * Q_BLOCK:4 * Q_BLOCK]
        selT = _dot_exact_lhs(mt_ref[...], ps, _NT)
        selT = jnp.where(forced > 0.5, FORCED, jnp.where(j_col * SLC_BLOCK <= t_lane, selT, -FORCED))
        selbT = jnp.where(_top_select(selT, j_f) > 0.5, 0.0, NEG).astype(BF16)
        selb.append(_dot(eye128, selbT, _NT).astype(BF16))

    def key_tile(kt, carry, causal):
        jr = lax.broadcasted_iota(jnp.int32, (ns, _KT), 0)
        cl = lax.broadcasted_iota(jnp.int32, (ns, _KT), 1)
        expand = jnp.where(jr == kt * (_KT // SLC_BLOCK) + (cl >> 6), 1.0, 0.0).astype(BF16)
        if causal:
            pos = kt * _KT + lax.broadcasted_iota(jnp.int32, (1, _KT), 1)
            bias_t = jnp.where(pos <= t_rows, 0.0, NEG)
        out = []
        for g in gs:
            m_run, l_run, acc = carry[g]
            bias = _dot(selb[g], expand)
            s = _dot(Qb[g], ksT_ref[0, g, kt]) + jnp.concatenate([bias] * Q_PER_KV, axis=0)
            if causal:
                s = s + bias_t
            m_new = jnp.maximum(m_run, jnp.max(s, axis=-1, keepdims=True))
            p = jnp.exp2(s - m_new)
            corr = jnp.exp2(m_run - m_new)
            vt = vs_ref[0, g, pl.ds(pl.multiple_of(kt * _KT, _KT), _KT), :]
            out.append((m_new, corr * l_run + jnp.sum(p, axis=-1, keepdims=True),
                        corr * acc + _dot(p.astype(BF16), vt)))
        return tuple(out)

    init = tuple((jnp.full((rows, 1), NEG, F32), jnp.zeros((rows, 1), F32), jnp.zeros((rows, HEAD_DIM), F32))
                 for _ in gs)
    last = start // _KT
    carry = lax.fori_loop(0, last, lambda kt, c: key_tile(kt, c, False), init)
    carry = key_tile(last, carry, True)
    o_slc = [carry[g][2] * (1.0 / carry[g][1]) for g in gs]

    nwt = WINDOW // Q_BLOCK + 1
    wb = jnp.maximum(i - WINDOW // Q_BLOCK, 0)
    dt = t_rows - (wb * Q_BLOCK + lax.broadcasted_iota(jnp.int32, (1, nwt * Q_BLOCK), 1))
    bias_w = jnp.where(dt >= 0, jnp.where(dt < WINDOW, 0.0, NEG), NEG)
    pieces = []
    for g in gs:
        sw = jnp.concatenate([_dot(Qb[g], kwT_ref[0, g, wb + u]) for u in range(nwt)], axis=1) + bias_w
        vwt = vw_ref[0, g, pl.ds(pl.multiple_of(wb * Q_BLOCK, Q_BLOCK), nwt * Q_BLOCK), :]
        o_win = _dot(_softmax2(sw).astype(BF16), vwt)
        pieces += _gated_heads(gts, g, o_cmp[g], o_slc[g], o_win, Q_BLOCK)
    o_ref[0] = jnp.concatenate(pieces, axis=1)


def _sel_matrix(n_rows, n_chunks):
    j = np.arange(n_rows)[:, None]
    c = np.arange(n_chunks)[None, :]
    per = SLC_BLOCK // CMP_STRIDE
    m = 0.5 * ((c // per == j).astype(np.float32) + ((c + 1) // per == j).astype(np.float32))
    return jnp.asarray(m, BF16)


def _nsa_prompt(z, ckv, ksT, vs, kwT, vw):
    B, T, _ = z.shape
    nc, ns = T // CMP_STRIDE, T // SLC_BLOCK
    nq = T // Q_BLOCK
    mt = _sel_matrix(ns, nc)
    per_b = lambda shape: pl.BlockSpec((1,) + shape, lambda b, i: (b,) + (0,) * len(shape))
    return pl.pallas_call(
        functools.partial(_nsa_prompt_kernel, T=T),
        grid=(B, nq),
        in_specs=[pl.BlockSpec((1, Q_BLOCK, GW), lambda b, i: (b, i, C_Q // GW)),
                  pl.BlockSpec((1, Q_BLOCK, 256), lambda b, i: (b, i, C_GATES // 256)),
                  per_b((2, KV_HEADS, nc, HEAD_DIM)),
                  per_b((KV_HEADS, T // _KT, HEAD_DIM, _KT)),
                  per_b((KV_HEADS, T, HEAD_DIM)),
                  per_b((KV_HEADS, nq, HEAD_DIM, Q_BLOCK)),
                  per_b((KV_HEADS, T, HEAD_DIM)),
                  _const_spec((ns, nc))],
        out_specs=pl.BlockSpec((1, Q_BLOCK, GW), lambda b, i: (b, i, 0)),
        out_shape=jax.ShapeDtypeStruct((B, T, GW), F32),
        compiler_params=_cparams(("parallel", "arbitrary")),
        name="nsa_prompt",
    )(z, z, ckv, ksT, vs, kwT, vw, mt)


_PPS = 8
_SEL_ROWS = 384


def _nsa_sample_a_kernel(pt_ref, *refs, n_pages, past_len, t_pad):
    pages = refs[:_PPS]
    w_ref, pe_ref, q_ref, mt_ref = refs[_PPS:_PPS + 4]
    ocmp_ref, selm_ref = refs[_PPS + 4:_PPS + 6]
    x_ref = refs[_PPS + 6]
    s = pl.program_id(1)
    cpp = PAGE // CMP_STRIDE
    for k in range(_PPS):
        x_ref[:, pl.ds(pl.multiple_of((s * _PPS + k) * cpp, cpp), cpp), :] = pages[k][0]

    @pl.when(s == pl.num_programs(1) - 1)
    def _():
        nc = n_pages * cpp
        n_blocks = past_len // SLC_BLOCK + 1
        lanes = 128
        ps_rows = []
        for g in range(KV_HEADS):
            ck = _compress_tail(_dot3(x_ref[g], w_ref[0]), _dot3(pe_ref[0], w_ref[0]))
            cv = _compress_tail(_dot3(x_ref[KV_HEADS + g], w_ref[1]), _dot3(pe_ref[1], w_ref[1]))
            Q = jnp.concatenate([q_ref[0, :, (g * Q_PER_KV + m) * HEAD_DIM:(g * Q_PER_KV + m + 1) * HEAD_DIM]
                                 for m in range(Q_PER_KV)], axis=0) * (HEAD_DIM ** -0.5)
            t_rows = past_len + (lax.broadcasted_iota(jnp.int32, (Q_PER_KV * t_pad, 1), 0) & (t_pad - 1))
            sc = _dot3(Q, ck, _NT)
            n_lane = lax.broadcasted_iota(jnp.int32, (1, nc), 1)
            mask_c = jnp.where(n_lane * CMP_STRIDE + (2 * CMP_STRIDE - 1) <= t_rows, 1.0, 0.0)
            pc = _masked_softmax(sc, mask_c)
            ocmp_ref[0, g] = _dot(pc.astype(BF16), cv.astype(BF16))
            ps_rows.append(pc[0:t_pad] + pc[t_pad:2 * t_pad] + pc[2 * t_pad:3 * t_pad] + pc[3 * t_pad:4 * t_pad])
        ps = jnp.concatenate(ps_rows + [jnp.zeros((lanes - KV_HEADS * t_pad, nc), F32)], axis=0)
        selT = _dot_exact_lhs(mt_ref[...], ps, _NT)
        j_col = lax.broadcasted_iota(jnp.int32, (_SEL_ROWS, 1), 0)
        t_lane = past_len + (lax.broadcasted_iota(jnp.int32, (1, lanes), 1) & (t_pad - 1))
        cur = t_lane >> 6
        forced = jnp.where(j_col == 0, 1.0, jnp.where(j_col == cur, 1.0, jnp.where(j_col == cur - 1, 1.0, 0.0)))
        selT = jnp.where(forced > 0.5, FORCED, jnp.where(j_col * SLC_BLOCK <= t_lane, selT, -FORCED))
        selT = jnp.where(j_col < n_blocks, selT, -3.0 * FORCED)
        selmT = _top_select(selT, j_col.astype(F32))
        r = lax.broadcasted_iota(jnp.int32, (lanes, lanes), 0)
        c = lax.broadcasted_iota(jnp.int32, (lanes, lanes), 1)
        eye = jnp.where(r == c, 1.0, 0.0).astype(BF16)
        selm = _dot(eye, selmT.astype(BF16), _NT)
        selm_ref[0] = selm[0:KV_HEADS * t_pad]


def _nsa_sample_a(page_table, cmp_pages, layer, wcat, pe2, z, past_len):
    B, t_pad, _ = z.shape
    n_pages = page_table.shape[1]
    nc = n_pages * (PAGE // CMP_STRIDE)
    mt = _sel_matrix(_SEL_ROWS, nc)
    page_spec = lambda k: pl.BlockSpec((None, 1, 2 * KV_HEADS, PAGE // CMP_STRIDE, 1024),
                                       lambda b, s, pt: (layer, pt[b, s * _PPS + k], 0, 0, 0))
    cst = lambda shape: pl.BlockSpec(shape, lambda b, s, pt: (0,) * len(shape))
    grid_spec = pltpu.PrefetchScalarGridSpec(
        num_scalar_prefetch=1,
        grid=(B, n_pages // _PPS),
        in_specs=[page_spec(k) for k in range(_PPS)] + [
            cst((2, 1024, 128)), cst((2, 8, 1024)),
            pl.BlockSpec((1, t_pad, GW), lambda b, s, pt: (b, 0, C_Q // GW)),
            cst((_SEL_ROWS, nc))],
        out_specs=[pl.BlockSpec((1, KV_HEADS, Q_PER_KV * t_pad, HEAD_DIM), lambda b, s, pt: (b, 0, 0, 0)),
                   pl.BlockSpec((1, KV_HEADS * t_pad, _SEL_ROWS), lambda b, s, pt: (b, 0, 0))],
        scratch_shapes=[pltpu.VMEM((2 * KV_HEADS, nc, 1024), F32)])
    return pl.pallas_call(
        functools.partial(_nsa_sample_a_kernel, n_pages=n_pages, past_len=past_len, t_pad=t_pad),
        grid_spec=grid_spec,
        out_shape=[jax.ShapeDtypeStruct((B, KV_HEADS, Q_PER_KV * t_pad, HEAD_DIM), F32),
                   jax.ShapeDtypeStruct((B, KV_HEADS * t_pad, _SEL_ROWS), F32)],
        compiler_params=_cparams(("parallel", "arbitrary")),
        name="nsa_sample_select",
    )(page_table, *([cmp_pages] * _PPS), wcat, pe2, z, mt)


def _nsa_sample_b_kernel(pt_ref, *refs, n_pages, past_len, t_pad, t_real):
    kpages = refs[:_PPS]
    vpages = refs[_PPS:2 * _PPS]
    q_ref, rows_ref, kwvw_ref, gt_ref, selm_ref, ocmp_ref, cw_ref = refs[2 * _PPS:2 * _PPS + 7]
    o_ref = refs[2 * _PPS + 7]
    m_ref, l_ref, acc_ref = refs[2 * _PPS + 8:]
    s = pl.program_id(1)
    n_steps = n_pages // _PPS
    rows = Q_PER_KV * t_pad
    t_rows = past_len + (lax.broadcasted_iota(jnp.int32, (rows, 1), 0) & (t_pad - 1))

    @pl.when(s == 0)
    def _():
        m_ref[...] = jnp.full(m_ref.shape, NEG, F32)
        l_ref[...] = jnp.zeros(l_ref.shape, F32)
        acc_ref[...] = jnp.zeros(acc_ref.shape, F32)

    def q_rows(g):
        Q = jnp.concatenate([q_ref[0, :, (g * Q_PER_KV + m) * HEAD_DIM:(g * Q_PER_KV + m + 1) * HEAD_DIM]
                             for m in range(Q_PER_KV)], axis=0) * (HEAD_DIM ** -0.5)
        return Q.astype(BF16)

    def online(g, sc, ok, values):
        sc = jnp.where(ok > 0.5, sc, NEG)
        m_run = m_ref[g]
        m_new = jnp.maximum(m_run, jnp.max(sc, axis=-1, keepdims=True))
        p = jnp.exp(sc - m_new) * ok
        corr = jnp.exp(m_run - m_new)
        l_ref[g] = corr * l_ref[g] + jnp.sum(p, axis=-1, keepdims=True)
        acc_ref[g] = corr * acc_ref[g] + values(p.astype(BF16))
        m_ref[g] = m_new

    @pl.when(s < n_steps)
    def _():
        keys = _PPS * PAGE
        jr = lax.broadcasted_iota(jnp.int32, (_SEL_ROWS, keys), 0)
        cl = lax.broadcasted_iota(jnp.int32, (_SEL_ROWS, keys), 1)
        expand = jnp.where(jr == s * (keys // SLC_BLOCK) + (cl >> 6), 1.0, 0.0).astype(BF16)
        mv = _dot(selm_ref[0].astype(BF16), expand)
        pos = s * keys + lax.broadcasted_iota(jnp.int32, (1, keys), 1)
        for g in range(KV_HEADS):
            Qb = q_rows(g)
            sc = jnp.concatenate([_dot(Qb, kpages[k][g].astype(BF16)) for k in range(_PPS)], axis=1)
            mvg = jnp.concatenate([mv[g * t_pad:(g + 1) * t_pad]] * Q_PER_KV, axis=0)
            ok = jnp.where(pos <= t_rows, mvg, 0.0)

            def values(pb, g=g):
                out = _dot(pb[:, 0:PAGE], vpages[0][g].astype(BF16), _NT)
                for k in range(1, _PPS):
                    out = out + _dot(pb[:, k * PAGE:(k + 1) * PAGE], vpages[k][g].astype(BF16), _NT)
                return out

            online(g, sc, ok, values)

    @pl.when(s == n_steps)
    def _():
        gts = _sigmoid(gt_ref[0])
        last_blk = past_len // SLC_BLOCK
        r_lane = lax.broadcasted_iota(jnp.int32, (1, t_pad), 1)
        pieces = []
        for g in range(KV_HEADS):
            Qb = q_rows(g)
            k_new = rows_ref[0, :, (2 * KV_HEADS + g) * HEAD_DIM:(2 * KV_HEADS + g + 1) * HEAD_DIM]
            v_new = rows_ref[0, :, (3 * KV_HEADS + g) * HEAD_DIM:(3 * KV_HEADS + g + 1) * HEAD_DIM]
            sc = _dot(Qb, k_new.astype(BF16), _NT)
            sel_last = jnp.concatenate([selm_ref[0, g * t_pad:(g + 1) * t_pad, last_blk:last_blk + 1]] * Q_PER_KV, axis=0)
            vis = jnp.where(past_len + r_lane <= t_rows, jnp.where(r_lane < t_real, 1.0, 0.0), 0.0)
            online(g, sc, vis * sel_last, lambda pb, v_new=v_new: _dot(pb, v_new.astype(BF16)))
            o_slc = acc_ref[g] / jnp.maximum(l_ref[g], 1e-30)
            wrows = cw_ref.shape[3]
            kw_new = kwvw_ref[0, :, g * HEAD_DIM:(g + 1) * HEAD_DIM]
            vw_new = kwvw_ref[0, :, (KV_HEADS + g) * HEAD_DIM:(KV_HEADS + g + 1) * HEAD_DIM]
            sw = jnp.concatenate([_dot(Qb, cw_ref[0, g].astype(BF16)),
                                  _dot(Qb, kw_new.astype(BF16), _NT)], axis=1)
            idx = lax.broadcasted_iota(jnp.int32, (1, wrows + t_pad), 1)
            spos = past_len - wrows + idx
            dt = t_rows - spos
            real = jnp.where(idx < wrows + t_real, 1.0, 0.0)
            mask_w = jnp.where(dt >= 0, jnp.where(dt < WINDOW, real, 0.0), 0.0)
            pw = _masked_softmax(sw, mask_w).astype(BF16)
            o_win = _dot(pw[:, 0:wrows], cw_ref[1, g].astype(BF16), _NT) + _dot(pw[:, wrows:], vw_new.astype(BF16))
            pieces += _gated_heads(gts, g, ocmp_ref[0, g], o_slc, o_win, t_pad)
        o_ref[0] = jnp.concatenate(pieces, axis=1)


def _nsa_sample_b(page_table, cache_t, win_t, layer, z, selm, ocmp, past_len, t_real):
    B, t_pad, _ = z.shape
    n_pages = page_table.shape[1]
    n_steps = n_pages // _PPS
    rows = Q_PER_KV * t_pad
    wrows = win_t.shape[5]
    pidx = lambda b, s, pt, k: pt[b, jnp.minimum(s, n_steps - 1) * _PPS + k]
    page = lambda which, k: pl.BlockSpec((None, None, None, KV_HEADS, HEAD_DIM, PAGE),
                                         lambda b, s, pt: (layer, pidx(b, s, pt, k), which, 0, 0, 0))
    kspec = lambda k: page(2, k)
    vspec = lambda k: page(3, k)
    zspec = lambda col, w: pl.BlockSpec((1, t_pad, w), lambda b, s, pt: (b, 0, col // w))
    per_b = lambda shape: pl.BlockSpec((1,) + shape, lambda b, s, pt: (b,) + (0,) * len(shape))
    grid_spec = pltpu.PrefetchScalarGridSpec(
        num_scalar_prefetch=1,
        grid=(B, n_steps + 1),
        in_specs=[kspec(k) for k in range(_PPS)] + [vspec(k) for k in range(_PPS)] + [
            zspec(C_Q, GW), zspec(C_ROWS, GW), zspec(C_KW, 256), zspec(C_GATES, 256),
            per_b((KV_HEADS * t_pad, _SEL_ROWS)), per_b((KV_HEADS, rows, HEAD_DIM)),
            pl.BlockSpec((None, None, 2, KV_HEADS, HEAD_DIM, wrows), lambda b, s, pt: (layer, b, 0, 0, 0, 0))],
        out_specs=pl.BlockSpec((1, t_pad, GW), lambda b, s, pt: (b, 0, 0)),
        scratch_shapes=[pltpu.VMEM((KV_HEADS, rows, 1), F32), pltpu.VMEM((KV_HEADS, rows, 1), F32),
                        pltpu.VMEM((KV_HEADS, rows, HEAD_DIM), F32)])
    return pl.pallas_call(
        functools.partial(_nsa_sample_b_kernel, n_pages=n_pages, past_len=past_len, t_pad=t_pad, t_real=t_real),
        grid_spec=grid_spec,
        out_shape=jax.ShapeDtypeStruct((B, t_pad, GW), F32),
        compiler_params=_cparams(("parallel", "arbitrary")),
        name="nsa_sample_attend",
    )(page_table, *([cache_t] * (2 * _PPS)), z, z, z, z, selm, ocmp, win_t)


def _proj_out_kernel(x_ref, ya_ref, yb_ref, yc_ref, yd_ref, g_ref, w_ref, b_ref, lg_ref, lb_ref, h_ref, *, alpha):
    yd = yd_ref[...] * g_ref[...]
    acc = _dot(ya_ref[...].astype(BF16), w_ref[0:GW, :])
    acc = acc + _dot(yb_ref[...].astype(BF16), w_ref[GW:2 * GW, :])
    acc = acc + _dot(yc_ref[...].astype(BF16), w_ref[2 * GW:3 * GW, :])
    acc = acc + _dot(yd.astype(BF16), w_ref[3 * GW:4 * GW, :])
    h_ref[...] = _layer_norm(alpha * x_ref[...] + acc + b_ref[...], lg_ref[...], lb_ref[...], LN_EPS)


def _proj_out(x2d, ya, yb, yc, yd, g, w_bf16, b, lg, lb, alpha):
    n = x2d.shape[0]
    tm = min(256, n)
    row = lambda w: pl.BlockSpec((tm, w), lambda i: (i, 0))
    return pl.pallas_call(
        functools.partial(_proj_out_kernel, alpha=alpha),
        grid=(n // tm,),
        in_specs=[row(D_MODEL), row(GW), row(GW), row(GW), row(GW), row(GW),
                  _const_spec((D_MODEL, D_MODEL)), _const_spec((1, D_MODEL)), _const_spec((1, D_MODEL)),
                  _const_spec((1, D_MODEL))],
        out_specs=row(D_MODEL),
        out_shape=jax.ShapeDtypeStruct((n, D_MODEL), F32),
        compiler_params=_cparams(("parallel",)),
        name="proj_out_ln1",
    )(x2d, ya, yb, yc, yd, g, w_bf16, b, lg, lb)


def _router_kernel(h_ref, w_ref, b_ref, o_ref):
    logits = _dot3(h_ref[...], w_ref[...]) + b_ref[...]
    lane = lax.broadcasted_iota(jnp.int32, logits.shape, 1).astype(F32)
    big = 1e6
    is_g = lane < N_GROUPS
    gl = jnp.where(is_g, logits, NEG)
    gmax = jnp.max(gl, axis=-1, keepdims=True)
    gsel = jnp.min(jnp.where(is_g, jnp.where(gl == gmax, lane, big), big), axis=-1, keepdims=True)
    gp = 1.0 / jnp.sum(jnp.where(is_g, jnp.exp(gl - gmax), 0.0), axis=-1, keepdims=True)
    lo = N_GROUPS + gsel * EXP_PER_GROUP
    in_grp = jnp.where(lane >= lo, jnp.where(lane < lo + EXP_PER_GROUP, 1.0, 0.0), 0.0)
    el = jnp.where(in_grp > 0.5, logits, NEG)
    e = jnp.exp(el - jnp.max(el, axis=-1, keepdims=True)) * in_grp
    p = e / jnp.sum(e, axis=-1, keepdims=True)
    pm = jnp.where(in_grp > 0.5, p, -1.0)
    v1 = jnp.max(pm, axis=-1, keepdims=True)
    i1 = jnp.min(jnp.where(pm == v1, lane, big), axis=-1, keepdims=True)
    pm2 = jnp.where(lane == i1, -1.0, pm)
    v2 = jnp.max(pm2, axis=-1, keepdims=True)
    i2 = jnp.min(jnp.where(pm2 == v2, lane, big), axis=-1, keepdims=True)
    tot = v1 + v2
    out = jnp.where(lane == 0, i1 - N_GROUPS,
                    jnp.where(lane == 1, i2 - N_GROUPS,
                              jnp.where(lane == 2, v1 / tot * gp, jnp.where(lane == 3, v2 / tot * gp, 0.0))))
    o_ref[...] = out


def _router(h, w_cat, b_cat):
    n = h.shape[0]
    tm = min(256, n)
    return pl.pallas_call(
        _router_kernel,
        grid=(n // tm,),
        in_specs=[pl.BlockSpec((tm, D_MODEL), lambda i: (i, 0)), _const_spec((D_MODEL, 128)), _const_spec((1, 128))],
        out_specs=pl.BlockSpec((tm, 128), lambda i: (i, 0)),
        out_shape=jax.ShapeDtypeStruct((n, 128), F32),
        compiler_params=_cparams(("parallel",)),
        name="moe_router",
    )(h, w_cat, b_cat)


def _issue_rows(idx_fn, n_rows, src_ref, dst_ref, sem):
    def body(r, c):
        pltpu.make_async_copy(src_ref.at[pl.ds(idx_fn(r), 1), :], dst_ref.at[pl.ds(r, 1), :], sem).start()
        return c

    lax.fori_loop(0, n_rows, body, 0, unroll=8)


def _wait_rows(n_rows, src_ref, dst_ref, sem):
    def body(r, c):
        pltpu.make_async_copy(src_ref.at[pl.ds(0, 1), :], dst_ref.at[pl.ds(r, 1), :], sem).wait()
        return c

    lax.fori_loop(0, n_rows, body, 0, unroll=8)


def _moe_kernel(te_ref, nu_ref, src_ref, cnt_ref, first_ref, nxt_ref, wslot_ref, tok_ref,
                h_ref, wg_ref, wu_ref, wd_ref, o_ref,
                xbuf_ref, sems, wgs_ref, wus_ref, wds_ref, wsems, wgb_ref, wub_ref, wdb_ref, *, layer):
    t = pl.program_id(0)
    tm = o_ref.shape[0]
    slot = t % 2
    n_used = nu_ref[0]
    last = tok_ref.shape[0] - 1

    def fetch(tile, to_slot):
        src, cnt = src_ref[tile], cnt_ref[tile]
        idx = lambda r: jnp.where(r < cnt, tok_ref[jnp.minimum(src + r, last)], 0)
        _issue_rows(idx, tm, h_ref, xbuf_ref.at[to_slot], sems.at[to_slot])

    def weight_copies(e, s):
        return (pltpu.make_async_copy(wg_ref.at[layer, e], wgs_ref.at[s], wsems.at[s, 0]),
                pltpu.make_async_copy(wu_ref.at[layer, e], wus_ref.at[s], wsems.at[s, 1]),
                pltpu.make_async_copy(wd_ref.at[layer, e], wds_ref.at[s], wsems.at[s, 2]))

    @pl.when(t == 0)
    def _():
        for c in weight_copies(te_ref[0], 0):
            c.start()
        fetch(0, 0)

    @pl.when(t + 1 < n_used)
    def _():
        fetch(t + 1, 1 - slot)

    @pl.when(first_ref[t] == 1)
    def _():
        s = wslot_ref[t]
        for c in weight_copies(te_ref[t], s):
            c.wait()

        @pl.when(nxt_ref[t] >= 0)
        def _():
            for c in weight_copies(nxt_ref[t], 1 - s):
                c.start()

        wgb_ref[...] = wgs_ref[s].astype(BF16)
        wub_ref[...] = wus_ref[s].astype(BF16)
        wdb_ref[...] = wds_ref[s].astype(BF16)

    @pl.when(jnp.logical_or(t == 0, t < n_used))
    def _():
        _wait_rows(tm, h_ref, xbuf_ref.at[slot], sems.at[slot])
        xb = xbuf_ref[slot].astype(BF16)
        hg = _silu(_dot(xb, wgb_ref[...])) * _dot(xb, wub_ref[...])
        o_ref[...] = _dot(hg.astype(BF16), wdb_ref[...])

    @pl.when(jnp.logical_and(t > 0, t >= n_used))
    def _():
        o_ref[...] = jnp.zeros(o_ref.shape, F32)


def _moe_experts(sched, h, w_gate, w_up, w_down, layer):
    tm = MOE_TILE
    r = sched[0].shape[0] * tm
    hbm = pl.BlockSpec(memory_space=pl.ANY)
    grid_spec = pltpu.PrefetchScalarGridSpec(
        num_scalar_prefetch=len(sched),
        grid=(r // tm,),
        in_specs=[hbm, hbm, hbm, hbm],
        out_specs=pl.BlockSpec((tm, D_MODEL), lambda t, *_: (t, 0)),
        scratch_shapes=[pltpu.VMEM((2, tm, D_MODEL), F32), pltpu.SemaphoreType.DMA((2,)),
                        pltpu.VMEM((2, D_MODEL, D_EXPERT), F32), pltpu.VMEM((2, D_MODEL, D_EXPERT), F32),
                        pltpu.VMEM((2, D_EXPERT, D_MODEL), F32), pltpu.SemaphoreType.DMA((2, 3)),
                        pltpu.VMEM((D_MODEL, D_EXPERT), BF16), pltpu.VMEM((D_MODEL, D_EXPERT), BF16),
                        pltpu.VMEM((D_EXPERT, D_MODEL), BF16)])
    return pl.pallas_call(
        functools.partial(_moe_kernel, layer=layer),
        grid_spec=grid_spec,
        out_shape=jax.ShapeDtypeStruct((r, D_MODEL), F32),
        compiler_params=_cparams(("arbitrary",), disable_bounds_checks=True),
        name="moe_experts",
    )(*sched, h, w_gate, w_up, w_down)


def _moe_schedule(eid):
    n = eid.shape[0]
    tm = MOE_TILE
    n_tiles = (2 * n + N_EXPERTS * (tm - 1)) // tm + 1
    flat_e = eid.reshape(-1)
    order = jnp.argsort(flat_e, stable=True).astype(jnp.int32)
    rank = jnp.argsort(order).astype(jnp.int32)
    onehot = (flat_e[:, None] == jnp.arange(N_EXPERTS, dtype=jnp.int32)[None, :]).astype(jnp.int32)
    counts = jnp.sum(onehot, axis=0)
    padded = (counts + tm - 1) // tm * tm
    pad_end = jnp.cumsum(padded)
    g_start = pad_end - padded
    u_start = jnp.cumsum(counts) - counts
    pos = (rank + jnp.sum(onehot * (g_start - u_start)[None, :], axis=1)).reshape(n, 2)
    n_used = (pad_end[-1] // tm).astype(jnp.int32)
    tile_id = jnp.arange(n_tiles, dtype=jnp.int32)
    tile_expert = jnp.sum((pad_end[None, :] <= (tile_id * tm)[:, None]).astype(jnp.int32), axis=1)
    tile_expert = jnp.minimum(tile_expert, N_EXPERTS - 1)
    in_tile = (tile_expert[:, None] == jnp.arange(N_EXPERTS, dtype=jnp.int32)[None, :]).astype(jnp.int32)
    pick = lambda table: jnp.sum(in_tile * table[None, :], axis=1)
    off = tile_id * tm - pick(g_start)
    tile_src = (pick(u_start) + off).astype(jnp.int32)
    tile_cnt = jnp.clip(pick(counts) - off, 0, tm).astype(jnp.int32)
    live = tile_id < n_used
    tile_cnt = jnp.where(live, tile_cnt, 0)
    ar = jnp.arange(N_EXPERTS, dtype=jnp.int32)
    used = counts > 0
    nxt_e = jnp.min(jnp.where(jnp.logical_and(ar[None, :] > ar[:, None], used[None, :]), ar[None, :], N_EXPERTS), axis=1)
    nxt_e = jnp.where(nxt_e < N_EXPERTS, nxt_e, -1)
    first = jnp.logical_and(live, off == 0).astype(jnp.int32)
    wslot = (pick(jnp.cumsum(used.astype(jnp.int32)) - 1) % 2).astype(jnp.int32)
    sched = (tile_expert.astype(jnp.int32), n_used.reshape(1), tile_src, tile_cnt, first,
             pick(nxt_e).astype(jnp.int32), wslot, order // 2)
    return sched, pos


def _ln2_kernel(idx_ref, h_ref, rt_ref, ys_ref, g_ref, b_ref, o_ref, ybuf_ref, sems, *, alpha):
    i = pl.program_id(0)
    tm = h_ref.shape[0]
    slot = i % 2

    def fetch(tile, to_slot):
        base = tile * 2 * tm
        _issue_rows(lambda r: idx_ref[base + r], 2 * tm, ys_ref, ybuf_ref.at[to_slot], sems.at[to_slot])

    @pl.when(i == 0)
    def _():
        fetch(0, 0)

    @pl.when(i + 1 < pl.num_programs(0))
    def _():
        fetch(i + 1, 1 - slot)

    _wait_rows(2 * tm, ys_ref, ybuf_ref.at[slot], sems.at[slot])
    w0 = rt_ref[:, 2:3]
    w1 = rt_ref[:, 3:4]
    moe = w0 * ybuf_ref[slot, 0:tm, :] + w1 * ybuf_ref[slot, tm:2 * tm, :]
    o_ref[...] = _layer_norm(alpha * h_ref[...] + moe, g_ref[...], b_ref[...], LN_EPS)


def _combine_ln2(h, routed, pos, ys, g, b, alpha):
    n = h.shape[0]
    tm = min(128, n)
    nt = n // tm
    idx = jnp.transpose(pos.reshape(nt, tm, 2), (0, 2, 1)).reshape(-1)
    grid_spec = pltpu.PrefetchScalarGridSpec(
        num_scalar_prefetch=1,
        grid=(nt,),
        in_specs=[pl.BlockSpec((tm, D_MODEL), lambda i, idx_ref: (i, 0)),
                  pl.BlockSpec((tm, 128), lambda i, idx_ref: (i, 0)),
                  pl.BlockSpec(memory_space=pl.ANY),
                  pl.BlockSpec((1, D_MODEL), lambda i, idx_ref: (0, 0)),
                  pl.BlockSpec((1, D_MODEL), lambda i, idx_ref: (0, 0))],
        out_specs=pl.BlockSpec((tm, D_MODEL), lambda i, idx_ref: (i, 0)),
        scratch_shapes=[pltpu.VMEM((2, 2 * tm, D_MODEL), F32), pltpu.SemaphoreType.DMA((2,))])
    return pl.pallas_call(
        functools.partial(_ln2_kernel, alpha=alpha),
        grid_spec=grid_spec,
        out_shape=jax.ShapeDtypeStruct((n, D_MODEL), F32),
        compiler_params=_cparams(("arbitrary",), disable_bounds_checks=True),
        name="combine_ln2",
    )(idx, h, routed, ys, g, b)


def _relayout_cols(w):
    pad = lambda n: jnp.zeros(w.shape[:-1] + (n,), w.dtype)
    d0 = 3864
    return jnp.concatenate([
        w[..., 0:2328], pad(C_BG - 2328),
        w[..., 2328:3864],
        w[..., d0:d0 + 1536],
        w[..., d0 + 1536:d0 + 1568], pad(96),
        w[..., d0 + 1568:d0 + 1600], pad(96),
        w[..., d0 + 1600:d0 + 1696], pad(N_Z - C_GL - 96)], axis=-1)


def _relayout_d(v):
    pad = lambda n: jnp.zeros(v.shape[:-1] + (n,), v.dtype)
    return jnp.concatenate([v[..., 0:1536], v[..., 1536:1568], pad(96), v[..., 1568:1600], pad(96),
                            v[..., 1600:1696], pad(32)], axis=-1)


def _shift_cols(zrow):
    return jnp.concatenate([zrow[..., C_R:C_R + 1536], zrow[..., C_WL:C_WL + 32], zrow[..., C_AL:C_AL + 32],
                            zrow[..., C_GL:C_GL + 96]], axis=-1)


def _pad_rows(w, rows):
    return jnp.concatenate([w, jnp.zeros((rows - w.shape[0],) + w.shape[1:], w.dtype)], axis=0)


def _layer_params(l, P):
    wc = P["nsa_wc"][l]
    half = 2 * CMP_STRIDE // 2
    wcat = jnp.concatenate([wc[:, :half].reshape(2, half * HEAD_DIM, HEAD_DIM),
                            wc[:, half:].reshape(2, half * HEAD_DIM, HEAD_DIM)], axis=-1)
    pe = P["nsa_pe"][l]
    pe2 = jnp.stack([pe[:, :half].reshape(2, half * HEAD_DIM), pe[:, half:].reshape(2, half * HEAD_DIM)], axis=1)
    pe2 = jnp.concatenate([pe2, jnp.zeros((2, 6, half * HEAD_DIM), F32)], axis=1)
    router_w = jnp.concatenate([P["moe_w_grp"][l], P["moe_w_rte"][l],
                                jnp.zeros((D_MODEL, 128 - N_GROUPS - N_EXPERTS), F32)], axis=1)
    router_b = jnp.concatenate([P["moe_b_grp"][l], P["moe_b_rte"][l],
                                jnp.zeros((128 - N_GROUPS - N_EXPERTS,), F32)])[None, :]
    row = lambda name: P[name][l][None, :]
    return dict(
        w_in=_relayout_cols(P["w_in"][l]).astype(BF16), b_in=_relayout_cols(P["b_in"][l])[None, :],
        w_out=P["w_out"][l].astype(BF16), b_out=row("b_out"),
        ln1_g=row("ln1_g"), ln1_b=row("ln1_b"), ln2_g=row("ln2_g"), ln2_b=row("ln2_b"),
        conv_a_w=P["conv_a_w"][l], conv_a_b=row("conv_a_b"), ln_a_g=row("ln_a_g"), ln_a_b=row("ln_a_b"),
        conv_c_w=P["conv_c_w"][l], wcat=wcat, pe2=pe2,
        rw_mu=_relayout_d(P["rw_mu"][l])[None, :], rw_w0=row("rw_w0"), rw_w2=_pad_rows(P["rw_w2"][l], 128),
        rw_a0=row("rw_a0"), rw_a2=_pad_rows(P["rw_a2"][l], 128), rw_g2=_pad_rows(P["rw_g2"][l], 128),
        rw_kk=row("rw_kk"), rw_ka=row("rw_ka"), rw_rk=P["rw_rk"][l].reshape(1, GW),
        rw_lnx_g=P["rw_lnx_g"][l].reshape(HEADS, HEAD_DIM), rw_lnx_b=P["rw_lnx_b"][l].reshape(HEADS, HEAD_DIM),
        router_w=router_w, router_b=router_b)


def _mix_rwkv(z, shift, s0, lp, t_real):
    B, T, _ = z.shape
    r, w, k, v, kk, b, bonus, g = _rwkv_prep(z, shift, lp["rw_mu"], lp["rw_w0"], lp["rw_w2"], lp["rw_a0"],
                                             lp["rw_a2"], lp["rw_g2"], lp["rw_kk"], lp["rw_ka"], lp["rw_rk"], t_real)
    if t_real == T and T % _RW_CHUNK == 0:
        y, s_new = _rwkv_chunked(r, w, k, v, kk, b, bonus, s0, lp["rw_lnx_g"], lp["rw_lnx_b"])
    else:
        y, s_new = _rwkv_scan(r, w, k, v, kk, b, bonus, s0, lp["rw_lnx_g"], lp["rw_lnx_b"], t_real)
    yd = jnp.transpose(y, (0, 2, 1, 3)).reshape(B, T, GW)
    return yd, g, s_new


def _nsa_prompt_group(z, lp):
    B, T, _ = z.shape
    kv6 = z[:, :, C_ROWS:C_GATES].reshape(B, T, 6, KV_HEADS, HEAD_DIM)
    nc = T // CMP_STRIDE
    cmp_in = kv6[:, :, 0:2].reshape(B, nc, CMP_STRIDE, 2, KV_HEADS, HEAD_DIM)
    cmp_in = jnp.transpose(cmp_in, (0, 3, 4, 1, 2, 5)).reshape(B, 2, KV_HEADS, nc, CMP_STRIDE * HEAD_DIM)
    ckv = _compress_prompt(cmp_in, lp["wcat"], lp["pe2"])
    keyT = lambda x, tile: jnp.transpose(x.astype(BF16).reshape(B, T // tile, tile, KV_HEADS, HEAD_DIM), (0, 3, 1, 4, 2))
    val = lambda x: jnp.transpose(x.astype(BF16), (0, 2, 1, 3))
    yb = _nsa_prompt(z, ckv, keyT(kv6[:, :, 2], _KT), val(kv6[:, :, 3]), keyT(kv6[:, :, 4], Q_BLOCK), val(kv6[:, :, 5]))
    rows = kv6[:, :, 0:4]
    win = kv6[:, T - min(WINDOW, T):, 4:6]
    return yb, rows, win


def _cache_views(cache_nsa, cache_win):
    L, n_pool = cache_nsa.shape[0], cache_nsa.shape[1]
    cpp = PAGE // CMP_STRIDE
    cache_t = jnp.transpose(cache_nsa, (0, 1, 3, 4, 5, 2))
    win_t = jnp.transpose(cache_win, (0, 1, 3, 4, 5, 2))
    cmp_pages = cache_t[:, :, 0:2].reshape(L, n_pool, 2, KV_HEADS, HEAD_DIM, cpp, CMP_STRIDE)
    cmp_pages = jnp.transpose(cmp_pages, (0, 1, 2, 3, 5, 6, 4)).reshape(L, n_pool, 2 * KV_HEADS, cpp, CMP_STRIDE * HEAD_DIM)
    return cache_t, win_t, cmp_pages


def _nsa_sample_group(z, views, layer, page_table, win_l, lp, t_real):
    B, t_pad, _ = z.shape
    cache_t, win_t, cmp_pages = views
    past_len = page_table.shape[1] * PAGE
    ocmp, selm = _nsa_sample_a(page_table, cmp_pages, layer, lp["wcat"], lp["pe2"], z, past_len)
    yb = _nsa_sample_b(page_table, cache_t, win_t, layer, z, selm, ocmp, past_len, t_real)
    zr = z[:, :t_real]
    rows = zr[:, :, C_ROWS:C_KW].reshape(B, t_real, 4, KV_HEADS, HEAD_DIM)
    new_win = zr[:, :, C_KW:C_GATES].reshape(B, t_real, 2, KV_HEADS, HEAD_DIM)
    win = jnp.concatenate([win_l, new_win], axis=1)[:, t_real:]
    return yb, rows, win


def _group_layer(x, lp, alpha, nsa_fn, conv_a_buf, conv_c_buf, rw_state, rw_shift, t_real):
    B, T, _ = x.shape
    z = _proj_in(x.reshape(B * T, D_MODEL), lp["w_in"], lp["b_in"]).reshape(B, T, N_Z)
    ya, yc, conv_a_new, conv_c_new = _conv_mix(z, conv_a_buf, conv_c_buf, lp["conv_a_w"], lp["conv_a_b"],
                                               lp["ln_a_g"], lp["ln_a_b"], lp["conv_c_w"], t_real)
    yb, nsa_rows, win_new = nsa_fn(z)
    yd, g, rw_state_new = _mix_rwkv(z, _relayout_d(rw_shift)[:, None, :], rw_state, lp, t_real)
    flat = lambda a: a.reshape(B * T, a.shape[-1])
    h = _proj_out(flat(x), flat(ya), flat(yb), flat(yc), flat(yd), flat(g), lp["w_out"], lp["b_out"],
                  lp["ln1_g"], lp["ln1_b"], alpha)
    shift_new = _shift_cols(z[:, t_real - 1])
    return h, (nsa_rows, win_new, conv_a_new, conv_c_new, rw_state_new, shift_new)


def _moe_ln2(h_list, lp, P, layer, alpha):
    routed = [_router(h, lp["router_w"], lp["router_b"]) for h in h_list]
    eid = jnp.concatenate([r[:, 0:2] for r in routed], axis=0).astype(jnp.int32)
    h_all = jnp.concatenate(h_list, axis=0)
    sched, pos = _moe_schedule(eid)
    ys = _moe_experts(sched, h_all, P["moe_w_gate"], P["moe_w_up"], P["moe_w_down"], layer)
    outs = []
    off = 0
    for h, rt in zip(h_list, routed):
        n = h.shape[0]
        outs.append(_combine_ln2(h, rt, pos[off:off + n], ys, lp["ln2_g"], lp["ln2_b"], alpha))
        off += n
    return outs


def kernel(x_prompt, x_sample, cache_nsa, cache_win, state_conv_a, state_conv_c, state_rwkv, state_rwkv_shift, page_table, w_in, b_in, w_out, b_out, ln1_g, ln1_b, ln2_g, ln2_b, conv_a_w, conv_a_b, ln_a_g, ln_a_b, conv_c_w, nsa_wc, nsa_pe, rw_mu, rw_w0, rw_w2, rw_a0, rw_a2, rw_g2, rw_kk, rw_ka, rw_rk, rw_lnx_g, rw_lnx_b, moe_w_grp, moe_b_grp, moe_w_rte, moe_b_rte, moe_w_gate, moe_w_up, moe_w_down):
    P = dict(w_in=w_in, b_in=b_in, w_out=w_out, b_out=b_out, ln1_g=ln1_g, ln1_b=ln1_b, ln2_g=ln2_g, ln2_b=ln2_b,
             conv_a_w=conv_a_w, conv_a_b=conv_a_b, ln_a_g=ln_a_g, ln_a_b=ln_a_b, conv_c_w=conv_c_w,
             nsa_wc=nsa_wc, nsa_pe=nsa_pe, rw_mu=rw_mu, rw_w0=rw_w0, rw_w2=rw_w2, rw_a0=rw_a0, rw_a2=rw_a2,
             rw_g2=rw_g2, rw_kk=rw_kk, rw_ka=rw_ka, rw_rk=rw_rk, rw_lnx_g=rw_lnx_g, rw_lnx_b=rw_lnx_b,
             moe_w_grp=moe_w_grp, moe_b_grp=moe_b_grp, moe_w_rte=moe_w_rte, moe_b_rte=moe_b_rte,
             moe_w_gate=moe_w_gate, moe_w_up=moe_w_up, moe_w_down=moe_w_down)
    depth = w_in.shape[0]
    alpha = (2.0 * depth) ** 0.25
    bp, tp, _ = x_prompt.shape
    bs, ts, _ = x_sample.shape
    ts_pad = 8
    xp = x_prompt
    xs = jnp.concatenate([x_sample, jnp.zeros((bs, ts_pad - ts, D_MODEL), F32)], axis=1)
    outs_p, outs_s = [], []
    views = _cache_views(cache_nsa, cache_win)
    for l in range(depth):
        lp = _layer_params(l, P)
        h_p, st_p = _group_layer(xp, lp, alpha, functools.partial(_nsa_prompt_group, lp=lp),
                                 jnp.zeros((bp, CONV_A_WIDTH - 1, GW), F32), jnp.zeros((bp, SHORT_CONV_WIDTH - 1, GW), F32),
                                 jnp.zeros((bp, HEADS, HEAD_DIM, HEAD_DIM), F32), jnp.zeros((bp, N_COLS_D), F32), tp)
        h_s, st_s = _group_layer(xs, lp, alpha,
                                 functools.partial(_nsa_sample_group, views=views, layer=l, page_table=page_table,
                                                   win_l=cache_win[l], lp=lp, t_real=ts),
                                 state_conv_a[l], state_conv_c[l], state_rwkv[l], state_rwkv_shift[l], ts)
        o_p, o_s = _moe_ln2([h_p, h_s], lp, P, l, alpha)
        xp = o_p.reshape(bp, tp, D_MODEL)
        xs = o_s.reshape(bs, ts_pad, D_MODEL)
        outs_p.append(st_p)
        outs_s.append(st_s)
    stk = lambda outs, i: jnp.stack([o[i] for o in outs], axis=0)
    return (xp, xs[:, :ts], stk(outs_p, 0), stk(outs_s, 0), stk(outs_p, 1), stk(outs_s, 1), stk(outs_p, 2), stk(outs_s, 2),
            stk(outs_p, 3), stk(outs_s, 3), stk(outs_p, 4), stk(outs_s, 4), stk(outs_p, 5), stk(outs_s, 5))
```

```python
import functools

import numpy as np
import jax
import jax.numpy as jnp
from jax import lax
from jax.experimental import pallas as pl
from jax.experimental.pallas import tpu as pltpu

F32 = jnp.float32
BF16 = jnp.bfloat16

D_MODEL = 2048
DEPTH = 2
HEAD_DIM = 64
GW = 512
HEADS = 8
CONV_A_WIDTH = 31
SHORT_CONV_WIDTH = 3
KV_HEADS = 2
Q_PER_KV = 4
CMP_STRIDE = 16
SLC_BLOCK = 64
N_SLC = 16
WINDOW = 512
Q_BLOCK = 128
PAGE = 128
RWKV_GN_EPS = 64e-5
LN_EPS = 1e-5
NEG = -1e30
FORCED = 1e4
N_EXPERTS = 32
EXP_PER_GROUP = 8
N_GROUPS = 4
D_EXPERT = 512
N_IN = 5560
N_COLS_D = 1696

N_Z = 6144
C_VAL, C_GATE = 0, 512
C_Q, C_ROWS, C_KW, C_GATES = 1024, 1536, 2048, 2304
C_BG, C_CG, C_H = 2560, 3072, 3584
C_R, C_K, C_V = 4096, 4608, 5120
C_WL, C_AL, C_GL = 5632, 5760, 5888

VMEM_LIMIT = 56 * 1024 * 1024
MOE_TILE = 256


def _cparams(sem, vmem=VMEM_LIMIT, disable_bounds_checks=False):
    return pltpu.CompilerParams(dimension_semantics=sem, vmem_limit_bytes=vmem,
                                disable_bounds_checks=disable_bounds_checks)


def _const_spec(shape):
    nd = len(shape)
    return pl.BlockSpec(shape, lambda *_: (0,) * nd)


def _split2(x):
    hi = x.astype(BF16)
    lo = (x - hi.astype(F32)).astype(BF16)
    return hi, lo


def _split3(x):
    hi = x.astype(BF16)
    r1 = x - hi.astype(F32)
    mid = r1.astype(BF16)
    lo = (r1 - mid.astype(F32)).astype(BF16)
    return hi, mid, lo


_NN = (((1,), (0,)), ((), ()))
_NT = (((1,), (1,)), ((), ()))


def _dot(a, b, dims=_NN):
    return lax.dot_general(a, b, dims, preferred_element_type=F32)


def _dot3(a, b, dims=_NN):
    ah, al = _split2(a)
    bh, bl = _split2(b)
    return _dot(ah, bh, dims) + _dot(al, bh, dims) + _dot(ah, bl, dims)


def _dot_exact_rhs(a, b_bf16, dims=_NN):
    hi, mid, lo = _split3(a)
    return _dot(hi, b_bf16, dims) + _dot(mid, b_bf16, dims) + _dot(lo, b_bf16, dims)


def _dot_exact_lhs(a_bf16, b, dims=_NN):
    hi, mid, lo = _split3(b)
    return _dot(a_bf16, hi, dims) + _dot(a_bf16, mid, dims) + _dot(a_bf16, lo, dims)


def _sigmoid(x):
    return 1.0 / (1.0 + jnp.exp(-x))


def _silu(x):
    return x * _sigmoid(x)


def _layer_norm(x, g, b, eps):
    mu = jnp.mean(x, axis=-1, keepdims=True)
    xc = x - mu
    var = jnp.mean(xc * xc, axis=-1, keepdims=True)
    return xc * lax.rsqrt(var + eps) * g + b


def _masked_softmax(s, maskf):
    s = jnp.where(maskf > 0.5, s, NEG)
    e = jnp.exp(s - jnp.max(s, axis=-1, keepdims=True)) * maskf
    return e / jnp.maximum(jnp.sum(e, axis=-1, keepdims=True), 1e-30)


def _proj_in_kernel(x_ref, w_ref, b_ref, o_ref, xb_ref):
    @pl.when(pl.program_id(1) == 0)
    def _():
        xb_ref[...] = x_ref[...].astype(BF16)

    o_ref[...] = _dot(xb_ref[...], w_ref[...]) + b_ref[...]


def _proj_in(x2d, w_bf16, b_row):
    n = x2d.shape[0]
    tm = min(1024, n)
    tn = 512
    return pl.pallas_call(
        _proj_in_kernel,
        grid=(n // tm, N_Z // tn),
        in_specs=[pl.BlockSpec((tm, D_MODEL), lambda i, j: (i, 0)),
                  pl.BlockSpec((D_MODEL, tn), lambda i, j: (0, j)),
                  pl.BlockSpec((1, tn), lambda i, j: (0, j))],
        out_specs=pl.BlockSpec((tm, tn), lambda i, j: (i, j)),
        out_shape=jax.ShapeDtypeStruct((n, N_Z), F32),
        scratch_shapes=[pltpu.VMEM((tm, D_MODEL), BF16)],
        compiler_params=_cparams(("parallel", "arbitrary")),
        name="proj_in",
    )(x2d, w_bf16, b_row)


_CONV_ROWS = 32


def _conv_kernel(val_ref, gate_ref, bg_ref, cg_ref, h_ref, bufa_ref, bufc_ref, wa_ref, ba_ref, lag_ref,
                 lab_ref, wc_ref, ya_ref, yc_ref, na_ref, nc_ref, ea_ref, ec_ref, *, tt, t_last):
    it = pl.program_id(1)
    nt = pl.num_programs(1)
    ha = 32
    hc = 8

    @pl.when(it == 0)
    def _():
        ea_ref[0:ha, :] = jnp.zeros((ha, GW), F32)
        ea_ref[ha - (CONV_A_WIDTH - 1):ha, :] = bufa_ref[0]
        ec_ref[0:hc, :] = jnp.zeros((hc, GW), F32)
        ec_ref[hc - (SHORT_CONV_WIDTH - 1):hc, :] = bufc_ref[0]

    @pl.when(it > 0)
    def _():
        ea_ref[0:ha, :] = ea_ref[tt:tt + ha, :]
        ec_ref[0:hc, :] = ec_ref[tt:tt + hc, :]

    ea_ref[ha:ha + tt, :] = val_ref[0] * _sigmoid(gate_ref[0])
    ec_ref[hc:hc + tt, :] = cg_ref[0] * h_ref[0]

    rc = min(_CONV_ROWS, tt)
    offa = ha - (CONV_A_WIDTH - 1)
    offc = hc - (SHORT_CONV_WIDTH - 1)
    for c in range(tt // rc):
        r0 = c * rc
        acc = jnp.zeros((rc, GW), F32)
        for j in range(CONV_A_WIDTH):
            acc = acc + wa_ref[j:j + 1, :] * ea_ref[r0 + offa + j:r0 + offa + j + rc, :]
        y = _layer_norm(acc + ba_ref[...], lag_ref[...], lab_ref[...], LN_EPS)
        ya_ref[0, r0:r0 + rc, :] = _silu(y)
        accc = jnp.zeros((rc, GW), F32)
        for j in range(SHORT_CONV_WIDTH):
            accc = accc + wc_ref[j:j + 1, :] * ec_ref[r0 + offc + j:r0 + offc + j + rc, :]
        yc_ref[0, r0:r0 + rc, :] = bg_ref[0, r0:r0 + rc, :] * accc

    @pl.when(it == nt - 1)
    def _():
        na_ref[0] = ea_ref[t_last + offa:t_last + ha, :]
        nc_ref[0] = ec_ref[t_last + offc:t_last + hc, :]


def _conv_mix(z, buf_a, buf_c, wa, ba, lag, lab, wc, t_real):
    B, T, _ = z.shape
    tt = min(256, T)
    nt = T // tt
    t_last = t_real - (nt - 1) * tt
    zspec = lambda col: pl.BlockSpec((1, tt, GW), lambda b, i: (b, i, col // GW))
    return pl.pallas_call(
        functools.partial(_conv_kernel, tt=tt, t_last=t_last),
        grid=(B, nt),
        in_specs=[zspec(C_VAL), zspec(C_GATE), zspec(C_BG), zspec(C_CG), zspec(C_H),
                  pl.BlockSpec((1, CONV_A_WIDTH - 1, GW), lambda b, i: (b, 0, 0)),
                  pl.BlockSpec((1, SHORT_CONV_WIDTH - 1, GW), lambda b, i: (b, 0, 0)),
                  _const_spec((CONV_A_WIDTH, GW)), _const_spec((1, GW)), _const_spec((1, GW)),
                  _const_spec((1, GW)), _const_spec((SHORT_CONV_WIDTH, GW))],
        out_specs=[pl.BlockSpec((1, tt, GW), lambda b, i: (b, i, 0)),
                   pl.BlockSpec((1, tt, GW), lambda b, i: (b, i, 0)),
                   pl.BlockSpec((1, CONV_A_WIDTH - 1, GW), lambda b, i: (b, 0, 0)),
                   pl.BlockSpec((1, SHORT_CONV_WIDTH - 1, GW), lambda b, i: (b, 0, 0))],
        out_shape=[jax.ShapeDtypeStruct((B, T, GW), F32), jax.ShapeDtypeStruct((B, T, GW), F32),
                   jax.ShapeDtypeStruct((B, CONV_A_WIDTH - 1, GW), F32),
                   jax.ShapeDtypeStruct((B, SHORT_CONV_WIDTH - 1, GW), F32)],
        scratch_shapes=[pltpu.VMEM((tt + 32, GW), F32), pltpu.VMEM((tt + 8, GW), F32)],
        compiler_params=_cparams(("parallel", "arbitrary")),
        name="conv_mix",
    )(z, z, z, z, z, buf_a, buf_c, wa, ba, lag, lab, wc)


def _rwkv_prep_kernel(r_ref, k_ref, v_ref, wl_ref, al_ref, gl_ref, sh_ref, mu_ref, w0_ref, w2_ref, a0_ref,
                      a2_ref, g2_ref, kkp_ref, kap_ref, rk_ref,
                      ro_ref, wo_ref, ko_ref, vo_ref, kko_ref, bo_ref, bonus_ref, g_ref, carry_ref, *, tt, t_last):
    it = pl.program_id(1)

    @pl.when(it == 0)
    def _():
        carry_ref[...] = sh_ref[0]

    row = lax.broadcasted_iota(jnp.int32, (tt, 1), 0)

    def mixed(x, c0):
        w = x.shape[1]
        prev = pltpu.roll(x, 1, 0) if tt > 1 else x
        prev = jnp.where(row == 0, carry_ref[:, c0:c0 + w], prev)
        return x + (prev - x) * mu_ref[:, c0:c0 + w]

    xr, xk, xv = r_ref[0], k_ref[0], v_ref[0]
    xwl, xal, xgl = wl_ref[0], al_ref[0], gl_ref[0]
    r = mixed(xr, 0)
    k = mixed(xk, GW)
    v = mixed(xv, 2 * GW)
    wl = mixed(xwl, 3 * GW)
    al = mixed(xal, 3 * GW + 128)
    gl = mixed(xgl, 3 * GW + 256)
    lr = t_last - 1
    carry_ref[:, 0:GW] = xr[lr:lr + 1]
    carry_ref[:, GW:2 * GW] = xk[lr:lr + 1]
    carry_ref[:, 2 * GW:3 * GW] = xv[lr:lr + 1]
    carry_ref[:, 3 * GW:3 * GW + 128] = xwl[lr:lr + 1]
    carry_ref[:, 3 * GW + 128:3 * GW + 256] = xal[lr:lr + 1]
    carry_ref[:, 3 * GW + 256:3 * GW + 384] = xgl[lr:lr + 1]

    y = w0_ref[...] + _dot3(jnp.tanh(wl), w2_ref[...])
    u = -y
    w_log = -(jnp.maximum(u, 0.0) + jnp.log(1.0 + jnp.exp(-jnp.abs(u)))) - 0.5
    log_decay = -jnp.exp(w_log)
    a = _sigmoid(a0_ref[...] + _dot3(al, a2_ref[...]))
    g_ref[0] = _dot3(_sigmoid(gl), g2_ref[...])
    kk = k * kkp_ref[...]
    k2 = k * (1.0 + (a - 1.0) * kap_ref[...])
    rkk = r * k2 * rk_ref[...]
    for h in range(HEADS):
        sl = slice(h * HEAD_DIM, (h + 1) * HEAD_DIM)
        kk_h = kk[:, sl]
        nrm = jnp.sqrt(jnp.sum(kk_h * kk_h, axis=-1, keepdims=True))
        kk_h = kk_h / jnp.maximum(nrm, 1e-12)
        v_h = v[:, sl]
        ro_ref[0, h] = r[:, sl]
        wo_ref[0, h] = log_decay[:, sl]
        ko_ref[0, h] = k2[:, sl]
        vo_ref[0, h] = v_h
        kko_ref[0, h] = kk_h
        bo_ref[0, h] = kk_h * a[:, sl]
        bonus_ref[0, h] = jnp.sum(rkk[:, sl], axis=-1, keepdims=True) * v_h


def _rwkv_prep(z, shift, mu, w0, w2p, a0, a2p, g2p, kkp, kap, rk, t_real):
    B, T, _ = z.shape
    tt = min(256, T)
    nt = T // tt
    t_last = t_real - (nt - 1) * tt
    wide = lambda col: pl.BlockSpec((1, tt, GW), lambda b, i: (b, i, col // GW))
    thin = lambda col: pl.BlockSpec((1, tt, 128), lambda b, i: (b, i, col // 128))
    head_out = pl.BlockSpec((1, HEADS, tt, HEAD_DIM), lambda b, i: (b, 0, i, 0))
    head_shape = jax.ShapeDtypeStruct((B, HEADS, T, HEAD_DIM), F32)
    dw = 3 * GW + 384
    return pl.pallas_call(
        functools.partial(_rwkv_prep_kernel, tt=tt, t_last=t_last),
        grid=(B, nt),
        in_specs=[wide(C_R), wide(C_K), wide(C_V), thin(C_WL), thin(C_AL), thin(C_GL),
                  pl.BlockSpec((1, 1, dw), lambda b, i: (b, 0, 0)),
                  _const_spec((1, dw)), _const_spec((1, GW)), _const_spec((128, GW)), _const_spec((1, GW)),
                  _const_spec((128, GW)), _const_spec((128, GW)), _const_spec((1, GW)), _const_spec((1, GW)),
                  _const_spec((1, GW))],
        out_specs=[head_out] * 7 + [pl.BlockSpec((1, tt, GW), lambda b, i: (b, i, 0))],
        out_shape=[head_shape] * 7 + [jax.ShapeDtypeStruct((B, T, GW), F32)],
        scratch_shapes=[pltpu.VMEM((1, dw), F32)],
        compiler_params=_cparams(("parallel", "arbitrary")),
        name="rwkv_prep",
    )(z, z, z, z, z, z, shift, mu, w0, w2p, a0, a2p, g2p, kkp, kap, rk)


def _rwkv_scan_kernel(r_ref, w_ref, k_ref, v_ref, kk_ref, b_ref, bonus_ref, s0_ref, lg_ref, lb_ref,
                      y_ref, s_ref, *, nb, tt, n_steps):
    it = pl.program_id(1)

    @pl.when(it == 0)
    def _():
        s_ref[...] = s0_ref[...]

    y_ref[...] = jnp.zeros(y_ref.shape, F32)
    ii = lax.broadcasted_iota(jnp.int32, (HEAD_DIM, HEAD_DIM), 0)
    jj = lax.broadcasted_iota(jnp.int32, (HEAD_DIM, HEAD_DIM), 1)
    eye = jnp.where(ii == jj, 1.0, 0.0).astype(F32)

    def step(t, carry):
        for n in range(nb):
            for h in range(HEADS):
                row = lambda ref: ref[n, h, pl.ds(t, 1), :]
                S = s_ref[n, h]
                sa = jnp.sum(S * row(kk_ref), axis=-1, keepdims=True)
                vcol = jnp.sum(eye * row(v_ref), axis=-1, keepdims=True)
                S = S * jnp.exp(row(w_ref)) - sa * row(b_ref) + vcol * row(k_ref)
                s_ref[n, h] = S
                ycol = jnp.sum(S * row(r_ref), axis=-1, keepdims=True)
                y_ref[n, h, pl.ds(t, 1), :] = jnp.sum(eye * ycol, axis=0, keepdims=True)
        return carry

    lax.fori_loop(0, n_steps, step, 0)

    for n in range(nb):
        for h in range(HEADS):
            y = y_ref[n, h]
            mu = jnp.mean(y, axis=-1, keepdims=True)
            yc = y - mu
            var = jnp.mean(yc * yc, axis=-1, keepdims=True)
            y_ref[n, h] = yc * lax.rsqrt(var + RWKV_GN_EPS) * lg_ref[h:h + 1, :] + lb_ref[h:h + 1, :] + bonus_ref[n, h]


def _rwkv_scan(r, w, k, v, kk, b, bonus, s0, lnx_g, lnx_b, t_real):
    B, H, T, _ = r.shape
    nb = 2
    tt = min(128, T)
    nt = T // tt
    n_steps = min(tt, t_real)
    blk = pl.BlockSpec((nb, H, tt, HEAD_DIM), lambda b_, i: (b_, 0, i, 0))
    sblk = pl.BlockSpec((nb, H, HEAD_DIM, HEAD_DIM), lambda b_, i: (b_, 0, 0, 0))
    return pl.pallas_call(
        functools.partial(_rwkv_scan_kernel, nb=nb, tt=tt, n_steps=n_steps),
        grid=(B // nb, nt),
        in_specs=[blk] * 7 + [sblk, _const_spec((H, HEAD_DIM)), _const_spec((H, HEAD_DIM))],
        out_specs=[blk, sblk],
        out_shape=[jax.ShapeDtypeStruct((B, H, T, HEAD_DIM), F32),
                   jax.ShapeDtypeStruct((B, H, HEAD_DIM, HEAD_DIM), F32)],
        compiler_params=_cparams(("parallel", "arbitrary")),
        name="rwkv_scan",
    )(r, w, k, v, kk, b, bonus, s0, lnx_g, lnx_b)


_RW_CHUNK = 64

_TN = (((0,), (0,)), ((), ()))


def _rwkv_chunk_kernel(r_ref, lw_ref, k_ref, v_ref, kk_ref, b_ref, bonus_ref, s0_ref, lg_ref, lb_ref, y_ref, s_ref):
    C = r_ref.shape[2]
    it = pl.program_id(1)

    @pl.when(it == 0)
    def _():
        s_ref[...] = s0_ref[...]

    ti = lax.broadcasted_iota(jnp.int32, (C, C), 0)
    si = lax.broadcasted_iota(jnp.int32, (C, C), 1)
    incl = ti >= si
    strict = ti > si
    tri_b = jnp.where(incl, 1.0, 0.0).astype(BF16)
    n_fac = C.bit_length() - 1
    hs = range(HEADS)
    v = [v_ref[0, h] for h in hs]
    cs = [_dot_exact_lhs(tri_b, lw_ref[0, h]) for h in hs]
    cs_last = [c[C - 1:C, :] for c in cs]
    AR = [jnp.concatenate([kk_ref[0, h] * jnp.exp(cs[h] - lw_ref[0, h]), r_ref[0, h] * jnp.exp(cs[h])], axis=0)
          for h in hs]
    BK = []
    for h in hs:
        g_inv = jnp.exp(-cs[h])
        BK.append(jnp.concatenate([b_ref[0, h] * g_inv, k_ref[0, h] * g_inv], axis=0))
    G = [_dot3(AR[h], BK[h], _NT) for h in hs]
    S0 = [s_ref[0, h] for h in hs]
    ARS = [_dot3(AR[h], S0[h], _NT) for h in hs]
    LMV = [_dot3(jnp.concatenate([jnp.where(strict, G[h][0:C, C:2 * C], 0.0),
                                  jnp.where(incl, G[h][C:2 * C, C:2 * C], 0.0)], axis=0), v[h]) for h in hs]
    X = [ARS[h][0:C] + LMV[h][0:C] for h in hs]
    N = [-jnp.where(strict, G[h][0:C, 0:C], 0.0) for h in hs]
    for f in range(n_fac):
        X = [X[h] + _dot3(N[h], X[h]) for h in hs]
        if f + 1 < n_fac:
            N = [_dot3(N[h], N[h]) for h in hs]
    Y = [ARS[h][C:2 * C] + LMV[h][C:2 * C] - _dot3(jnp.where(incl, G[h][C:2 * C, 0:C], 0.0), X[h]) for h in hs]
    for h in hs:
        g_end = jnp.exp(cs_last[h] - cs[h])
        UV = jnp.concatenate([-X[h], v[h]], axis=0)
        BKg = jnp.concatenate([b_ref[0, h] * g_end, k_ref[0, h] * g_end], axis=0)
        s_ref[0, h] = S0[h] * jnp.exp(cs_last[h]) + _dot3(UV, BKg, _TN)
    for h in hs:
        mu = jnp.mean(Y[h], axis=-1, keepdims=True)
        yc = Y[h] - mu
        var = jnp.mean(yc * yc, axis=-1, keepdims=True)
        y_ref[0, h] = yc * lax.rsqrt(var + RWKV_GN_EPS) * lg_ref[h:h + 1, :] + lb_ref[h:h + 1, :] + bonus_ref[0, h]


def _rwkv_chunked(r, lw, k, v, kk, b, bonus, s0, lnx_g, lnx_b):
    B, H, T, _ = r.shape
    C = _RW_CHUNK
    blk = pl.BlockSpec((1, H, C, HEAD_DIM), lambda b_, i: (b_, 0, i, 0))
    sblk = pl.BlockSpec((1, H, HEAD_DIM, HEAD_DIM), lambda b_, i: (b_, 0, 0, 0))
    return pl.pallas_call(
        _rwkv_chunk_kernel,
        grid=(B, T // C),
        in_specs=[blk] * 7 + [sblk, _const_spec((H, HEAD_DIM)), _const_spec((H, HEAD_DIM))],
        out_specs=[blk, sblk],
        out_shape=[jax.ShapeDtypeStruct((B, H, T, HEAD_DIM), F32),
                   jax.ShapeDtypeStruct((B, H, HEAD_DIM, HEAD_DIM), F32)],
        compiler_params=_cparams(("parallel", "arbitrary")),
        name="rwkv_chunked",
    )(r, lw, k, v, kk, b, bonus, s0, lnx_g, lnx_b)


def _compress_tail(F, cst):
    n = F.shape[0]
    nxt = pltpu.roll(F[:, HEAD_DIM:2 * HEAD_DIM], n - 1, 0)
    const = cst[0:1, 0:HEAD_DIM] + cst[1:2, HEAD_DIM:2 * HEAD_DIM]
    return F[:, 0:HEAD_DIM] + nxt + const


def _compress_kernel(x_ref, w_ref, pe_ref, o_ref):
    F = _dot3(x_ref[0, 0, 0], w_ref[0])
    cst = _dot3(pe_ref[0], w_ref[0])
    o_ref[0, 0, 0] = _compress_tail(F, cst)


def _compress_prompt(cmp_in, wcat, pe2):
    B, _, _, nc, _ = cmp_in.shape
    return pl.pallas_call(
        _compress_kernel,
        grid=(B, 2, 2),
        in_specs=[pl.BlockSpec((1, 1, 1, nc, 1024), lambda b, a, g: (b, a, g, 0, 0)),
                  pl.BlockSpec((1, 1024, 128), lambda b, a, g: (a, 0, 0)),
                  pl.BlockSpec((1, 8, 1024), lambda b, a, g: (a, 0, 0))],
        out_specs=pl.BlockSpec((1, 1, 1, nc, HEAD_DIM), lambda b, a, g: (b, a, g, 0, 0)),
        out_shape=jax.ShapeDtypeStruct((B, 2, 2, nc, HEAD_DIM), F32),
        compiler_params=_cparams(("parallel", "parallel", "parallel")),
        name="nsa_compress",
    )(cmp_in, wcat, pe2)


def _top_select(selT, j_f):
    v = selT
    sel = jnp.zeros(selT.shape, F32)
    for _ in range(N_SLC):
        m = jnp.max(v, axis=0, keepdims=True)
        first = jnp.min(jnp.where(v == m, j_f, 1e9), axis=0, keepdims=True)
        hit = j_f == first
        sel = jnp.where(hit, 1.0, sel)
        v = jnp.where(hit, -3e38, v)
    return sel


def _softmax2(s):
    e = jnp.exp2(s - jnp.max(s, axis=-1, keepdims=True))
    return e * (1.0 / jnp.sum(e, axis=-1, keepdims=True))


def _gated_heads(gts, g, o_cmp, o_slc, o_win, rows):
    out = []
    for m in range(Q_PER_KV):
        c0 = g * 3 * Q_PER_KV + m * 3
        sl = slice(m * rows, (m + 1) * rows)
        out.append(gts[:, c0:c0 + 1] * o_cmp[sl] + gts[:, c0 + 1:c0 + 2] * o_slc[sl] + gts[:, c0 + 2:c0 + 3] * o_win[sl])
    return out


_KT = 512


_LOG2E = 1.4426950408889634


def _nsa_prompt_kernel(q_ref, gt_ref, ckv_ref, ksT_ref, vs_ref, kwT_ref, vw_ref, mt_ref, o_ref, *, T):
    i = pl.program_id(1)
    start = i * Q_BLOCK
    nc = T // CMP_STRIDE
    ns = T // SLC_BLOCK
    rows = Q_PER_KV * Q_BLOCK
    gs = range(KV_HEADS)
    t_tok = start + lax.broadcasted_iota(jnp.int32, (Q_BLOCK, 1), 0)
    t_rows = jnp.concatenate([t_tok] * Q_PER_KV, axis=0)
    gts = _sigmoid(gt_ref[0])
    j_col = lax.broadcasted_iota(jnp.int32, (ns, 1), 0)
    j_f = j_col.astype(F32)
    t_lane = start + lax.broadcasted_iota(jnp.int32, (1, Q_BLOCK), 1)
    cur = t_lane >> 6
    forced = jnp.where(j_col == 0, 1.0, jnp.where(j_col == cur, 1.0, jnp.where(j_col == cur - 1, 1.0, 0.0)))
    r128 = lax.broadcasted_iota(jnp.int32, (Q_BLOCK, Q_BLOCK), 0)
    c128 = lax.broadcasted_iota(jnp.int32, (Q_BLOCK, Q_BLOCK), 1)
    eye128 = jnp.where(r128 == c128, 1.0, 0.0).astype(BF16)
    Q = [jnp.concatenate([q_ref[0, :, (g * Q_PER_KV + m) * HEAD_DIM:(g * Q_PER_KV + m + 1) * HEAD_DIM]
                          for m in range(Q_PER_KV)], axis=0) * (HEAD_DIM ** -0.5 * _LOG2E) for g in gs]
    Qb = [q.astype(BF16) for q in Q]

    n_lane = lax.broadcasted_iota(jnp.int32, (1, nc), 1)
    bias_c = jnp.where(n_lane * CMP_STRIDE + (2 * CMP_STRIDE - 1) <= t_rows, 0.0, NEG)
    any_c = jnp.where(t_rows >= 2 * CMP_STRIDE - 1, 1.0, 0.0)
    pc = [_softmax2(_dot3(Q[g], ckv_ref[0, 0, g], _NT) + bias_c) * any_c for g in gs]
    o_cmp = [_dot(pc[g].astype(BF16), ckv_ref[0, 1, g].astype(BF16)) for g in gs]
    selb = []
    for g in gs:
        ps = pc[g][0:Q_BLOCK] + pc[g][Q_BLOCK:2 * Q_BLOCK] + pc[g][2 * Q_BLOCK:3 * Q_BLOCK] + pc[g][3 * Q_BLOCK:4 * Q_BLOCK]
        selT = _dot_exact_lhs(mt_ref[...], ps, _NT)
        selT = jnp.where(forced > 0.5, FORCED, jnp.where(j_col * SLC_BLOCK <= t_lane, selT, -FORCED))
        selbT = jnp.where(_top_select(selT, j_f) > 0.5, 0.0, NEG).astype(BF16)
        selb.append(_dot(eye128, selbT, _NT).astype(BF16))

    def key_tile(kt, carry, causal):
        jr = lax.broadcasted_iota(jnp.int32, (ns, _KT), 0)
        cl = lax.broadcasted_iota(jnp.int32, (ns, _KT), 1)
        expand = jnp.where(jr == kt * (_KT // SLC_BLOCK) + (cl >> 6), 1.0, 0.0).astype(BF16)
        if causal:
            pos = kt * _KT + lax.broadcasted_iota(jnp.int32, (1, _KT), 1)
            bias_t = jnp.where(pos <= t_rows, 0.0, NEG)
        out = []
        for g in gs:
            m_run, acc = carry[g]
            bias = _dot(selb[g], expand)
            s = _dot(Qb[g], ksT_ref[0, g, kt]) + jnp.concatenate([bias] * Q_PER_KV, axis=0)
            if causal:
                s = s + bias_t
            m_new = jnp.maximum(m_run, jnp.max(s, axis=-1, keepdims=True))
            p = jnp.exp2(s - m_new)
            corr = jnp.exp2(m_run - m_new)
            vt = vs_ref[0, g, pl.ds(pl.multiple_of(kt * _KT, _KT), _KT), :]
            out.append((m_new, corr * acc + _dot(p.astype(BF16), vt)))
        return tuple(out)

    normed = lambda a: a[:, 0:HEAD_DIM] * (1.0 / a[:, HEAD_DIM:HEAD_DIM + 1])
    init = tuple((jnp.full((rows, 1), NEG, F32), jnp.zeros((rows, 2 * HEAD_DIM), F32)) for _ in gs)
    last = start // _KT
    carry = lax.fori_loop(0, last, lambda kt, c: key_tile(kt, c, False), init)
    carry = key_tile(last, carry, True)
    o_slc = [normed(carry[g][1]) for g in gs]

    nwt = WINDOW // Q_BLOCK + 1
    wb = jnp.maximum(i - WINDOW // Q_BLOCK, 0)
    dt = t_rows - (wb * Q_BLOCK + lax.broadcasted_iota(jnp.int32, (1, nwt * Q_BLOCK), 1))
    bias_w = jnp.where(dt >= 0, jnp.where(dt < WINDOW, 0.0, NEG), NEG)
    pieces = []
    for g in gs:
        sw = jnp.concatenate([_dot(Qb[g], kwT_ref[0, g, wb + u]) for u in range(nwt)], axis=1) + bias_w
        vwt = vw_ref[0, g, pl.ds(pl.multiple_of(wb * Q_BLOCK, Q_BLOCK), nwt * Q_BLOCK), :]
        pw = jnp.exp2(sw - jnp.max(sw, axis=-1, keepdims=True))
        o_win = normed(_dot(pw.astype(BF16), vwt))
        pieces += _gated_heads(gts, g, o_cmp[g], o_slc[g], o_win, Q_BLOCK)
    o_ref[0] = jnp.concatenate(pieces, axis=1)


def _sel_matrix(n_rows, n_chunks):
    j = np.arange(n_rows)[:, None]
    c = np.arange(n_chunks)[None, :]
    per = SLC_BLOCK // CMP_STRIDE
    m = 0.5 * ((c // per == j).astype(np.float32) + ((c + 1) // per == j).astype(np.float32))
    return jnp.asarray(m, BF16)


def _nsa_prompt(z, ckv, ksT, vs, kwT, vw):
    B, T, _ = z.shape
    nc, ns = T // CMP_STRIDE, T // SLC_BLOCK
    nq = T // Q_BLOCK
    mt = _sel_matrix(ns, nc)
    per_b = lambda shape: pl.BlockSpec((1,) + shape, lambda b, i: (b,) + (0,) * len(shape))
    return pl.pallas_call(
        functools.partial(_nsa_prompt_kernel, T=T),
        grid=(B, nq),
        in_specs=[pl.BlockSpec((1, Q_BLOCK, GW), lambda b, i: (b, i, C_Q // GW)),
                  pl.BlockSpec((1, Q_BLOCK, 256), lambda b, i: (b, i, C_GATES // 256)),
                  per_b((2, KV_HEADS, nc, HEAD_DIM)),
                  per_b((KV_HEADS, T // _KT, HEAD_DIM, _KT)),
                  per_b((KV_HEADS, T, 2 * HEAD_DIM)),
                  per_b((KV_HEADS, nq, HEAD_DIM, Q_BLOCK)),
                  per_b((KV_HEADS, T, 2 * HEAD_DIM)),
                  _const_spec((ns, nc))],
        out_specs=pl.BlockSpec((1, Q_BLOCK, GW), lambda b, i: (b, i, 0)),
        out_shape=jax.ShapeDtypeStruct((B, T, GW), F32),
        compiler_params=_cparams(("parallel", "arbitrary")),
        name="nsa_prompt",
    )(z, z, ckv, ksT, vs, kwT, vw, mt)


_PPS = 8
_SEL_ROWS = 384


def _nsa_sample_a_kernel(pt_ref, *refs, n_pages, past_len, t_pad):
    pages = refs[:_PPS]
    w_ref, pe_ref, q_ref, mt_ref = refs[_PPS:_PPS + 4]
    ocmp_ref, selm_ref = refs[_PPS + 4:_PPS + 6]
    x_ref = refs[_PPS + 6]
    s = pl.program_id(1)
    cpp = PAGE // CMP_STRIDE
    for k in range(_PPS):
        x_ref[:, pl.ds(pl.multiple_of((s * _PPS + k) * cpp, cpp), cpp), :] = pages[k][0]

    @pl.when(s == pl.num_programs(1) - 1)
    def _():
        nc = n_pages * cpp
        n_blocks = past_len // SLC_BLOCK + 1
        lanes = 128
        ps_rows = []
        for g in range(KV_HEADS):
            ck = _compress_tail(_dot3(x_ref[g], w_ref[0]), _dot3(pe_ref[0], w_ref[0]))
            cv = _compress_tail(_dot(x_ref[KV_HEADS + g].astype(BF16), w_ref[1].astype(BF16)), _dot3(pe_ref[1], w_ref[1]))
            Q = jnp.concatenate([q_ref[0, :, (g * Q_PER_KV + m) * HEAD_DIM:(g * Q_PER_KV + m + 1) * HEAD_DIM]
                                 for m in range(Q_PER_KV)], axis=0) * (HEAD_DIM ** -0.5)
            t_rows = past_len + (lax.broadcasted_iota(jnp.int32, (Q_PER_KV * t_pad, 1), 0) & (t_pad - 1))
            sc = _dot3(Q, ck, _NT)
            n_lane = lax.broadcasted_iota(jnp.int32, (1, nc), 1)
            mask_c = jnp.where(n_lane * CMP_STRIDE + (2 * CMP_STRIDE - 1) <= t_rows, 1.0, 0.0)
            pc = _masked_softmax(sc, mask_c)
            ocmp_ref[0, g] = _dot(pc.astype(BF16), cv.astype(BF16))
            ps_rows.append(pc[0:t_pad] + pc[t_pad:2 * t_pad] + pc[2 * t_pad:3 * t_pad] + pc[3 * t_pad:4 * t_pad])
        ps = jnp.concatenate(ps_rows + [jnp.zeros((lanes - KV_HEADS * t_pad, nc), F32)], axis=0)
        selT = _dot_exact_lhs(mt_ref[...], ps, _NT)
        j_col = lax.broadcasted_iota(jnp.int32, (_SEL_ROWS, 1), 0)
        t_lane = past_len + (lax.broadcasted_iota(jnp.int32, (1, lanes), 1) & (t_pad - 1))
        cur = t_lane >> 6
        forced = jnp.where(j_col == 0, 1.0, jnp.where(j_col == cur, 1.0, jnp.where(j_col == cur - 1, 1.0, 0.0)))
        selT = jnp.where(forced > 0.5, FORCED, jnp.where(j_col * SLC_BLOCK <= t_lane, selT, -FORCED))
        selT = jnp.where(j_col < n_blocks, selT, -3.0 * FORCED)
        selmT = _top_select(selT, j_col.astype(F32))
        r = lax.broadcasted_iota(jnp.int32, (lanes, lanes), 0)
        c = lax.broadcasted_iota(jnp.int32, (lanes, lanes), 1)
        eye = jnp.where(r == c, 1.0, 0.0).astype(BF16)
        selm = _dot(eye, selmT.astype(BF16), _NT)
        selm_ref[0] = selm[0:KV_HEADS * t_pad]


def _nsa_sample_a(page_table, cmp_pages, layer, wcat, pe2, z, past_len):
    B, t_pad, _ = z.shape
    n_pages = page_table.shape[1]
    nc = n_pages * (PAGE // CMP_STRIDE)
    mt = _sel_matrix(_SEL_ROWS, nc)
    page_spec = lambda k: pl.BlockSpec((None, 1, 2 * KV_HEADS, PAGE // CMP_STRIDE, 1024),
                                       lambda b, s, pt: (layer, pt[b, s * _PPS + k], 0, 0, 0))
    cst = lambda shape: pl.BlockSpec(shape, lambda b, s, pt: (0,) * len(shape))
    grid_spec = pltpu.PrefetchScalarGridSpec(
        num_scalar_prefetch=1,
        grid=(B, n_pages // _PPS),
        in_specs=[page_spec(k) for k in range(_PPS)] + [
            cst((2, 1024, 128)), cst((2, 8, 1024)),
            pl.BlockSpec((1, t_pad, GW), lambda b, s, pt: (b, 0, C_Q // GW)),
            cst((_SEL_ROWS, nc))],
        out_specs=[pl.BlockSpec((1, KV_HEADS, Q_PER_KV * t_pad, HEAD_DIM), lambda b, s, pt: (b, 0, 0, 0)),
                   pl.BlockSpec((1, KV_HEADS * t_pad, _SEL_ROWS), lambda b, s, pt: (b, 0, 0))],
        scratch_shapes=[pltpu.VMEM((2 * KV_HEADS, nc, 1024), F32)])
    return pl.pallas_call(
        functools.partial(_nsa_sample_a_kernel, n_pages=n_pages, past_len=past_len, t_pad=t_pad),
        grid_spec=grid_spec,
        out_shape=[jax.ShapeDtypeStruct((B, KV_HEADS, Q_PER_KV * t_pad, HEAD_DIM), F32),
                   jax.ShapeDtypeStruct((B, KV_HEADS * t_pad, _SEL_ROWS), F32)],
        compiler_params=_cparams(("parallel", "arbitrary")),
        name="nsa_sample_select",
    )(page_table, *([cmp_pages] * _PPS), wcat, pe2, z, mt)


def _nsa_sample_b_kernel(pt_ref, *refs, n_pages, past_len, t_pad, t_real):
    kpages = refs[:_PPS]
    vpages = refs[_PPS:2 * _PPS]
    q_ref, rows_ref, kwvw_ref, gt_ref, selm_ref, ocmp_ref, cw_ref = refs[2 * _PPS:2 * _PPS + 7]
    o_ref = refs[2 * _PPS + 7]
    m_ref, l_ref, acc_ref = refs[2 * _PPS + 8:]
    s = pl.program_id(1)
    n_steps = n_pages // _PPS
    rows = Q_PER_KV * t_pad
    t_rows = past_len + (lax.broadcasted_iota(jnp.int32, (rows, 1), 0) & (t_pad - 1))

    @pl.when(s == 0)
    def _():
        m_ref[...] = jnp.full(m_ref.shape, NEG, F32)
        l_ref[...] = jnp.zeros(l_ref.shape, F32)
        acc_ref[...] = jnp.zeros(acc_ref.shape, F32)

    def q_rows(g):
        Q = jnp.concatenate([q_ref[0, :, (g * Q_PER_KV + m) * HEAD_DIM:(g * Q_PER_KV + m + 1) * HEAD_DIM]
                             for m in range(Q_PER_KV)], axis=0) * (HEAD_DIM ** -0.5)
        return Q.astype(BF16)

    def online(g, sc, ok, values):
        sc = jnp.where(ok > 0.5, sc, NEG)
        m_run = m_ref[g]
        m_new = jnp.maximum(m_run, jnp.max(sc, axis=-1, keepdims=True))
        p = jnp.exp(sc - m_new) * ok
        corr = jnp.exp(m_run - m_new)
        l_ref[g] = corr * l_ref[g] + jnp.sum(p, axis=-1, keepdims=True)
        acc_ref[g] = corr * acc_ref[g] + values(p.astype(BF16))
        m_ref[g] = m_new

    @pl.when(s < n_steps)
    def _():
        keys = _PPS * PAGE
        jr = lax.broadcasted_iota(jnp.int32, (_SEL_ROWS, keys), 0)
        cl = lax.broadcasted_iota(jnp.int32, (_SEL_ROWS, keys), 1)
        expand = jnp.where(jr == s * (keys // SLC_BLOCK) + (cl >> 6), 1.0, 0.0).astype(BF16)
        mv = _dot(selm_ref[0].astype(BF16), expand)
        pos = s * keys + lax.broadcasted_iota(jnp.int32, (1, keys), 1)
        for g in range(KV_HEADS):
            Qb = q_rows(g)
            sc = jnp.concatenate([_dot(Qb, kpages[k][g].astype(BF16)) for k in range(_PPS)], axis=1)
            mvg = jnp.concatenate([mv[g * t_pad:(g + 1) * t_pad]] * Q_PER_KV, axis=0)
            ok = jnp.where(pos <= t_rows, mvg, 0.0)

            def values(pb, g=g):
                out = _dot(pb[:, 0:PAGE], vpages[0][g].astype(BF16), _NT)
                for k in range(1, _PPS):
                    out = out + _dot(pb[:, k * PAGE:(k + 1) * PAGE], vpages[k][g].astype(BF16), _NT)
                return out

            online(g, sc, ok, values)

    @pl.when(s == n_steps)
    def _():
        gts = _sigmoid(gt_ref[0])
        last_blk = past_len // SLC_BLOCK
        r_lane = lax.broadcasted_iota(jnp.int32, (1, t_pad), 1)
        pieces = []
        for g in range(KV_HEADS):
            Qb = q_rows(g)
            k_new = rows_ref[0, :, (2 * KV_HEADS + g) * HEAD_DIM:(2 * KV_HEADS + g + 1) * HEAD_DIM]
            v_new = rows_ref[0, :, (3 * KV_HEADS + g) * HEAD_DIM:(3 * KV_HEADS + g + 1) * HEAD_DIM]
            sc = _dot(Qb, k_new.astype(BF16), _NT)
            sel_last = jnp.concatenate([selm_ref[0, g * t_pad:(g + 1) * t_pad, last_blk:last_blk + 1]] * Q_PER_KV, axis=0)
            vis = jnp.where(past_len + r_lane <= t_rows, jnp.where(r_lane < t_real, 1.0, 0.0), 0.0)
            online(g, sc, vis * sel_last, lambda pb, v_new=v_new: _dot(pb, v_new.astype(BF16)))
            o_slc = acc_ref[g] / jnp.maximum(l_ref[g], 1e-30)
            wrows = cw_ref.shape[3]
            kw_new = kwvw_ref[0, :, g * HEAD_DIM:(g + 1) * HEAD_DIM]
            vw_new = kwvw_ref[0, :, (KV_HEADS + g) * HEAD_DIM:(KV_HEADS + g + 1) * HEAD_DIM]
            sw = jnp.concatenate([_dot(Qb, cw_ref[0, g].astype(BF16)),
                                  _dot(Qb, kw_new.astype(BF16), _NT)], axis=1)
            idx = lax.broadcasted_iota(jnp.int32, (1, wrows + t_pad), 1)
            spos = past_len - wrows + idx
            dt = t_rows - spos
            real = jnp.where(idx < wrows + t_real, 1.0, 0.0)
            mask_w = jnp.where(dt >= 0, jnp.where(dt < WINDOW, real, 0.0), 0.0)
            pw = _masked_softmax(sw, mask_w).astype(BF16)
            o_win = _dot(pw[:, 0:wrows], cw_ref[1, g].astype(BF16), _NT) + _dot(pw[:, wrows:], vw_new.astype(BF16))
            pieces += _gated_heads(gts, g, ocmp_ref[0, g], o_slc, o_win, t_pad)
        o_ref[0] = jnp.concatenate(pieces, axis=1)


def _nsa_sample_b(page_table, cache_t, win_t, layer, z, selm, ocmp, past_len, t_real):
    B, t_pad, _ = z.shape
    n_pages = page_table.shape[1]
    n_steps = n_pages // _PPS
    rows = Q_PER_KV * t_pad
    wrows = win_t.shape[5]
    pidx = lambda b, s, pt, k: pt[b, jnp.minimum(s, n_steps - 1) * _PPS + k]
    page = lambda which, k: pl.BlockSpec((None, None, None, KV_HEADS, HEAD_DIM, PAGE),
                                         lambda b, s, pt: (layer, pidx(b, s, pt, k), which, 0, 0, 0))
    kspec = lambda k: page(2, k)
    vspec = lambda k: page(3, k)
    zspec = lambda col, w: pl.BlockSpec((1, t_pad, w), lambda b, s, pt: (b, 0, col // w))
    per_b = lambda shape: pl.BlockSpec((1,) + shape, lambda b, s, pt: (b,) + (0,) * len(shape))
    grid_spec = pltpu.PrefetchScalarGridSpec(
        num_scalar_prefetch=1,
        grid=(B, n_steps + 1),
        in_specs=[kspec(k) for k in range(_PPS)] + [vspec(k) for k in range(_PPS)] + [
            zspec(C_Q, GW), zspec(C_ROWS, GW), zspec(C_KW, 256), zspec(C_GATES, 256),
            per_b((KV_HEADS * t_pad, _SEL_ROWS)), per_b((KV_HEADS, rows, HEAD_DIM)),
            pl.BlockSpec((None, None, 2, KV_HEADS, HEAD_DIM, wrows), lambda b, s, pt: (layer, b, 0, 0, 0, 0))],
        out_specs=pl.BlockSpec((1, t_pad, GW), lambda b, s, pt: (b, 0, 0)),
        scratch_shapes=[pltpu.VMEM((KV_HEADS, rows, 1), F32), pltpu.VMEM((KV_HEADS, rows, 1), F32),
                        pltpu.VMEM((KV_HEADS, rows, HEAD_DIM), F32)])
    return pl.pallas_call(
        functools.partial(_nsa_sample_b_kernel, n_pages=n_pages, past_len=past_len, t_pad=t_pad, t_real=t_real),
        grid_spec=grid_spec,
        out_shape=jax.ShapeDtypeStruct((B, t_pad, GW), F32),
        compiler_params=_cparams(("parallel", "arbitrary")),
        name="nsa_sample_attend",
    )(page_table, *([cache_t] * (2 * _PPS)), z, z, z, z, selm, ocmp, win_t)


def _route(h, w, b):
    logits = _dot3(h, w) + b
    lane = lax.broadcasted_iota(jnp.int32, logits.shape, 1).astype(F32)
    big = 1e6
    is_g = lane < N_GROUPS
    gl = jnp.where(is_g, logits, NEG)
    gmax = jnp.max(gl, axis=-1, keepdims=True)
    gsel = jnp.min(jnp.where(is_g, jnp.where(gl == gmax, lane, big), big), axis=-1, keepdims=True)
    gp = 1.0 / jnp.sum(jnp.where(is_g, jnp.exp(gl - gmax), 0.0), axis=-1, keepdims=True)
    lo = N_GROUPS + gsel * EXP_PER_GROUP
    in_grp = jnp.where(lane >= lo, jnp.where(lane < lo + EXP_PER_GROUP, 1.0, 0.0), 0.0)
    el = jnp.where(in_grp > 0.5, logits, NEG)
    e = jnp.exp(el - jnp.max(el, axis=-1, keepdims=True)) * in_grp
    p = e / jnp.sum(e, axis=-1, keepdims=True)
    pm = jnp.where(in_grp > 0.5, p, -1.0)
    v1 = jnp.max(pm, axis=-1, keepdims=True)
    i1 = jnp.min(jnp.where(pm == v1, lane, big), axis=-1, keepdims=True)
    pm2 = jnp.where(lane == i1, -1.0, pm)
    v2 = jnp.max(pm2, axis=-1, keepdims=True)
    i2 = jnp.min(jnp.where(pm2 == v2, lane, big), axis=-1, keepdims=True)
    tot = v1 + v2
    return jnp.where(lane == 0, i1 - N_GROUPS,
                     jnp.where(lane == 1, i2 - N_GROUPS,
                               jnp.where(lane == 2, v1 / tot * gp, jnp.where(lane == 3, v2 / tot * gp, 0.0))))


def _proj_out_kernel(x_ref, ya_ref, yb_ref, yc_ref, yd_ref, g_ref, w_ref, b_ref, lg_ref, lb_ref, rw_ref, rb_ref,
                     h_ref, r_ref, *, alpha):
    yd = yd_ref[...] * g_ref[...]
    acc = _dot(ya_ref[...].astype(BF16), w_ref[0:GW, :])
    acc = acc + _dot(yb_ref[...].astype(BF16), w_ref[GW:2 * GW, :])
    acc = acc + _dot(yc_ref[...].astype(BF16), w_ref[2 * GW:3 * GW, :])
    acc = acc + _dot(yd.astype(BF16), w_ref[3 * GW:4 * GW, :])
    h = _layer_norm(alpha * x_ref[...] + acc + b_ref[...], lg_ref[...], lb_ref[...], LN_EPS)
    h_ref[...] = h
    r_ref[...] = _route(h, rw_ref[...], rb_ref[...])


def _proj_out(x2d, ya, yb, yc, yd, g, lp, alpha):
    n = x2d.shape[0]
    tm = min(256, n)
    row = lambda w: pl.BlockSpec((tm, w), lambda i: (i, 0))
    return pl.pallas_call(
        functools.partial(_proj_out_kernel, alpha=alpha),
        grid=(n // tm,),
        in_specs=[row(D_MODEL), row(GW), row(GW), row(GW), row(GW), row(GW),
                  _const_spec((D_MODEL, D_MODEL)), _const_spec((1, D_MODEL)), _const_spec((1, D_MODEL)),
                  _const_spec((1, D_MODEL)), _const_spec((D_MODEL, 128)), _const_spec((1, 128))],
        out_specs=[row(D_MODEL), row(128)],
        out_shape=[jax.ShapeDtypeStruct((n, D_MODEL), F32), jax.ShapeDtypeStruct((n, 128), F32)],
        compiler_params=_cparams(("parallel",)),
        name="proj_out_ln1",
    )(x2d, ya, yb, yc, yd, g, lp["w_out"], lp["b_out"], lp["ln1_g"], lp["ln1_b"], lp["router_w"], lp["router_b"])


def _for_rows(n_rows, body):
    if isinstance(n_rows, int):
        lax.fori_loop(0, n_rows, lambda r, c: (body(r), c)[1], 0, unroll=8)
        return
    groups = n_rows // 8

    def group(i, c):
        for u in range(8):
            body(i * 8 + u)
        return c

    lax.fori_loop(0, groups, group, 0)
    lax.fori_loop(groups * 8, n_rows, lambda r, c: (body(r), c)[1], 0)


def _issue_rows(idx_fn, n_rows, src_ref, dst_ref, sem):
    _for_rows(n_rows, lambda r: pltpu.make_async_copy(src_ref.at[pl.ds(idx_fn(r), 1), :],
                                                      dst_ref.at[pl.ds(r, 1), :], sem).start())


def _wait_rows(n_rows, src_ref, dst_ref, sem):
    _for_rows(n_rows, lambda r: pltpu.make_async_copy(src_ref.at[pl.ds(0, 1), :],
                                                      dst_ref.at[pl.ds(r, 1), :], sem).wait())


def _moe_kernel(te_ref, nu_ref, src_ref, cnt_ref, first_ref, nxt_ref, wslot_ref, tok_ref,
                h_ref, wg_ref, wu_ref, wd_ref, o_ref,
                xbuf_ref, sems, wgs_ref, wus_ref, wds_ref, wsems, wgb_ref, wub_ref, wdb_ref, *, layer):
    t = pl.program_id(0)
    tm = o_ref.shape[0]
    slot = t % 2
    n_used = nu_ref[0]

    def fetch(tile, to_slot):
        src = src_ref[tile]
        _issue_rows(lambda r: tok_ref[src + r], cnt_ref[tile], h_ref, xbuf_ref.at[to_slot], sems.at[to_slot])

    def weight_copies(e, s):
        return (pltpu.make_async_copy(wg_ref.at[layer, e], wgs_ref.at[s], wsems.at[s, 0]),
                pltpu.make_async_copy(wu_ref.at[layer, e], wus_ref.at[s], wsems.at[s, 1]),
                pltpu.make_async_copy(wd_ref.at[layer, e], wds_ref.at[s], wsems.at[s, 2]))

    @pl.when(t == 0)
    def _():
        for c in weight_copies(te_ref[0], 0):
            c.start()
        xbuf_ref[...] = jnp.zeros(xbuf_ref.shape, F32)
        fetch(0, 0)

    @pl.when(t + 1 < n_used)
    def _():
        fetch(t + 1, 1 - slot)

    @pl.when(first_ref[t] == 1)
    def _():
        s = wslot_ref[t]
        for c in weight_copies(te_ref[t], s):
            c.wait()

        @pl.when(nxt_ref[t] >= 0)
        def _():
            for c in weight_copies(nxt_ref[t], 1 - s):
                c.start()

        wgb_ref[...] = wgs_ref[s].astype(BF16)
        wub_ref[...] = wus_ref[s].astype(BF16)
        wdb_ref[...] = wds_ref[s].astype(BF16)

    @pl.when(jnp.logical_or(t == 0, t < n_used))
    def _():
        _wait_rows(cnt_ref[t], h_ref, xbuf_ref.at[slot], sems.at[slot])
        xb = xbuf_ref[slot].astype(BF16)
        hg = _silu(_dot(xb, wgb_ref[...])) * _dot(xb, wub_ref[...])
        o_ref[...] = _dot(hg.astype(BF16), wdb_ref[...])

    @pl.when(jnp.logical_and(t > 0, t >= n_used))
    def _():
        o_ref[...] = jnp.zeros(o_ref.shape, F32)


def _moe_experts(sched, h, w_gate, w_up, w_down, layer):
    tm = MOE_TILE
    r = sched[0].shape[0] * tm
    hbm = pl.BlockSpec(memory_space=pl.ANY)
    grid_spec = pltpu.PrefetchScalarGridSpec(
        num_scalar_prefetch=len(sched),
        grid=(r // tm,),
        in_specs=[hbm, hbm, hbm, hbm],
        out_specs=pl.BlockSpec((tm, D_MODEL), lambda t, *_: (t, 0)),
        scratch_shapes=[pltpu.VMEM((2, tm, D_MODEL), F32), pltpu.SemaphoreType.DMA((2,)),
                        pltpu.VMEM((2, D_MODEL, D_EXPERT), F32), pltpu.VMEM((2, D_MODEL, D_EXPERT), F32),
                        pltpu.VMEM((2, D_EXPERT, D_MODEL), F32), pltpu.SemaphoreType.DMA((2, 3)),
                        pltpu.VMEM((D_MODEL, D_EXPERT), BF16), pltpu.VMEM((D_MODEL, D_EXPERT), BF16),
                        pltpu.VMEM((D_EXPERT, D_MODEL), BF16)])
    return pl.pallas_call(
        functools.partial(_moe_kernel, layer=layer),
        grid_spec=grid_spec,
        out_shape=jax.ShapeDtypeStruct((r, D_MODEL), F32),
        compiler_params=_cparams(("arbitrary",), disable_bounds_checks=True),
        name="moe_experts",
    )(*sched, h, w_gate, w_up, w_down)


def _moe_schedule(eid):
    n = eid.shape[0]
    tm = MOE_TILE
    n_tiles = (2 * n + N_EXPERTS * (tm - 1)) // tm + 1
    flat_e = eid.reshape(-1)
    order = jnp.argsort(flat_e, stable=True).astype(jnp.int32)
    rank = jnp.argsort(order).astype(jnp.int32)
    onehot = (flat_e[:, None] == jnp.arange(N_EXPERTS, dtype=jnp.int32)[None, :]).astype(jnp.int32)
    counts = jnp.sum(onehot, axis=0)
    padded = (counts + tm - 1) // tm * tm
    pad_end = jnp.cumsum(padded)
    g_start = pad_end - padded
    u_start = jnp.cumsum(counts) - counts
    pos = (rank + jnp.sum(onehot * (g_start - u_start)[None, :], axis=1)).reshape(n, 2)
    n_used = (pad_end[-1] // tm).astype(jnp.int32)
    tile_id = jnp.arange(n_tiles, dtype=jnp.int32)
    tile_expert = jnp.sum((pad_end[None, :] <= (tile_id * tm)[:, None]).astype(jnp.int32), axis=1)
    tile_expert = jnp.minimum(tile_expert, N_EXPERTS - 1)
    in_tile = (tile_expert[:, None] == jnp.arange(N_EXPERTS, dtype=jnp.int32)[None, :]).astype(jnp.int32)
    pick = lambda table: jnp.sum(in_tile * table[None, :], axis=1)
    off = tile_id * tm - pick(g_start)
    tile_src = (pick(u_start) + off).astype(jnp.int32)
    tile_cnt = jnp.clip(pick(counts) - off, 0, tm).astype(jnp.int32)
    live = tile_id < n_used
    tile_cnt = jnp.where(live, tile_cnt, 0)
    ar = jnp.arange(N_EXPERTS, dtype=jnp.int32)
    used = counts > 0
    nxt_e = jnp.min(jnp.where(jnp.logical_and(ar[None, :] > ar[:, None], used[None, :]), ar[None, :], N_EXPERTS), axis=1)
    nxt_e = jnp.where(nxt_e < N_EXPERTS, nxt_e, -1)
    first = jnp.logical_and(live, off == 0).astype(jnp.int32)
    wslot = (pick(jnp.cumsum(used.astype(jnp.int32)) - 1) % 2).astype(jnp.int32)
    sched = (tile_expert.astype(jnp.int32), n_used.reshape(1), tile_src, tile_cnt, first,
             pick(nxt_e).astype(jnp.int32), wslot, order // 2)
    return sched, pos


def _ln2_kernel(idx_ref, h_ref, rt_ref, ys_ref, g_ref, b_ref, o_ref, ybuf_ref, sems, *, alpha):
    i = pl.program_id(0)
    tm = h_ref.shape[0]
    slot = i % 2

    def fetch(tile, to_slot):
        base = tile * 2 * tm
        _issue_rows(lambda r: idx_ref[base + r], 2 * tm, ys_ref, ybuf_ref.at[to_slot], sems.at[to_slot])

    @pl.when(i == 0)
    def _():
        fetch(0, 0)

    @pl.when(i + 1 < pl.num_programs(0))
    def _():
        fetch(i + 1, 1 - slot)

    _wait_rows(2 * tm, ys_ref, ybuf_ref.at[slot], sems.at[slot])
    w0 = rt_ref[:, 2:3]
    w1 = rt_ref[:, 3:4]
    moe = w0 * ybuf_ref[slot, 0:tm, :] + w1 * ybuf_ref[slot, tm:2 * tm, :]
    o_ref[...] = _layer_norm(alpha * h_ref[...] + moe, g_ref[...], b_ref[...], LN_EPS)


def _combine_ln2(h, routed, pos, ys, g, b, alpha):
    n = h.shape[0]
    tm = min(128, n)
    nt = n // tm
    idx = jnp.transpose(pos.reshape(nt, tm, 2), (0, 2, 1)).reshape(-1)
    grid_spec = pltpu.PrefetchScalarGridSpec(
        num_scalar_prefetch=1,
        grid=(nt,),
        in_specs=[pl.BlockSpec((tm, D_MODEL), lambda i, idx_ref: (i, 0)),
                  pl.BlockSpec((tm, 128), lambda i, idx_ref: (i, 0)),
                  pl.BlockSpec(memory_space=pl.ANY),
                  pl.BlockSpec((1, D_MODEL), lambda i, idx_ref: (0, 0)),
                  pl.BlockSpec((1, D_MODEL), lambda i, idx_ref: (0, 0))],
        out_specs=pl.BlockSpec((tm, D_MODEL), lambda i, idx_ref: (i, 0)),
        scratch_shapes=[pltpu.VMEM((2, 2 * tm, D_MODEL), F32), pltpu.SemaphoreType.DMA((2,))])
    return pl.pallas_call(
        functools.partial(_ln2_kernel, alpha=alpha),
        grid_spec=grid_spec,
        out_shape=jax.ShapeDtypeStruct((n, D_MODEL), F32),
        compiler_params=_cparams(("arbitrary",), disable_bounds_checks=True),
        name="combine_ln2",
    )(idx, h, routed, ys, g, b)


def _relayout_cols(w):
    pad = lambda n: jnp.zeros(w.shape[:-1] + (n,), w.dtype)
    d0 = 3864
    return jnp.concatenate([
        w[..., 0:2328], pad(C_BG - 2328),
        w[..., 2328:3864],
        w[..., d0:d0 + 1536],
        w[..., d0 + 1536:d0 + 1568], pad(96),
        w[..., d0 + 1568:d0 + 1600], pad(96),
        w[..., d0 + 1600:d0 + 1696], pad(N_Z - C_GL - 96)], axis=-1)


def _relayout_d(v):
    pad = lambda n: jnp.zeros(v.shape[:-1] + (n,), v.dtype)
    return jnp.concatenate([v[..., 0:1536], v[..., 1536:1568], pad(96), v[..., 1568:1600], pad(96),
                            v[..., 1600:1696], pad(32)], axis=-1)


def _shift_cols(zrow):
    return jnp.concatenate([zrow[..., C_R:C_R + 1536], zrow[..., C_WL:C_WL + 32], zrow[..., C_AL:C_AL + 32],
                            zrow[..., C_GL:C_GL + 96]], axis=-1)


def _pad_rows(w, rows):
    return jnp.concatenate([w, jnp.zeros((rows - w.shape[0],) + w.shape[1:], w.dtype)], axis=0)


def _layer_params(l, P):
    wc = P["nsa_wc"][l]
    half = 2 * CMP_STRIDE // 2
    wcat = jnp.concatenate([wc[:, :half].reshape(2, half * HEAD_DIM, HEAD_DIM),
                            wc[:, half:].reshape(2, half * HEAD_DIM, HEAD_DIM)], axis=-1)
    pe = P["nsa_pe"][l]
    pe2 = jnp.stack([pe[:, :half].reshape(2, half * HEAD_DIM), pe[:, half:].reshape(2, half * HEAD_DIM)], axis=1)
    pe2 = jnp.concatenate([pe2, jnp.zeros((2, 6, half * HEAD_DIM), F32)], axis=1)
    router_w = jnp.concatenate([P["moe_w_grp"][l], P["moe_w_rte"][l],
                                jnp.zeros((D_MODEL, 128 - N_GROUPS - N_EXPERTS), F32)], axis=1)
    router_b = jnp.concatenate([P["moe_b_grp"][l], P["moe_b_rte"][l],
                                jnp.zeros((128 - N_GROUPS - N_EXPERTS,), F32)])[None, :]
    row = lambda name: P[name][l][None, :]
    return dict(
        w_in=_relayout_cols(P["w_in"][l]).astype(BF16), b_in=_relayout_cols(P["b_in"][l])[None, :],
        w_out=P["w_out"][l].astype(BF16), b_out=row("b_out"),
        ln1_g=row("ln1_g"), ln1_b=row("ln1_b"), ln2_g=row("ln2_g"), ln2_b=row("ln2_b"),
        conv_a_w=P["conv_a_w"][l], conv_a_b=row("conv_a_b"), ln_a_g=row("ln_a_g"), ln_a_b=row("ln_a_b"),
        conv_c_w=P["conv_c_w"][l], wcat=wcat, pe2=pe2,
        rw_mu=_relayout_d(P["rw_mu"][l])[None, :], rw_w0=row("rw_w0"), rw_w2=_pad_rows(P["rw_w2"][l], 128),
        rw_a0=row("rw_a0"), rw_a2=_pad_rows(P["rw_a2"][l], 128), rw_g2=_pad_rows(P["rw_g2"][l], 128),
        rw_kk=row("rw_kk"), rw_ka=row("rw_ka"), rw_rk=P["rw_rk"][l].reshape(1, GW),
        rw_lnx_g=P["rw_lnx_g"][l].reshape(HEADS, HEAD_DIM), rw_lnx_b=P["rw_lnx_b"][l].reshape(HEADS, HEAD_DIM),
        router_w=router_w, router_b=router_b)


def _mix_rwkv(z, shift, s0, lp, t_real):
    B, T, _ = z.shape
    r, w, k, v, kk, b, bonus, g = _rwkv_prep(z, shift, lp["rw_mu"], lp["rw_w0"], lp["rw_w2"], lp["rw_a0"],
                                             lp["rw_a2"], lp["rw_g2"], lp["rw_kk"], lp["rw_ka"], lp["rw_rk"], t_real)
    if t_real == T and T % _RW_CHUNK == 0:
        y, s_new = _rwkv_chunked(r, w, k, v, kk, b, bonus, s0, lp["rw_lnx_g"], lp["rw_lnx_b"])
    else:
        y, s_new = _rwkv_scan(r, w, k, v, kk, b, bonus, s0, lp["rw_lnx_g"], lp["rw_lnx_b"], t_real)
    yd = jnp.transpose(y, (0, 2, 1, 3)).reshape(B, T, GW)
    return yd, g, s_new


def _nsa_prompt_group(z, lp):
    B, T, _ = z.shape
    kv6 = z[:, :, C_ROWS:C_GATES].reshape(B, T, 6, KV_HEADS, HEAD_DIM)
    nc = T // CMP_STRIDE
    cmp_in = kv6[:, :, 0:2].reshape(B, nc, CMP_STRIDE, 2, KV_HEADS, HEAD_DIM)
    cmp_in = jnp.transpose(cmp_in, (0, 3, 4, 1, 2, 5)).reshape(B, 2, KV_HEADS, nc, CMP_STRIDE * HEAD_DIM)
    ckv = _compress_prompt(cmp_in, lp["wcat"], lp["pe2"])
    keyT = lambda x, tile: jnp.transpose(x.astype(BF16).reshape(B, T // tile, tile, KV_HEADS, HEAD_DIM), (0, 3, 1, 4, 2))
    ones_col = jnp.concatenate([jnp.ones((B, KV_HEADS, T, 1), BF16), jnp.zeros((B, KV_HEADS, T, HEAD_DIM - 1), BF16)], axis=-1)
    val = lambda x: jnp.concatenate([jnp.transpose(x.astype(BF16), (0, 2, 1, 3)), ones_col], axis=-1)
    yb = _nsa_prompt(z, ckv, keyT(kv6[:, :, 2], _KT), val(kv6[:, :, 3]), keyT(kv6[:, :, 4], Q_BLOCK), val(kv6[:, :, 5]))
    rows = kv6[:, :, 0:4]
    win = kv6[:, T - min(WINDOW, T):, 4:6]
    return yb, rows, win


def _cache_views(cache_nsa, cache_win):
    L, n_pool = cache_nsa.shape[0], cache_nsa.shape[1]
    cpp = PAGE // CMP_STRIDE
    cache_t = jnp.transpose(cache_nsa, (0, 1, 3, 4, 5, 2))
    win_t = jnp.transpose(cache_win, (0, 1, 3, 4, 5, 2))
    cmp_pages = cache_t[:, :, 0:2].reshape(L, n_pool, 2, KV_HEADS, HEAD_DIM, cpp, CMP_STRIDE)
    cmp_pages = jnp.transpose(cmp_pages, (0, 1, 2, 3, 5, 6, 4)).reshape(L, n_pool, 2 * KV_HEADS, cpp, CMP_STRIDE * HEAD_DIM)
    return cache_t, win_t, cmp_pages


def _nsa_sample_group(z, views, layer, page_table, win_l, lp, t_real):
    B, t_pad, _ = z.shape
    cache_t, win_t, cmp_pages = views
    past_len = page_table.shape[1] * PAGE
    ocmp, selm = _nsa_sample_a(page_table, cmp_pages, layer, lp["wcat"], lp["pe2"], z, past_len)
    yb = _nsa_sample_b(page_table, cache_t, win_t, layer, z, selm, ocmp, past_len, t_real)
    zr = z[:, :t_real]
    rows = zr[:, :, C_ROWS:C_KW].reshape(B, t_real, 4, KV_HEADS, HEAD_DIM)
    new_win = zr[:, :, C_KW:C_GATES].reshape(B, t_real, 2, KV_HEADS, HEAD_DIM)
    win = jnp.concatenate([win_l, new_win], axis=1)[:, t_real:]
    return yb, rows, win


def _group_layer(x, lp, alpha, nsa_fn, conv_a_buf, conv_c_buf, rw_state, rw_shift, t_real):
    B, T, _ = x.shape
    z = _proj_in(x.reshape(B * T, D_MODEL), lp["w_in"], lp["b_in"]).reshape(B, T, N_Z)
    ya, yc, conv_a_new, conv_c_new = _conv_mix(z, conv_a_buf, conv_c_buf, lp["conv_a_w"], lp["conv_a_b"],
                                               lp["ln_a_g"], lp["ln_a_b"], lp["conv_c_w"], t_real)
    yb, nsa_rows, win_new = nsa_fn(z)
    yd, g, rw_state_new = _mix_rwkv(z, _relayout_d(rw_shift)[:, None, :], rw_state, lp, t_real)
    flat = lambda a: a.reshape(B * T, a.shape[-1])
    h, routed = _proj_out(flat(x), flat(ya), flat(yb), flat(yc), flat(yd), flat(g), lp, alpha)
    shift_new = _shift_cols(z[:, t_real - 1])
    return (h, routed), (nsa_rows, win_new, conv_a_new, conv_c_new, rw_state_new, shift_new)


def _moe_ln2(groups, lp, P, layer, alpha):
    h_list = [g[0] for g in groups]
    routed = [g[1] for g in groups]
    eid = jnp.concatenate([r[:, 0:2] for r in routed], axis=0).astype(jnp.int32)
    h_all = jnp.concatenate(h_list, axis=0)
    sched, pos = _moe_schedule(eid)
    ys = _moe_experts(sched, h_all, P["moe_w_gate"], P["moe_w_up"], P["moe_w_down"], layer)
    outs = []
    off = 0
    for h, rt in zip(h_list, routed):
        n = h.shape[0]
        outs.append(_combine_ln2(h, rt, pos[off:off + n], ys, lp["ln2_g"], lp["ln2_b"], alpha))
        off += n
    return outs


def kernel(x_prompt, x_sample, cache_nsa, cache_win, state_conv_a, state_conv_c, state_rwkv, state_rwkv_shift, page_table, w_in, b_in, w_out, b_out, ln1_g, ln1_b, ln2_g, ln2_b, conv_a_w, conv_a_b, ln_a_g, ln_a_b, conv_c_w, nsa_wc, nsa_pe, rw_mu, rw_w0, rw_w2, rw_a0, rw_a2, rw_g2, rw_kk, rw_ka, rw_rk, rw_lnx_g, rw_lnx_b, moe_w_grp, moe_b_grp, moe_w_rte, moe_b_rte, moe_w_gate, moe_w_up, moe_w_down):
    P = dict(w_in=w_in, b_in=b_in, w_out=w_out, b_out=b_out, ln1_g=ln1_g, ln1_b=ln1_b, ln2_g=ln2_g, ln2_b=ln2_b,
             conv_a_w=conv_a_w, conv_a_b=conv_a_b, ln_a_g=ln_a_g, ln_a_b=ln_a_b, conv_c_w=conv_c_w,
             nsa_wc=nsa_wc, nsa_pe=nsa_pe, rw_mu=rw_mu, rw_w0=rw_w0, rw_w2=rw_w2, rw_a0=rw_a0, rw_a2=rw_a2,
             rw_g2=rw_g2, rw_kk=rw_kk, rw_ka=rw_ka, rw_rk=rw_rk, rw_lnx_g=rw_lnx_g, rw_lnx_b=rw_lnx_b,
             moe_w_grp=moe_w_grp, moe_b_grp=moe_b_grp, moe_w_rte=moe_w_rte, moe_b_rte=moe_b_rte,
             moe_w_gate=moe_w_gate, moe_w_up=moe_w_up, moe_w_down=moe_w_down)
    depth = w_in.shape[0]
    alpha = (2.0 * depth) ** 0.25
    bp, tp, _ = x_prompt.shape
    bs, ts, _ = x_sample.shape
    ts_pad = 8
    xp = x_prompt
    xs = jnp.concatenate([x_sample, jnp.zeros((bs, ts_pad - ts, D_MODEL), F32)], axis=1)
    outs_p, outs_s = [], []
    views = _cache_views(cache_nsa, cache_win)
    for l in range(depth):
        lp = _layer_params(l, P)
        h_p, st_p = _group_layer(xp, lp, alpha, functools.partial(_nsa_prompt_group, lp=lp),
                                 jnp.zeros((bp, CONV_A_WIDTH - 1, GW), F32), jnp.zeros((bp, SHORT_CONV_WIDTH - 1, GW), F32),
                                 jnp.zeros((bp, HEADS, HEAD_DIM, HEAD_DIM), F32), jnp.zeros((bp, N_COLS_D), F32), tp)
        h_s, st_s = _group_layer(xs, lp, alpha,
                                 functools.partial(_nsa_sample_group, views=views, layer=l, page_table=page_table,
                                                   win_l=cache_win[l], lp=lp, t_real=ts),
                                 state_conv_a[l], state_conv_c[l], state_rwkv[l], state_rwkv_shift[l], ts)
        o_p, o_s = _moe_ln2([h_p, h_s], lp, P, l, alpha)
        xp = o_p.reshape(bp, tp, D_MODEL)
        xs = o_s.reshape(bs, ts_pad, D_MODEL)
        outs_p.append(st_p)
        outs_s.append(st_s)
    stk = lambda outs, i: jnp.stack([o[i] for o in outs], axis=0)
    return (xp, xs[:, :ts], stk(outs_p, 0), stk(outs_s, 0), stk(outs_p, 1), stk(outs_s, 1), stk(outs_p, 2), stk(outs_s, 2),
            stk(outs_p, 3), stk(outs_s, 3), stk(outs_p, 4), stk(outs_s, 4), stk(outs_p, 5), stk(outs_s, 5))
```

```python
import functools

import numpy as np
import jax
import jax.numpy as jnp
from jax import lax
from jax.experimental import pallas as pl
from jax.experimental.pallas import tpu as pltpu

F32 = jnp.float32
BF16 = jnp.bfloat16

D_MODEL = 2048
DEPTH = 2
HEAD_DIM = 64
GW = 512
HEADS = 8
CONV_A_WIDTH = 31
SHORT_CONV_WIDTH = 3
KV_HEADS = 2
Q_PER_KV = 4
CMP_STRIDE = 16
SLC_BLOCK = 64
N_SLC = 16
WINDOW = 512
Q_BLOCK = 128
PAGE = 128
RWKV_GN_EPS = 64e-5
LN_EPS = 1e-5
NEG = -1e30
FORCED = 1e4
N_EXPERTS = 32
EXP_PER_GROUP = 8
N_GROUPS = 4
D_EXPERT = 512
N_IN = 5560
N_COLS_D = 1696

N_Z = 6144
C_VAL, C_GATE = 0, 512
C_Q, C_ROWS, C_KW, C_GATES = 1024, 1536, 2048, 2304
C_BG, C_CG, C_H = 2560, 3072, 3584
C_R, C_K, C_V = 4096, 4608, 5120
C_WL, C_AL, C_GL = 5632, 5760, 5888

VMEM_LIMIT = 56 * 1024 * 1024
MOE_TILE = 256


def _cparams(sem, vmem=VMEM_LIMIT, disable_bounds_checks=False):
    return pltpu.CompilerParams(dimension_semantics=sem, vmem_limit_bytes=vmem,
                                disable_bounds_checks=disable_bounds_checks)


def _const_spec(shape):
    nd = len(shape)
    return pl.BlockSpec(shape, lambda *_: (0,) * nd)


def _split2(x):
    hi = x.astype(BF16)
    lo = (x - hi.astype(F32)).astype(BF16)
    return hi, lo


def _split3(x):
    hi = x.astype(BF16)
    r1 = x - hi.astype(F32)
    mid = r1.astype(BF16)
    lo = (r1 - mid.astype(F32)).astype(BF16)
    return hi, mid, lo


_NN = (((1,), (0,)), ((), ()))
_NT = (((1,), (1,)), ((), ()))


def _dot(a, b, dims=_NN):
    return lax.dot_general(a, b, dims, preferred_element_type=F32)


def _dot3(a, b, dims=_NN):
    ah, al = _split2(a)
    bh, bl = _split2(b)
    return _dot(ah, bh, dims) + _dot(al, bh, dims) + _dot(ah, bl, dims)


def _dot_exact_rhs(a, b_bf16, dims=_NN):
    hi, mid, lo = _split3(a)
    return _dot(hi, b_bf16, dims) + _dot(mid, b_bf16, dims) + _dot(lo, b_bf16, dims)


def _dot_exact_lhs(a_bf16, b, dims=_NN):
    hi, mid, lo = _split3(b)
    return _dot(a_bf16, hi, dims) + _dot(a_bf16, mid, dims) + _dot(a_bf16, lo, dims)


def _sigmoid(x):
    return 1.0 / (1.0 + jnp.exp(-x))


def _silu(x):
    return x * _sigmoid(x)


def _layer_norm(x, g, b, eps):
    mu = jnp.mean(x, axis=-1, keepdims=True)
    xc = x - mu
    var = jnp.mean(xc * xc, axis=-1, keepdims=True)
    return xc * lax.rsqrt(var + eps) * g + b


def _masked_softmax(s, maskf):
    s = jnp.where(maskf > 0.5, s, NEG)
    e = jnp.exp(s - jnp.max(s, axis=-1, keepdims=True)) * maskf
    return e / jnp.maximum(jnp.sum(e, axis=-1, keepdims=True), 1e-30)


def _proj_in_kernel(x_ref, w_ref, b_ref, o_ref, xb_ref):
    @pl.when(pl.program_id(1) == 0)
    def _():
        xb_ref[...] = x_ref[...].astype(BF16)

    o_ref[...] = _dot(xb_ref[...], w_ref[...]) + b_ref[...]


def _proj_in(x2d, w_bf16, b_row):
    n = x2d.shape[0]
    tm = min(1024, n)
    tn = 512
    return pl.pallas_call(
        _proj_in_kernel,
        grid=(n // tm, N_Z // tn),
        in_specs=[pl.BlockSpec((tm, D_MODEL), lambda i, j: (i, 0)),
                  pl.BlockSpec((D_MODEL, tn), lambda i, j: (0, j)),
                  pl.BlockSpec((1, tn), lambda i, j: (0, j))],
        out_specs=pl.BlockSpec((tm, tn), lambda i, j: (i, j)),
        out_shape=jax.ShapeDtypeStruct((n, N_Z), F32),
        scratch_shapes=[pltpu.VMEM((tm, D_MODEL), BF16)],
        compiler_params=_cparams(("parallel", "arbitrary")),
        name="proj_in",
    )(x2d, w_bf16, b_row)


_CONV_ROWS = 32


def _conv_kernel(val_ref, gate_ref, bg_ref, cg_ref, h_ref, bufa_ref, bufc_ref, wa_ref, ba_ref, lag_ref,
                 lab_ref, wc_ref, ya_ref, yc_ref, na_ref, nc_ref, ea_ref, ec_ref, *, tt, t_last):
    it = pl.program_id(1)
    nt = pl.num_programs(1)
    ha = 32
    hc = 8

    @pl.when(it == 0)
    def _():
        ea_ref[0:ha, :] = jnp.zeros((ha, GW), F32)
        ea_ref[ha - (CONV_A_WIDTH - 1):ha, :] = bufa_ref[0]
        ec_ref[0:hc, :] = jnp.zeros((hc, GW), F32)
        ec_ref[hc - (SHORT_CONV_WIDTH - 1):hc, :] = bufc_ref[0]

    @pl.when(it > 0)
    def _():
        ea_ref[0:ha, :] = ea_ref[tt:tt + ha, :]
        ec_ref[0:hc, :] = ec_ref[tt:tt + hc, :]

    ea_ref[ha:ha + tt, :] = val_ref[0] * _sigmoid(gate_ref[0])
    ec_ref[hc:hc + tt, :] = cg_ref[0] * h_ref[0]

    rc = min(_CONV_ROWS, tt)
    offa = ha - (CONV_A_WIDTH - 1)
    offc = hc - (SHORT_CONV_WIDTH - 1)
    for c in range(tt // rc):
        r0 = c * rc
        acc = jnp.zeros((rc, GW), F32)
        for b in range(8):
            rows_b = rc if b == 0 else rc + 8
            part = None
            for a in range((offa + CONV_A_WIDTH - 1) // 8 + 1):
                j = 8 * a + b - offa
                if 0 <= j < CONV_A_WIDTH:
                    assert r0 + 8 * a + rows_b <= tt + ha
                    term = wa_ref[j:j + 1, :] * ea_ref[r0 + 8 * a:r0 + 8 * a + rows_b, :]
                    part = term if part is None else part + term
            acc = acc + part[b:b + rc]
        y = _layer_norm(acc + ba_ref[...], lag_ref[...], lab_ref[...], LN_EPS)
        ya_ref[0, r0:r0 + rc, :] = _silu(y)
        accc = jnp.zeros((rc, GW), F32)
        for j in range(SHORT_CONV_WIDTH):
            accc = accc + wc_ref[j:j + 1, :] * ec_ref[r0 + offc + j:r0 + offc + j + rc, :]
        yc_ref[0, r0:r0 + rc, :] = bg_ref[0, r0:r0 + rc, :] * accc

    @pl.when(it == nt - 1)
    def _():
        na_ref[0] = ea_ref[t_last + offa:t_last + ha, :]
        nc_ref[0] = ec_ref[t_last + offc:t_last + hc, :]


def _conv_mix(z, buf_a, buf_c, wa, ba, lag, lab, wc, t_real):
    B, T, _ = z.shape
    tt = min(256, T)
    nt = T // tt
    t_last = t_real - (nt - 1) * tt
    zspec = lambda col: pl.BlockSpec((1, tt, GW), lambda b, i: (b, i, col // GW))
    return pl.pallas_call(
        functools.partial(_conv_kernel, tt=tt, t_last=t_last),
        grid=(B, nt),
        in_specs=[zspec(C_VAL), zspec(C_GATE), zspec(C_BG), zspec(C_CG), zspec(C_H),
                  pl.BlockSpec((1, CONV_A_WIDTH - 1, GW), lambda b, i: (b, 0, 0)),
                  pl.BlockSpec((1, SHORT_CONV_WIDTH - 1, GW), lambda b, i: (b, 0, 0)),
                  _const_spec((CONV_A_WIDTH, GW)), _const_spec((1, GW)), _const_spec((1, GW)),
                  _const_spec((1, GW)), _const_spec((SHORT_CONV_WIDTH, GW))],
        out_specs=[pl.BlockSpec((1, tt, GW), lambda b, i: (b, i, 0)),
                   pl.BlockSpec((1, tt, GW), lambda b, i: (b, i, 0)),
                   pl.BlockSpec((1, CONV_A_WIDTH - 1, GW), lambda b, i: (b, 0, 0)),
                   pl.BlockSpec((1, SHORT_CONV_WIDTH - 1, GW), lambda b, i: (b, 0, 0))],
        out_shape=[jax.ShapeDtypeStruct((B, T, GW), F32), jax.ShapeDtypeStruct((B, T, GW), F32),
                   jax.ShapeDtypeStruct((B, CONV_A_WIDTH - 1, GW), F32),
                   jax.ShapeDtypeStruct((B, SHORT_CONV_WIDTH - 1, GW), F32)],
        scratch_shapes=[pltpu.VMEM((tt + 32, GW), F32), pltpu.VMEM((tt + 8, GW), F32)],
        compiler_params=_cparams(("parallel", "arbitrary")),
        name="conv_mix",
    )(z, z, z, z, z, buf_a, buf_c, wa, ba, lag, lab, wc)


def _rwkv_prep_kernel(r_ref, k_ref, v_ref, wl_ref, al_ref, gl_ref, sh_ref, mu_ref, w0_ref, w2_ref, a0_ref,
                      a2_ref, g2_ref, kkp_ref, kap_ref, rk_ref,
                      ro_ref, wo_ref, ko_ref, vo_ref, kko_ref, bo_ref, bonus_ref, g_ref, carry_ref, *, tt, t_last):
    it = pl.program_id(1)

    @pl.when(it == 0)
    def _():
        carry_ref[...] = sh_ref[0]

    row = lax.broadcasted_iota(jnp.int32, (tt, 1), 0)

    def mixed(x, c0):
        w = x.shape[1]
        prev = pltpu.roll(x, 1, 0) if tt > 1 else x
        prev = jnp.where(row == 0, carry_ref[:, c0:c0 + w], prev)
        return x + (prev - x) * mu_ref[:, c0:c0 + w]

    xr, xk, xv = r_ref[0], k_ref[0], v_ref[0]
    xwl, xal, xgl = wl_ref[0], al_ref[0], gl_ref[0]
    r = mixed(xr, 0)
    k = mixed(xk, GW)
    v = mixed(xv, 2 * GW)
    wl = mixed(xwl, 3 * GW)
    al = mixed(xal, 3 * GW + 128)
    gl = mixed(xgl, 3 * GW + 256)
    lr = t_last - 1
    carry_ref[:, 0:GW] = xr[lr:lr + 1]
    carry_ref[:, GW:2 * GW] = xk[lr:lr + 1]
    carry_ref[:, 2 * GW:3 * GW] = xv[lr:lr + 1]
    carry_ref[:, 3 * GW:3 * GW + 128] = xwl[lr:lr + 1]
    carry_ref[:, 3 * GW + 128:3 * GW + 256] = xal[lr:lr + 1]
    carry_ref[:, 3 * GW + 256:3 * GW + 384] = xgl[lr:lr + 1]

    y = w0_ref[...] + _dot3(jnp.tanh(wl), w2_ref[...])
    u = -y
    w_log = -(jnp.maximum(u, 0.0) + jnp.log(1.0 + jnp.exp(-jnp.abs(u)))) - 0.5
    log_decay = -jnp.exp(w_log)
    a = _sigmoid(a0_ref[...] + _dot3(al, a2_ref[...]))
    g_ref[0] = _dot3(_sigmoid(gl), g2_ref[...])
    kk = k * kkp_ref[...]
    k2 = k * (1.0 + (a - 1.0) * kap_ref[...])
    rkk = r * k2 * rk_ref[...]
    for h in range(HEADS):
        sl = slice(h * HEAD_DIM, (h + 1) * HEAD_DIM)
        kk_h = kk[:, sl]
        nrm = jnp.sqrt(jnp.sum(kk_h * kk_h, axis=-1, keepdims=True))
        kk_h = kk_h / jnp.maximum(nrm, 1e-12)
        v_h = v[:, sl]
        ro_ref[0, h] = r[:, sl]
        wo_ref[0, h] = log_decay[:, sl]
        ko_ref[0, h] = k2[:, sl]
        vo_ref[0, h] = v_h
        kko_ref[0, h] = kk_h
        bo_ref[0, h] = kk_h * a[:, sl]
        bonus_ref[0, h] = jnp.sum(rkk[:, sl], axis=-1, keepdims=True) * v_h


def _rwkv_prep(z, shift, mu, w0, w2p, a0, a2p, g2p, kkp, kap, rk, t_real):
    B, T, _ = z.shape
    tt = min(256, T)
    nt = T // tt
    t_last = t_real - (nt - 1) * tt
    wide = lambda col: pl.BlockSpec((1, tt, GW), lambda b, i: (b, i, col // GW))
    thin = lambda col: pl.BlockSpec((1, tt, 128), lambda b, i: (b, i, col // 128))
    head_out = pl.BlockSpec((1, HEADS, tt, HEAD_DIM), lambda b, i: (b, 0, i, 0))
    head_shape = jax.ShapeDtypeStruct((B, HEADS, T, HEAD_DIM), F32)
    dw = 3 * GW + 384
    return pl.pallas_call(
        functools.partial(_rwkv_prep_kernel, tt=tt, t_last=t_last),
        grid=(B, nt),
        in_specs=[wide(C_R), wide(C_K), wide(C_V), thin(C_WL), thin(C_AL), thin(C_GL),
                  pl.BlockSpec((1, 1, dw), lambda b, i: (b, 0, 0)),
                  _const_spec((1, dw)), _const_spec((1, GW)), _const_spec((128, GW)), _const_spec((1, GW)),
                  _const_spec((128, GW)), _const_spec((128, GW)), _const_spec((1, GW)), _const_spec((1, GW)),
                  _const_spec((1, GW))],
        out_specs=[head_out] * 7 + [pl.BlockSpec((1, tt, GW), lambda b, i: (b, i, 0))],
        out_shape=[head_shape] * 7 + [jax.ShapeDtypeStruct((B, T, GW), F32)],
        scratch_shapes=[pltpu.VMEM((1, dw), F32)],
        compiler_params=_cparams(("parallel", "arbitrary")),
        name="rwkv_prep",
    )(z, z, z, z, z, z, shift, mu, w0, w2p, a0, a2p, g2p, kkp, kap, rk)


def _rwkv_scan_kernel(r_ref, w_ref, k_ref, v_ref, kk_ref, b_ref, bonus_ref, s0_ref, lg_ref, lb_ref,
                      y_ref, s_ref, *, nb, tt, n_steps):
    it = pl.program_id(1)

    @pl.when(it == 0)
    def _():
        s_ref[...] = s0_ref[...]

    y_ref[...] = jnp.zeros(y_ref.shape, F32)
    ii = lax.broadcasted_iota(jnp.int32, (HEAD_DIM, HEAD_DIM), 0)
    jj = lax.broadcasted_iota(jnp.int32, (HEAD_DIM, HEAD_DIM), 1)
    eye = jnp.where(ii == jj, 1.0, 0.0).astype(F32)

    def step(t, carry):
        for n in range(nb):
            for h in range(HEADS):
                row = lambda ref: ref[n, h, pl.ds(t, 1), :]
                S = s_ref[n, h]
                sa = jnp.sum(S * row(kk_ref), axis=-1, keepdims=True)
                vcol = jnp.sum(eye * row(v_ref), axis=-1, keepdims=True)
                S = S * jnp.exp(row(w_ref)) - sa * row(b_ref) + vcol * row(k_ref)
                s_ref[n, h] = S
                ycol = jnp.sum(S * row(r_ref), axis=-1, keepdims=True)
                y_ref[n, h, pl.ds(t, 1), :] = jnp.sum(eye * ycol, axis=0, keepdims=True)
        return carry

    lax.fori_loop(0, n_steps, step, 0)

    for n in range(nb):
        for h in range(HEADS):
            y = y_ref[n, h]
            mu = jnp.mean(y, axis=-1, keepdims=True)
            yc = y - mu
            var = jnp.mean(yc * yc, axis=-1, keepdims=True)
            y_ref[n, h] = yc * lax.rsqrt(var + RWKV_GN_EPS) * lg_ref[h:h + 1, :] + lb_ref[h:h + 1, :] + bonus_ref[n, h]


def _rwkv_scan(r, w, k, v, kk, b, bonus, s0, lnx_g, lnx_b, t_real):
    B, H, T, _ = r.shape
    nb = 2
    tt = min(128, T)
    nt = T // tt
    n_steps = min(tt, t_real)
    blk = pl.BlockSpec((nb, H, tt, HEAD_DIM), lambda b_, i: (b_, 0, i, 0))
    sblk = pl.BlockSpec((nb, H, HEAD_DIM, HEAD_DIM), lambda b_, i: (b_, 0, 0, 0))
    return pl.pallas_call(
        functools.partial(_rwkv_scan_kernel, nb=nb, tt=tt, n_steps=n_steps),
        grid=(B // nb, nt),
        in_specs=[blk] * 7 + [sblk, _const_spec((H, HEAD_DIM)), _const_spec((H, HEAD_DIM))],
        out_specs=[blk, sblk],
        out_shape=[jax.ShapeDtypeStruct((B, H, T, HEAD_DIM), F32),
                   jax.ShapeDtypeStruct((B, H, HEAD_DIM, HEAD_DIM), F32)],
        compiler_params=_cparams(("parallel", "arbitrary")),
        name="rwkv_scan",
    )(r, w, k, v, kk, b, bonus, s0, lnx_g, lnx_b)


_RW_CHUNK = 64

_TN = (((0,), (0,)), ((), ()))


def _rwkv_chunk_kernel(r_ref, lw_ref, k_ref, v_ref, kk_ref, b_ref, bonus_ref, s0_ref, lg_ref, lb_ref, y_ref, s_ref):
    C = r_ref.shape[2]
    it = pl.program_id(1)

    @pl.when(it == 0)
    def _():
        s_ref[...] = s0_ref[...]

    ti = lax.broadcasted_iota(jnp.int32, (C, C), 0)
    si = lax.broadcasted_iota(jnp.int32, (C, C), 1)
    incl = ti >= si
    strict = ti > si
    tri_b = jnp.where(incl, 1.0, 0.0).astype(BF16)
    n_fac = C.bit_length() - 1
    hs = range(HEADS)
    v = [v_ref[0, h] for h in hs]
    cs = [_dot_exact_lhs(tri_b, lw_ref[0, h]) for h in hs]
    cs_last = [c[C - 1:C, :] for c in cs]
    AR = [jnp.concatenate([kk_ref[0, h] * jnp.exp(cs[h] - lw_ref[0, h]), r_ref[0, h] * jnp.exp(cs[h])], axis=0)
          for h in hs]
    BK = []
    for h in hs:
        g_inv = jnp.exp(-cs[h])
        BK.append(jnp.concatenate([b_ref[0, h] * g_inv, k_ref[0, h] * g_inv], axis=0))
    G = [_dot3(AR[h], BK[h], _NT) for h in hs]
    S0 = [s_ref[0, h] for h in hs]
    ARS = [_dot3(AR[h], S0[h], _NT) for h in hs]
    LMV = [_dot3(jnp.concatenate([jnp.where(strict, G[h][0:C, C:2 * C], 0.0),
                                  jnp.where(incl, G[h][C:2 * C, C:2 * C], 0.0)], axis=0), v[h]) for h in hs]
    X = [ARS[h][0:C] + LMV[h][0:C] for h in hs]
    N = [-jnp.where(strict, G[h][0:C, 0:C], 0.0) for h in hs]
    for f in range(n_fac):
        X = [X[h] + _dot3(N[h], X[h]) for h in hs]
        if f + 1 < n_fac:
            N = [_dot3(N[h], N[h]) for h in hs]
    Y = [ARS[h][C:2 * C] + LMV[h][C:2 * C] - _dot3(jnp.where(incl, G[h][C:2 * C, 0:C], 0.0), X[h]) for h in hs]
    for h in hs:
        g_end = jnp.exp(cs_last[h] - cs[h])
        UV = jnp.concatenate([-X[h], v[h]], axis=0)
        BKg = jnp.concatenate([b_ref[0, h] * g_end, k_ref[0, h] * g_end], axis=0)
        s_ref[0, h] = S0[h] * jnp.exp(cs_last[h]) + _dot3(UV, BKg, _TN)
    for h in hs:
        mu = jnp.mean(Y[h], axis=-1, keepdims=True)
        yc = Y[h] - mu
        var = jnp.mean(yc * yc, axis=-1, keepdims=True)
        y_ref[0, h] = yc * lax.rsqrt(var + RWKV_GN_EPS) * lg_ref[h:h + 1, :] + lb_ref[h:h + 1, :] + bonus_ref[0, h]


def _rwkv_chunked(r, lw, k, v, kk, b, bonus, s0, lnx_g, lnx_b):
    B, H, T, _ = r.shape
    C = _RW_CHUNK
    blk = pl.BlockSpec((1, H, C, HEAD_DIM), lambda b_, i: (b_, 0, i, 0))
    sblk = pl.BlockSpec((1, H, HEAD_DIM, HEAD_DIM), lambda b_, i: (b_, 0, 0, 0))
    return pl.pallas_call(
        _rwkv_chunk_kernel,
        grid=(B, T // C),
        in_specs=[blk] * 7 + [sblk, _const_spec((H, HEAD_DIM)), _const_spec((H, HEAD_DIM))],
        out_specs=[blk, sblk],
        out_shape=[jax.ShapeDtypeStruct((B, H, T, HEAD_DIM), F32),
                   jax.ShapeDtypeStruct((B, H, HEAD_DIM, HEAD_DIM), F32)],
        compiler_params=_cparams(("parallel", "arbitrary")),
        name="rwkv_chunked",
    )(r, lw, k, v, kk, b, bonus, s0, lnx_g, lnx_b)


def _compress_tail(F, cst):
    n = F.shape[0]
    nxt = pltpu.roll(F[:, HEAD_DIM:2 * HEAD_DIM], n - 1, 0)
    const = cst[0:1, 0:HEAD_DIM] + cst[1:2, HEAD_DIM:2 * HEAD_DIM]
    return F[:, 0:HEAD_DIM] + nxt + const


def _compress_kernel(x_ref, w_ref, pe_ref, o_ref):
    F = _dot3(x_ref[0, 0, 0], w_ref[0])
    cst = _dot3(pe_ref[0], w_ref[0])
    o_ref[0, 0, 0] = _compress_tail(F, cst)


def _compress_prompt(cmp_in, wcat, pe2):
    B, _, _, nc, _ = cmp_in.shape
    return pl.pallas_call(
        _compress_kernel,
        grid=(B, 2, 2),
        in_specs=[pl.BlockSpec((1, 1, 1, nc, 1024), lambda b, a, g: (b, a, g, 0, 0)),
                  pl.BlockSpec((1, 1024, 128), lambda b, a, g: (a, 0, 0)),
                  pl.BlockSpec((1, 8, 1024), lambda b, a, g: (a, 0, 0))],
        out_specs=pl.BlockSpec((1, 1, 1, nc, HEAD_DIM), lambda b, a, g: (b, a, g, 0, 0)),
        out_shape=jax.ShapeDtypeStruct((B, 2, 2, nc, HEAD_DIM), F32),
        compiler_params=_cparams(("parallel", "parallel", "parallel")),
        name="nsa_compress",
    )(cmp_in, wcat, pe2)


def _top_select(selT, j_f):
    v = selT
    sel = jnp.zeros(selT.shape, F32)
    for _ in range(N_SLC):
        m = jnp.max(v, axis=0, keepdims=True)
        first = jnp.min(jnp.where(v == m, j_f, 1e9), axis=0, keepdims=True)
        hit = j_f == first
        sel = jnp.where(hit, 1.0, sel)
        v = jnp.where(hit, -3e38, v)
    return sel


def _gated_heads(gts, g, o_cmp, o_slc, o_win, rows):
    out = []
    for m in range(Q_PER_KV):
        c0 = g * 3 * Q_PER_KV + m * 3
        sl = slice(m * rows, (m + 1) * rows)
        out.append(gts[:, c0:c0 + 1] * o_cmp[sl] + gts[:, c0 + 1:c0 + 2] * o_slc[sl] + gts[:, c0 + 2:c0 + 3] * o_win[sl])
    return out


_KT = 512


_LOG2E = 1.4426950408889634


def _nsa_prompt_kernel(qT_ref, gT_ref, ckv_ref, ks_ref, vsT_ref, kw_ref, vwT_ref, mt_ref, oT_ref, *, T):
    i = pl.program_id(1)
    start = i * Q_BLOCK
    nc = T // CMP_STRIDE
    ns = T // SLC_BLOCK
    cols = Q_PER_KV * Q_BLOCK
    gs = range(KV_HEADS)
    t_lane = start + lax.broadcasted_iota(jnp.int32, (1, Q_BLOCK), 1)
    t_cols = jnp.concatenate([t_lane] * Q_PER_KV, axis=1)
    gts = _sigmoid(gT_ref[0])
    j_col = lax.broadcasted_iota(jnp.int32, (ns, 1), 0)
    j_f = j_col.astype(F32)
    cur = t_lane >> 6
    forced = jnp.where(j_col == 0, 1.0, jnp.where(j_col == cur, 1.0, jnp.where(j_col == cur - 1, 1.0, 0.0)))
    QT = [jnp.concatenate([qT_ref[0, (g * Q_PER_KV + m) * HEAD_DIM:(g * Q_PER_KV + m + 1) * HEAD_DIM, :]
                           for m in range(Q_PER_KV)], axis=1) * (HEAD_DIM ** -0.5 * _LOG2E) for g in gs]
    normed = lambda a: a[0:HEAD_DIM] * (1.0 / a[HEAD_DIM:HEAD_DIM + 1])

    n_col = lax.broadcasted_iota(jnp.int32, (nc, 1), 0)
    bias_c = jnp.where(n_col * CMP_STRIDE + (2 * CMP_STRIDE - 1) <= t_cols, 0.0, NEG)
    any_c = jnp.where(t_cols >= 2 * CMP_STRIDE - 1, 1.0, 0.0)
    pcT = []
    for g in gs:
        s = _dot3(ckv_ref[0, 0, g], QT[g]) + bias_c
        e = jnp.exp2(s - jnp.max(s, axis=0, keepdims=True))
        pcT.append(e * (any_c / jnp.sum(e, axis=0, keepdims=True)))
    o_cmp = [_dot(ckv_ref[0, 1, g].astype(BF16), pcT[g].astype(BF16), _TN) for g in gs]
    rhs = []
    for g in gs:
        psT = pcT[g][:, 0:Q_BLOCK]
        for m in range(1, Q_PER_KV):
            psT = psT + pcT[g][:, m * Q_BLOCK:(m + 1) * Q_BLOCK]
        selT = _dot_exact_lhs(mt_ref[...], psT)
        selT = jnp.where(forced > 0.5, FORCED, jnp.where(j_col * SLC_BLOCK <= t_lane, selT, -FORCED))
        selbT = jnp.where(_top_select(selT, j_f) > 0.5, 0.0, NEG).astype(BF16)
        rhs.append(jnp.concatenate([QT[g].astype(BF16), jnp.concatenate([selbT] * Q_PER_KV, axis=1)], axis=0))

    def key_tile(kt, carry, causal):
        if causal:
            pos = kt * _KT + lax.broadcasted_iota(jnp.int32, (_KT, 1), 0)
            bias_t = jnp.where(pos <= t_cols, 0.0, NEG)
        out = []
        for g in gs:
            m_run, acc = carry[g]
            s = _dot(ks_ref[0, g, pl.ds(pl.multiple_of(kt * _KT, _KT), _KT), :], rhs[g])
            if causal:
                s = s + bias_t
            m_new = jnp.maximum(m_run, jnp.max(s, axis=0, keepdims=True))
            p = jnp.exp2(s - m_new)
            out.append((m_new, jnp.exp2(m_run - m_new) * acc + _dot(vsT_ref[0, g, kt], p.astype(BF16))))
        return tuple(out)

    init = tuple((jnp.full((1, cols), NEG, F32), jnp.zeros((2 * HEAD_DIM, cols), F32)) for _ in gs)
    last = start // _KT
    carry = lax.fori_loop(0, last, lambda kt, c: key_tile(kt, c, False), init)
    carry = key_tile(last, carry, True)
    o_slc = [normed(carry[g][1]) for g in gs]

    nwt = WINDOW // Q_BLOCK + 1
    wb = jnp.maximum(i - WINDOW // Q_BLOCK, 0)
    dt = t_cols - (wb * Q_BLOCK + lax.broadcasted_iota(jnp.int32, (nwt * Q_BLOCK, 1), 0))
    bias_w = jnp.where(dt >= 0, jnp.where(dt < WINDOW, 0.0, NEG), NEG)
    for g in gs:
        kwt = kw_ref[0, g, pl.ds(pl.multiple_of(wb * Q_BLOCK, Q_BLOCK), nwt * Q_BLOCK), :]
        sw = _dot(kwt, rhs[g][0:HEAD_DIM]) + bias_w
        pw = jnp.exp2(sw - jnp.max(sw, axis=0, keepdims=True))
        vwT = jnp.concatenate([vwT_ref[0, g, wb + u] for u in range(nwt)], axis=1)
        o_win = normed(_dot(vwT, pw.astype(BF16)))
        for m in range(Q_PER_KV):
            c0 = g * 3 * Q_PER_KV + m * 3
            sl = slice(m * Q_BLOCK, (m + 1) * Q_BLOCK)
            head = g * Q_PER_KV + m
            oT_ref[0, head * HEAD_DIM:(head + 1) * HEAD_DIM, :] = (
                gts[c0:c0 + 1] * o_cmp[g][:, sl] + gts[c0 + 1:c0 + 2] * o_slc[g][:, sl] + gts[c0 + 2:c0 + 3] * o_win[:, sl])


def _sel_matrix(n_rows, n_chunks):
    j = np.arange(n_rows)[:, None]
    c = np.arange(n_chunks)[None, :]
    per = SLC_BLOCK // CMP_STRIDE
    m = 0.5 * ((c // per == j).astype(np.float32) + ((c + 1) // per == j).astype(np.float32))
    return jnp.asarray(m, BF16)


def _nsa_prompt(qT, gT, ckv, ks, vsT, kw, vwT):
    B, _, T = qT.shape
    nc, ns = T // CMP_STRIDE, T // SLC_BLOCK
    nq = T // Q_BLOCK
    mt = _sel_matrix(ns, nc)
    per_b = lambda shape: pl.BlockSpec((1,) + shape, lambda b, i: (b,) + (0,) * len(shape))
    return pl.pallas_call(
        functools.partial(_nsa_prompt_kernel, T=T),
        grid=(B, nq),
        in_specs=[pl.BlockSpec((1, GW, Q_BLOCK), lambda b, i: (b, 0, i)),
                  pl.BlockSpec((1, gT.shape[1], Q_BLOCK), lambda b, i: (b, 0, i)),
                  per_b((2, KV_HEADS, nc, HEAD_DIM)),
                  per_b((KV_HEADS, T, HEAD_DIM + ns)),
                  per_b((KV_HEADS, T // _KT, 2 * HEAD_DIM, _KT)),
                  per_b((KV_HEADS, T, HEAD_DIM)),
                  per_b((KV_HEADS, nq, 2 * HEAD_DIM, Q_BLOCK)),
                  _const_spec((ns, nc))],
        out_specs=pl.BlockSpec((1, GW, Q_BLOCK), lambda b, i: (b, 0, i)),
        out_shape=jax.ShapeDtypeStruct((B, GW, T), F32),
        compiler_params=_cparams(("parallel", "arbitrary")),
        name="nsa_prompt",
    )(qT, gT, ckv, ks, vsT, kw, vwT, mt)


_PPS = 16
_SEL_ROWS = 384


def _nsa_sample_a_kernel(pt_ref, *refs, n_pages, past_len, t_pad):
    pages = refs[:_PPS]
    w_ref, pe_ref, q_ref, mt_ref = refs[_PPS:_PPS + 4]
    ocmp_ref, selm_ref = refs[_PPS + 4:_PPS + 6]
    x_ref = refs[_PPS + 6]
    s = pl.program_id(1)
    cpp = PAGE // CMP_STRIDE
    for k in range(_PPS):
        x_ref[:, pl.ds(pl.multiple_of((s * _PPS + k) * cpp, cpp), cpp), :] = pages[k][0]

    @pl.when(s == pl.num_programs(1) - 1)
    def _():
        nc = n_pages * cpp
        n_blocks = past_len // SLC_BLOCK + 1
        lanes = 128
        ps_rows = []
        for g in range(KV_HEADS):
            ck = _compress_tail(_dot3(x_ref[g], w_ref[0]), _dot3(pe_ref[0], w_ref[0]))
            cv = _compress_tail(_dot(x_ref[KV_HEADS + g].astype(BF16), w_ref[1].astype(BF16)), _dot3(pe_ref[1], w_ref[1]))
            Q = jnp.concatenate([q_ref[0, :, (g * Q_PER_KV + m) * HEAD_DIM:(g * Q_PER_KV + m + 1) * HEAD_DIM]
                                 for m in range(Q_PER_KV)], axis=0) * (HEAD_DIM ** -0.5)
            t_rows = past_len + (lax.broadcasted_iota(jnp.int32, (Q_PER_KV * t_pad, 1), 0) & (t_pad - 1))
            sc = _dot3(Q, ck, _NT)
            n_lane = lax.broadcasted_iota(jnp.int32, (1, nc), 1)
            mask_c = jnp.where(n_lane * CMP_STRIDE + (2 * CMP_STRIDE - 1) <= t_rows, 1.0, 0.0)
            pc = _masked_softmax(sc, mask_c)
            ocmp_ref[0, g] = _dot(pc.astype(BF16), cv.astype(BF16))
            ps_rows.append(pc[0:t_pad] + pc[t_pad:2 * t_pad] + pc[2 * t_pad:3 * t_pad] + pc[3 * t_pad:4 * t_pad])
        ps = jnp.concatenate(ps_rows + [jnp.zeros((lanes - KV_HEADS * t_pad, nc), F32)], axis=0)
        selT = _dot_exact_lhs(mt_ref[...], ps, _NT)
        j_col = lax.broadcasted_iota(jnp.int32, (_SEL_ROWS, 1), 0)
        t_lane = past_len + (lax.broadcasted_iota(jnp.int32, (1, lanes), 1) & (t_pad - 1))
        cur = t_lane >> 6
        forced = jnp.where(j_col == 0, 1.0, jnp.where(j_col == cur, 1.0, jnp.where(j_col == cur - 1, 1.0, 0.0)))
        selT = jnp.where(forced > 0.5, FORCED, jnp.where(j_col * SLC_BLOCK <= t_lane, selT, -FORCED))
        selT = jnp.where(j_col < n_blocks, selT, -3.0 * FORCED)
        selmT = _top_select(selT, j_col.astype(F32))
        r = lax.broadcasted_iota(jnp.int32, (lanes, lanes), 0)
        c = lax.broadcasted_iota(jnp.int32, (lanes, lanes), 1)
        eye = jnp.where(r == c, 1.0, 0.0).astype(BF16)
        selm = _dot(eye, selmT.astype(BF16), _NT)
        selm_ref[0] = selm[0:KV_HEADS * t_pad]


def _nsa_sample_a(page_table, cmp_pages, layer, wcat, pe2, z, past_len):
    B, t_pad, _ = z.shape
    n_pages = page_table.shape[1]
    nc = n_pages * (PAGE // CMP_STRIDE)
    mt = _sel_matrix(_SEL_ROWS, nc)
    page_spec = lambda k: pl.BlockSpec((None, 1, 2 * KV_HEADS, PAGE // CMP_STRIDE, 1024),
                                       lambda b, s, pt: (layer, pt[b, s * _PPS + k], 0, 0, 0))
    cst = lambda shape: pl.BlockSpec(shape, lambda b, s, pt: (0,) * len(shape))
    grid_spec = pltpu.PrefetchScalarGridSpec(
        num_scalar_prefetch=1,
        grid=(B, n_pages // _PPS),
        in_specs=[page_spec(k) for k in range(_PPS)] + [
            cst((2, 1024, 128)), cst((2, 8, 1024)),
            pl.BlockSpec((1, t_pad, GW), lambda b, s, pt: (b, 0, C_Q // GW)),
            cst((_SEL_ROWS, nc))],
        out_specs=[pl.BlockSpec((1, KV_HEADS, Q_PER_KV * t_pad, HEAD_DIM), lambda b, s, pt: (b, 0, 0, 0)),
                   pl.BlockSpec((1, KV_HEADS * t_pad, _SEL_ROWS), lambda b, s, pt: (b, 0, 0))],
        scratch_shapes=[pltpu.VMEM((2 * KV_HEADS, nc, 1024), F32)])
    return pl.pallas_call(
        functools.partial(_nsa_sample_a_kernel, n_pages=n_pages, past_len=past_len, t_pad=t_pad),
        grid_spec=grid_spec,
        out_shape=[jax.ShapeDtypeStruct((B, KV_HEADS, Q_PER_KV * t_pad, HEAD_DIM), F32),
                   jax.ShapeDtypeStruct((B, KV_HEADS * t_pad, _SEL_ROWS), F32)],
        compiler_params=_cparams(("parallel", "arbitrary")),
        name="nsa_sample_select",
    )(page_table, *([cmp_pages] * _PPS), wcat, pe2, z, mt)


def _nsa_sample_b_kernel(pt_ref, *refs, n_pages, past_len, t_pad, t_real):
    kpages = refs[:_PPS]
    vpages = refs[_PPS:2 * _PPS]
    q_ref, rows_ref, kwvw_ref, gt_ref, selm_ref, ocmp_ref, cw_ref = refs[2 * _PPS:2 * _PPS + 7]
    o_ref = refs[2 * _PPS + 7]
    m_ref, l_ref, acc_ref = refs[2 * _PPS + 8:]
    s = pl.program_id(1)
    n_steps = n_pages // _PPS
    rows = Q_PER_KV * t_pad
    t_rows = past_len + (lax.broadcasted_iota(jnp.int32, (rows, 1), 0) & (t_pad - 1))

    @pl.when(s == 0)
    def _():
        m_ref[...] = jnp.full(m_ref.shape, NEG, F32)
        l_ref[...] = jnp.zeros(l_ref.shape, F32)
        acc_ref[...] = jnp.zeros(acc_ref.shape, F32)

    def q_rows(g):
        Q = jnp.concatenate([q_ref[0, :, (g * Q_PER_KV + m) * HEAD_DIM:(g * Q_PER_KV + m + 1) * HEAD_DIM]
                             for m in range(Q_PER_KV)], axis=0) * (HEAD_DIM ** -0.5)
        return Q.astype(BF16)

    def online(g, sc, ok, values):
        sc = jnp.where(ok > 0.5, sc, NEG)
        m_run = m_ref[g]
        m_new = jnp.maximum(m_run, jnp.max(sc, axis=-1, keepdims=True))
        p = jnp.exp(sc - m_new) * ok
        corr = jnp.exp(m_run - m_new)
        l_ref[g] = corr * l_ref[g] + jnp.sum(p, axis=-1, keepdims=True)
        acc_ref[g] = corr * acc_ref[g] + values(p.astype(BF16))
        m_ref[g] = m_new

    @pl.when(s < n_steps)
    def _():
        keys = _PPS * PAGE
        jr = lax.broadcasted_iota(jnp.int32, (_SEL_ROWS, keys), 0)
        cl = lax.broadcasted_iota(jnp.int32, (_SEL_ROWS, keys), 1)
        expand = jnp.where(jr == s * (keys // SLC_BLOCK) + (cl >> 6), 1.0, 0.0).astype(BF16)
        mv = _dot(selm_ref[0].astype(BF16), expand)
        pos = s * keys + lax.broadcasted_iota(jnp.int32, (1, keys), 1)
        for g in range(KV_HEADS):
            Qb = q_rows(g)
            sc = jnp.concatenate([_dot(Qb, kpages[k][g].astype(BF16)) for k in range(_PPS)], axis=1)
            mvg = jnp.concatenate([mv[g * t_pad:(g + 1) * t_pad]] * Q_PER_KV, axis=0)
            ok = jnp.where(pos <= t_rows, mvg, 0.0)

            def values(pb, g=g):
                out = _dot(pb[:, 0:PAGE], vpages[0][g].astype(BF16), _NT)
                for k in range(1, _PPS):
                    out = out + _dot(pb[:, k * PAGE:(k + 1) * PAGE], vpages[k][g].astype(BF16), _NT)
                return out

            online(g, sc, ok, values)

    @pl.when(s == n_steps)
    def _():
        gts = _sigmoid(gt_ref[0])
        last_blk = past_len // SLC_BLOCK
        r_lane = lax.broadcasted_iota(jnp.int32, (1, t_pad), 1)
        pieces = []
        for g in range(KV_HEADS):
            Qb = q_rows(g)
            k_new = rows_ref[0, :, (2 * KV_HEADS + g) * HEAD_DIM:(2 * KV_HEADS + g + 1) * HEAD_DIM]
            v_new = rows_ref[0, :, (3 * KV_HEADS + g) * HEAD_DIM:(3 * KV_HEADS + g + 1) * HEAD_DIM]
            sc = _dot(Qb, k_new.astype(BF16), _NT)
            sel_last = jnp.concatenate([selm_ref[0, g * t_pad:(g + 1) * t_pad, last_blk:last_blk + 1]] * Q_PER_KV, axis=0)
            vis = jnp.where(past_len + r_lane <= t_rows, jnp.where(r_lane < t_real, 1.0, 0.0), 0.0)
            online(g, sc, vis * sel_last, lambda pb, v_new=v_new: _dot(pb, v_new.astype(BF16)))
            o_slc = acc_ref[g] / jnp.maximum(l_ref[g], 1e-30)
            wrows = cw_ref.shape[3]
            kw_new = kwvw_ref[0, :, g * HEAD_DIM:(g + 1) * HEAD_DIM]
            vw_new = kwvw_ref[0, :, (KV_HEADS + g) * HEAD_DIM:(KV_HEADS + g + 1) * HEAD_DIM]
            sw = jnp.concatenate([_dot(Qb, cw_ref[0, g].astype(BF16)),
                                  _dot(Qb, kw_new.astype(BF16), _NT)], axis=1)
            idx = lax.broadcasted_iota(jnp.int32, (1, wrows + t_pad), 1)
            spos = past_len - wrows + idx
            dt = t_rows - spos
            real = jnp.where(idx < wrows + t_real, 1.0, 0.0)
            mask_w = jnp.where(dt >= 0, jnp.where(dt < WINDOW, real, 0.0), 0.0)
            pw = _masked_softmax(sw, mask_w).astype(BF16)
            o_win = _dot(pw[:, 0:wrows], cw_ref[1, g].astype(BF16), _NT) + _dot(pw[:, wrows:], vw_new.astype(BF16))
            pieces += _gated_heads(gts, g, ocmp_ref[0, g], o_slc, o_win, t_pad)
        o_ref[0] = jnp.concatenate(pieces, axis=1)


def _nsa_sample_b(page_table, cache_t, win_t, layer, z, selm, ocmp, past_len, t_real):
    B, t_pad, _ = z.shape
    n_pages = page_table.shape[1]
    n_steps = n_pages // _PPS
    rows = Q_PER_KV * t_pad
    wrows = win_t.shape[5]
    pidx = lambda b, s, pt, k: pt[b, jnp.minimum(s, n_steps - 1) * _PPS + k]
    page = lambda which, k: pl.BlockSpec((None, None, None, KV_HEADS, HEAD_DIM, PAGE),
                                         lambda b, s, pt: (layer, pidx(b, s, pt, k), which, 0, 0, 0))
    kspec = lambda k: page(2, k)
    vspec = lambda k: page(3, k)
    zspec = lambda col, w: pl.BlockSpec((1, t_pad, w), lambda b, s, pt: (b, 0, col // w))
    per_b = lambda shape: pl.BlockSpec((1,) + shape, lambda b, s, pt: (b,) + (0,) * len(shape))
    grid_spec = pltpu.PrefetchScalarGridSpec(
        num_scalar_prefetch=1,
        grid=(B, n_steps + 1),
        in_specs=[kspec(k) for k in range(_PPS)] + [vspec(k) for k in range(_PPS)] + [
            zspec(C_Q, GW), zspec(C_ROWS, GW), zspec(C_KW, 256), zspec(C_GATES, 256),
            per_b((KV_HEADS * t_pad, _SEL_ROWS)), per_b((KV_HEADS, rows, HEAD_DIM)),
            pl.BlockSpec((None, None, 2, KV_HEADS, HEAD_DIM, wrows), lambda b, s, pt: (layer, b, 0, 0, 0, 0))],
        out_specs=pl.BlockSpec((1, t_pad, GW), lambda b, s, pt: (b, 0, 0)),
        scratch_shapes=[pltpu.VMEM((KV_HEADS, rows, 1), F32), pltpu.VMEM((KV_HEADS, rows, 1), F32),
                        pltpu.VMEM((KV_HEADS, rows, HEAD_DIM), F32)])
    return pl.pallas_call(
        functools.partial(_nsa_sample_b_kernel, n_pages=n_pages, past_len=past_len, t_pad=t_pad, t_real=t_real),
        grid_spec=grid_spec,
        out_shape=jax.ShapeDtypeStruct((B, t_pad, GW), F32),
        compiler_params=_cparams(("parallel", "arbitrary")),
        name="nsa_sample_attend",
    )(page_table, *([cache_t] * (2 * _PPS)), z, z, z, z, selm, ocmp, win_t)


def _route(h, w, b):
    logits = _dot3(h, w) + b
    lane = lax.broadcasted_iota(jnp.int32, logits.shape, 1).astype(F32)
    big = 1e6
    is_g = lane < N_GROUPS
    gl = jnp.where(is_g, logits, NEG)
    gmax = jnp.max(gl, axis=-1, keepdims=True)
    gsel = jnp.min(jnp.where(is_g, jnp.where(gl == gmax, lane, big), big), axis=-1, keepdims=True)
    gp = 1.0 / jnp.sum(jnp.where(is_g, jnp.exp(gl - gmax), 0.0), axis=-1, keepdims=True)
    lo = N_GROUPS + gsel * EXP_PER_GROUP
    in_grp = jnp.where(lane >= lo, jnp.where(lane < lo + EXP_PER_GROUP, 1.0, 0.0), 0.0)
    el = jnp.where(in_grp > 0.5, logits, NEG)
    e = jnp.exp(el - jnp.max(el, axis=-1, keepdims=True)) * in_grp
    p = e / jnp.sum(e, axis=-1, keepdims=True)
    pm = jnp.where(in_grp > 0.5, p, -1.0)
    v1 = jnp.max(pm, axis=-1, keepdims=True)
    i1 = jnp.min(jnp.where(pm == v1, lane, big), axis=-1, keepdims=True)
    pm2 = jnp.where(lane == i1, -1.0, pm)
    v2 = jnp.max(pm2, axis=-1, keepdims=True)
    i2 = jnp.min(jnp.where(pm2 == v2, lane, big), axis=-1, keepdims=True)
    tot = v1 + v2
    return jnp.where(lane == 0, i1 - N_GROUPS,
                     jnp.where(lane == 1, i2 - N_GROUPS,
                               jnp.where(lane == 2, v1 / tot * gp, jnp.where(lane == 3, v2 / tot * gp, 0.0))))


def _proj_out_kernel(x_ref, ya_ref, yb_ref, yc_ref, yd_ref, g_ref, w_ref, b_ref, lg_ref, lb_ref, rw_ref, rb_ref,
                     h_ref, r_ref, *, alpha):
    yd = yd_ref[...] * g_ref[...]
    acc = _dot(ya_ref[...].astype(BF16), w_ref[0:GW, :])
    acc = acc + _dot(yb_ref[...].astype(BF16), w_ref[GW:2 * GW, :])
    acc = acc + _dot(yc_ref[...].astype(BF16), w_ref[2 * GW:3 * GW, :])
    acc = acc + _dot(yd.astype(BF16), w_ref[3 * GW:4 * GW, :])
    h = _layer_norm(alpha * x_ref[...] + acc + b_ref[...], lg_ref[...], lb_ref[...], LN_EPS)
    h_ref[...] = h
    r_ref[...] = _route(h, rw_ref[...], rb_ref[...])


def _proj_out(x2d, ya, yb, yc, yd, g, lp, alpha):
    n = x2d.shape[0]
    tm = min(256, n)
    row = lambda w: pl.BlockSpec((tm, w), lambda i: (i, 0))
    return pl.pallas_call(
        functools.partial(_proj_out_kernel, alpha=alpha),
        grid=(n // tm,),
        in_specs=[row(D_MODEL), row(GW), row(GW), row(GW), row(GW), row(GW),
                  _const_spec((D_MODEL, D_MODEL)), _const_spec((1, D_MODEL)), _const_spec((1, D_MODEL)),
                  _const_spec((1, D_MODEL)), _const_spec((D_MODEL, 128)), _const_spec((1, 128))],
        out_specs=[row(D_MODEL), row(128)],
        out_shape=[jax.ShapeDtypeStruct((n, D_MODEL), F32), jax.ShapeDtypeStruct((n, 128), F32)],
        compiler_params=_cparams(("parallel",)),
        name="proj_out_ln1",
    )(x2d, ya, yb, yc, yd, g, lp["w_out"], lp["b_out"], lp["ln1_g"], lp["ln1_b"], lp["router_w"], lp["router_b"])


def _for_rows(n_rows, body):
    if isinstance(n_rows, int):
        lax.fori_loop(0, n_rows, lambda r, c: (body(r), c)[1], 0, unroll=8)
        return
    groups = n_rows // 8

    def group(i, c):
        for u in range(8):
            body(i * 8 + u)
        return c

    lax.fori_loop(0, groups, group, 0)
    lax.fori_loop(groups * 8, n_rows, lambda r, c: (body(r), c)[1], 0)


def _issue_rows(idx_fn, n_rows, src_ref, dst_ref, sem):
    _for_rows(n_rows, lambda r: pltpu.make_async_copy(src_ref.at[pl.ds(idx_fn(r), 1), :],
                                                      dst_ref.at[pl.ds(r, 1), :], sem).start())


def _wait_rows(n_rows, src_ref, dst_ref, sem):
    _for_rows(n_rows, lambda r: pltpu.make_async_copy(src_ref.at[pl.ds(0, 1), :],
                                                      dst_ref.at[pl.ds(r, 1), :], sem).wait())


def _moe_kernel(te_ref, nu_ref, src_ref, cnt_ref, first_ref, nxt_ref, wslot_ref, tok_ref,
                h_ref, wg_ref, wu_ref, wd_ref, o_ref,
                xbuf_ref, sems, wgs_ref, wus_ref, wds_ref, wsems, wgb_ref, wub_ref, wdb_ref, *, layer):
    t = pl.program_id(0)
    tm = o_ref.shape[0]
    slot = t % 2
    n_used = nu_ref[0]

    def fetch(tile, to_slot):
        src = src_ref[tile]
        _issue_rows(lambda r: tok_ref[src + r], cnt_ref[tile], h_ref, xbuf_ref.at[to_slot], sems.at[to_slot])

    def weight_copies(e, s):
        return (pltpu.make_async_copy(wg_ref.at[layer, e], wgs_ref.at[s], wsems.at[s, 0]),
                pltpu.make_async_copy(wu_ref.at[layer, e], wus_ref.at[s], wsems.at[s, 1]),
                pltpu.make_async_copy(wd_ref.at[layer, e], wds_ref.at[s], wsems.at[s, 2]))

    @pl.when(t == 0)
    def _():
        for c in weight_copies(te_ref[0], 0):
            c.start()
        xbuf_ref[...] = jnp.zeros(xbuf_ref.shape, F32)
        fetch(0, 0)

    @pl.when(t + 1 < n_used)
    def _():
        fetch(t + 1, 1 - slot)

    @pl.when(first_ref[t] == 1)
    def _():
        s = wslot_ref[t]
        for c in weight_copies(te_ref[t], s):
            c.wait()

        @pl.when(nxt_ref[t] >= 0)
        def _():
            for c in weight_copies(nxt_ref[t], 1 - s):
                c.start()

        wgb_ref[...] = wgs_ref[s].astype(BF16)
        wub_ref[...] = wus_ref[s].astype(BF16)
        wdb_ref[...] = wds_ref[s].astype(BF16)

    @pl.when(jnp.logical_or(t == 0, t < n_used))
    def _():
        _wait_rows(cnt_ref[t], h_ref, xbuf_ref.at[slot], sems.at[slot])
        xb = xbuf_ref[slot].astype(BF16)
        hg = _silu(_dot(xb, wgb_ref[...])) * _dot(xb, wub_ref[...])
        o_ref[...] = _dot(hg.astype(BF16), wdb_ref[...])

    @pl.when(jnp.logical_and(t > 0, t >= n_used))
    def _():
        o_ref[...] = jnp.zeros(o_ref.shape, F32)


def _moe_experts(sched, h, w_gate, w_up, w_down, layer):
    tm = MOE_TILE
    r = sched[0].shape[0] * tm
    hbm = pl.BlockSpec(memory_space=pl.ANY)
    grid_spec = pltpu.PrefetchScalarGridSpec(
        num_scalar_prefetch=len(sched),
        grid=(r // tm,),
        in_specs=[hbm, hbm, hbm, hbm],
        out_specs=pl.BlockSpec((tm, D_MODEL), lambda t, *_: (t, 0)),
        scratch_shapes=[pltpu.VMEM((2, tm, D_MODEL), F32), pltpu.SemaphoreType.DMA((2,)),
                        pltpu.VMEM((2, D_MODEL, D_EXPERT), F32), pltpu.VMEM((2, D_MODEL, D_EXPERT), F32),
                        pltpu.VMEM((2, D_EXPERT, D_MODEL), F32), pltpu.SemaphoreType.DMA((2, 3)),
                        pltpu.VMEM((D_MODEL, D_EXPERT), BF16), pltpu.VMEM((D_MODEL, D_EXPERT), BF16),
                        pltpu.VMEM((D_EXPERT, D_MODEL), BF16)])
    return pl.pallas_call(
        functools.partial(_moe_kernel, layer=layer),
        grid_spec=grid_spec,
        out_shape=jax.ShapeDtypeStruct((r, D_MODEL), F32),
        compiler_params=_cparams(("arbitrary",), disable_bounds_checks=True),
        name="moe_experts",
    )(*sched, h, w_gate, w_up, w_down)


def _moe_schedule(eid):
    n = eid.shape[0]
    tm = MOE_TILE
    n_tiles = (2 * n + N_EXPERTS * (tm - 1)) // tm + 1
    flat_e = eid.reshape(-1)
    order = jnp.argsort(flat_e, stable=True).astype(jnp.int32)
    rank = jnp.argsort(order).astype(jnp.int32)
    onehot = (flat_e[:, None] == jnp.arange(N_EXPERTS, dtype=jnp.int32)[None, :]).astype(jnp.int32)
    counts = jnp.sum(onehot, axis=0)
    padded = (counts + tm - 1) // tm * tm
    pad_end = jnp.cumsum(padded)
    g_start = pad_end - padded
    u_start = jnp.cumsum(counts) - counts
    pos = (rank + jnp.sum(onehot * (g_start - u_start)[None, :], axis=1)).reshape(n, 2)
    n_used = (pad_end[-1] // tm).astype(jnp.int32)
    tile_id = jnp.arange(n_tiles, dtype=jnp.int32)
    tile_expert = jnp.sum((pad_end[None, :] <= (tile_id * tm)[:, None]).astype(jnp.int32), axis=1)
    tile_expert = jnp.minimum(tile_expert, N_EXPERTS - 1)
    in_tile = (tile_expert[:, None] == jnp.arange(N_EXPERTS, dtype=jnp.int32)[None, :]).astype(jnp.int32)
    pick = lambda table: jnp.sum(in_tile * table[None, :], axis=1)
    off = tile_id * tm - pick(g_start)
    tile_src = (pick(u_start) + off).astype(jnp.int32)
    tile_cnt = jnp.clip(pick(counts) - off, 0, tm).astype(jnp.int32)
    live = tile_id < n_used
    tile_cnt = jnp.where(live, tile_cnt, 0)
    ar = jnp.arange(N_EXPERTS, dtype=jnp.int32)
    used = counts > 0
    nxt_e = jnp.min(jnp.where(jnp.logical_and(ar[None, :] > ar[:, None], used[None, :]), ar[None, :], N_EXPERTS), axis=1)
    nxt_e = jnp.where(nxt_e < N_EXPERTS, nxt_e, -1)
    first = jnp.logical_and(live, off == 0).astype(jnp.int32)
    wslot = (pick(jnp.cumsum(used.astype(jnp.int32)) - 1) % 2).astype(jnp.int32)
    sched = (tile_expert.astype(jnp.int32), n_used.reshape(1), tile_src, tile_cnt, first,
             pick(nxt_e).astype(jnp.int32), wslot, order // 2)
    return sched, pos


def _ln2_kernel(idx_ref, h_ref, rt_ref, ys_ref, g_ref, b_ref, o_ref, ybuf_ref, sems, *, alpha):
    i = pl.program_id(0)
    tm = h_ref.shape[0]
    slot = i % 2

    def fetch(tile, to_slot):
        base = tile * 2 * tm
        _issue_rows(lambda r: idx_ref[base + r], 2 * tm, ys_ref, ybuf_ref.at[to_slot], sems.at[to_slot])

    @pl.when(i == 0)
    def _():
        fetch(0, 0)

    @pl.when(i + 1 < pl.num_programs(0))
    def _():
        fetch(i + 1, 1 - slot)

    _wait_rows(2 * tm, ys_ref, ybuf_ref.at[slot], sems.at[slot])
    w0 = rt_ref[:, 2:3]
    w1 = rt_ref[:, 3:4]
    moe = w0 * ybuf_ref[slot, 0:tm, :] + w1 * ybuf_ref[slot, tm:2 * tm, :]
    o_ref[...] = _layer_norm(alpha * h_ref[...] + moe, g_ref[...], b_ref[...], LN_EPS)


def _combine_ln2(h, routed, pos, ys, g, b, alpha):
    n = h.shape[0]
    tm = min(128, n)
    nt = n // tm
    idx = jnp.transpose(pos.reshape(nt, tm, 2), (0, 2, 1)).reshape(-1)
    grid_spec = pltpu.PrefetchScalarGridSpec(
        num_scalar_prefetch=1,
        grid=(nt,),
        in_specs=[pl.BlockSpec((tm, D_MODEL), lambda i, idx_ref: (i, 0)),
                  pl.BlockSpec((tm, 128), lambda i, idx_ref: (i, 0)),
                  pl.BlockSpec(memory_space=pl.ANY),
                  pl.BlockSpec((1, D_MODEL), lambda i, idx_ref: (0, 0)),
                  pl.BlockSpec((1, D_MODEL), lambda i, idx_ref: (0, 0))],
        out_specs=pl.BlockSpec((tm, D_MODEL), lambda i, idx_ref: (i, 0)),
        scratch_shapes=[pltpu.VMEM((2, 2 * tm, D_MODEL), F32), pltpu.SemaphoreType.DMA((2,))])
    return pl.pallas_call(
        functools.partial(_ln2_kernel, alpha=alpha),
        grid_spec=grid_spec,
        out_shape=jax.ShapeDtypeStruct((n, D_MODEL), F32),
        compiler_params=_cparams(("arbitrary",), disable_bounds_checks=True),
        name="combine_ln2",
    )(idx, h, routed, ys, g, b)


def _relayout_cols(w):
    pad = lambda n: jnp.zeros(w.shape[:-1] + (n,), w.dtype)
    d0 = 3864
    return jnp.concatenate([
        w[..., 0:2328], pad(C_BG - 2328),
        w[..., 2328:3864],
        w[..., d0:d0 + 1536],
        w[..., d0 + 1536:d0 + 1568], pad(96),
        w[..., d0 + 1568:d0 + 1600], pad(96),
        w[..., d0 + 1600:d0 + 1696], pad(N_Z - C_GL - 96)], axis=-1)


def _relayout_d(v):
    pad = lambda n: jnp.zeros(v.shape[:-1] + (n,), v.dtype)
    return jnp.concatenate([v[..., 0:1536], v[..., 1536:1568], pad(96), v[..., 1568:1600], pad(96),
                            v[..., 1600:1696], pad(32)], axis=-1)


def _shift_cols(zrow):
    return jnp.concatenate([zrow[..., C_R:C_R + 1536], zrow[..., C_WL:C_WL + 32], zrow[..., C_AL:C_AL + 32],
                            zrow[..., C_GL:C_GL + 96]], axis=-1)


def _pad_rows(w, rows):
    return jnp.concatenate([w, jnp.zeros((rows - w.shape[0],) + w.shape[1:], w.dtype)], axis=0)


def _layer_params(l, P):
    wc = P["nsa_wc"][l]
    half = 2 * CMP_STRIDE // 2
    wcat = jnp.concatenate([wc[:, :half].reshape(2, half * HEAD_DIM, HEAD_DIM),
                            wc[:, half:].reshape(2, half * HEAD_DIM, HEAD_DIM)], axis=-1)
    pe = P["nsa_pe"][l]
    pe2 = jnp.stack([pe[:, :half].reshape(2, half * HEAD_DIM), pe[:, half:].reshape(2, half * HEAD_DIM)], axis=1)
    pe2 = jnp.concatenate([pe2, jnp.zeros((2, 6, half * HEAD_DIM), F32)], axis=1)
    router_w = jnp.concatenate([P["moe_w_grp"][l], P["moe_w_rte"][l],
                                jnp.zeros((D_MODEL, 128 - N_GROUPS - N_EXPERTS), F32)], axis=1)
    router_b = jnp.concatenate([P["moe_b_grp"][l], P["moe_b_rte"][l],
                                jnp.zeros((128 - N_GROUPS - N_EXPERTS,), F32)])[None, :]
    row = lambda name: P[name][l][None, :]
    return dict(
        w_in=_relayout_cols(P["w_in"][l]).astype(BF16), b_in=_relayout_cols(P["b_in"][l])[None, :],
        w_out=P["w_out"][l].astype(BF16), b_out=row("b_out"),
        ln1_g=row("ln1_g"), ln1_b=row("ln1_b"), ln2_g=row("ln2_g"), ln2_b=row("ln2_b"),
        conv_a_w=P["conv_a_w"][l], conv_a_b=row("conv_a_b"), ln_a_g=row("ln_a_g"), ln_a_b=row("ln_a_b"),
        conv_c_w=P["conv_c_w"][l], wcat=wcat, pe2=pe2,
        rw_mu=_relayout_d(P["rw_mu"][l])[None, :], rw_w0=row("rw_w0"), rw_w2=_pad_rows(P["rw_w2"][l], 128),
        rw_a0=row("rw_a0"), rw_a2=_pad_rows(P["rw_a2"][l], 128), rw_g2=_pad_rows(P["rw_g2"][l], 128),
        rw_kk=row("rw_kk"), rw_ka=row("rw_ka"), rw_rk=P["rw_rk"][l].reshape(1, GW),
        rw_lnx_g=P["rw_lnx_g"][l].reshape(HEADS, HEAD_DIM), rw_lnx_b=P["rw_lnx_b"][l].reshape(HEADS, HEAD_DIM),
        router_w=router_w, router_b=router_b)


def _mix_rwkv(z, shift, s0, lp, t_real):
    B, T, _ = z.shape
    r, w, k, v, kk, b, bonus, g = _rwkv_prep(z, shift, lp["rw_mu"], lp["rw_w0"], lp["rw_w2"], lp["rw_a0"],
                                             lp["rw_a2"], lp["rw_g2"], lp["rw_kk"], lp["rw_ka"], lp["rw_rk"], t_real)
    if t_real == T and T % _RW_CHUNK == 0:
        y, s_new = _rwkv_chunked(r, w, k, v, kk, b, bonus, s0, lp["rw_lnx_g"], lp["rw_lnx_b"])
    else:
        y, s_new = _rwkv_scan(r, w, k, v, kk, b, bonus, s0, lp["rw_lnx_g"], lp["rw_lnx_b"], t_real)
    yd = jnp.transpose(y, (0, 2, 1, 3)).reshape(B, T, GW)
    return yd, g, s_new


def _nsa_prompt_group(z, lp):
    B, T, _ = z.shape
    kv6 = z[:, :, C_ROWS:C_GATES].reshape(B, T, 6, KV_HEADS, HEAD_DIM)
    nc, ns = T // CMP_STRIDE, T // SLC_BLOCK
    cmp_in = kv6[:, :, 0:2].reshape(B, nc, CMP_STRIDE, 2, KV_HEADS, HEAD_DIM)
    cmp_in = jnp.transpose(cmp_in, (0, 3, 4, 1, 2, 5)).reshape(B, 2, KV_HEADS, nc, CMP_STRIDE * HEAD_DIM)
    ckv = _compress_prompt(cmp_in, lp["wcat"], lp["pe2"])
    rows_of = lambda x: jnp.transpose(x.astype(BF16), (0, 2, 1, 3))
    block_id = jnp.arange(T, dtype=jnp.int32) // SLC_BLOCK
    onehot = (block_id[:, None] == jnp.arange(ns, dtype=jnp.int32)[None, :]).astype(BF16)
    ks = jnp.concatenate([rows_of(kv6[:, :, 2]), jnp.broadcast_to(onehot, (B, KV_HEADS, T, ns))], axis=-1)

    def value_tiles(x, tile):
        vT = jnp.transpose(x.astype(BF16), (0, 2, 3, 1))
        aug = jnp.concatenate([vT, jnp.ones((B, KV_HEADS, 1, T), BF16), jnp.zeros((B, KV_HEADS, HEAD_DIM - 1, T), BF16)], axis=2)
        return jnp.transpose(aug.reshape(B, KV_HEADS, 2 * HEAD_DIM, T // tile, tile), (0, 1, 3, 2, 4))

    qT = jnp.transpose(z[:, :, C_Q:C_Q + GW], (0, 2, 1))
    gT = jnp.transpose(z[:, :, C_GATES:C_GATES + 32], (0, 2, 1))
    ybT = _nsa_prompt(qT, gT, ckv, ks, value_tiles(kv6[:, :, 3], _KT), rows_of(kv6[:, :, 4]), value_tiles(kv6[:, :, 5], Q_BLOCK))
    rows = kv6[:, :, 0:4]
    win = kv6[:, T - min(WINDOW, T):, 4:6]
    return jnp.transpose(ybT, (0, 2, 1)), rows, win


def _cache_views(cache_nsa, cache_win):
    L, n_pool = cache_nsa.shape[0], cache_nsa.shape[1]
    cpp = PAGE // CMP_STRIDE
    cache_t = jnp.transpose(cache_nsa, (0, 1, 3, 4, 5, 2))
    win_t = jnp.transpose(cache_win, (0, 1, 3, 4, 5, 2))
    cmp_pages = cache_t[:, :, 0:2].reshape(L, n_pool, 2, KV_HEADS, HEAD_DIM, cpp, CMP_STRIDE)
    cmp_pages = jnp.transpose(cmp_pages, (0, 1, 2, 3, 5, 6, 4)).reshape(L, n_pool, 2 * KV_HEADS, cpp, CMP_STRIDE * HEAD_DIM)
    return cache_t, win_t, cmp_pages


def _nsa_sample_group(z, views, layer, page_table, win_l, lp, t_real):
    B, t_pad, _ = z.shape
    cache_t, win_t, cmp_pages = views
    past_len = page_table.shape[1] * PAGE
    ocmp, selm = _nsa_sample_a(page_table, cmp_pages, layer, lp["wcat"], lp["pe2"], z, past_len)
    yb = _nsa_sample_b(page_table, cache_t, win_t, layer, z, selm, ocmp, past_len, t_real)
    zr = z[:, :t_real]
    rows = zr[:, :, C_ROWS:C_KW].reshape(B, t_real, 4, KV_HEADS, HEAD_DIM)
    new_win = zr[:, :, C_KW:C_GATES].reshape(B, t_real, 2, KV_HEADS, HEAD_DIM)
    win = jnp.concatenate([win_l, new_win], axis=1)[:, t_real:]
    return yb, rows, win


def _group_layer(x, lp, alpha, nsa_fn, conv_a_buf, conv_c_buf, rw_state, rw_shift, t_real):
    B, T, _ = x.shape
    z = _proj_in(x.reshape(B * T, D_MODEL), lp["w_in"], lp["b_in"]).reshape(B, T, N_Z)
    ya, yc, conv_a_new, conv_c_new = _conv_mix(z, conv_a_buf, conv_c_buf, lp["conv_a_w"], lp["conv_a_b"],
                                               lp["ln_a_g"], lp["ln_a_b"], lp["conv_c_w"], t_real)
    yb, nsa_rows, win_new = nsa_fn(z)
    yd, g, rw_state_new = _mix_rwkv(z, _relayout_d(rw_shift)[:, None, :], rw_state, lp, t_real)
    flat = lambda a: a.reshape(B * T, a.shape[-1])
    h, routed = _proj_out(flat(x), flat(ya), flat(yb), flat(yc), flat(yd), flat(g), lp, alpha)
    shift_new = _shift_cols(z[:, t_real - 1])
    return (h, routed), (nsa_rows, win_new, conv_a_new, conv_c_new, rw_state_new, shift_new)


def _moe_ln2(groups, lp, P, layer, alpha):
    h_list = [g[0] for g in groups]
    routed = [g[1] for g in groups]
    eid = jnp.concatenate([r[:, 0:2] for r in routed], axis=0).astype(jnp.int32)
    h_all = jnp.concatenate(h_list, axis=0)
    sched, pos = _moe_schedule(eid)
    ys = _moe_experts(sched, h_all, P["moe_w_gate"], P["moe_w_up"], P["moe_w_down"], layer)
    outs = []
    off = 0
    for h, rt in zip(h_list, routed):
        n = h.shape[0]
        outs.append(_combine_ln2(h, rt, pos[off:off + n], ys, lp["ln2_g"], lp["ln2_b"], alpha))
        off += n
    return outs


def kernel(x_prompt, x_sample, cache_nsa, cache_win, state_conv_a, state_conv_c, state_rwkv, state_rwkv_shift, page_table, w_in, b_in, w_out, b_out, ln1_g, ln1_b, ln2_g, ln2_b, conv_a_w, conv_a_b, ln_a_g, ln_a_b, conv_c_w, nsa_wc, nsa_pe, rw_mu, rw_w0, rw_w2, rw_a0, rw_a2, rw_g2, rw_kk, rw_ka, rw_rk, rw_lnx_g, rw_lnx_b, moe_w_grp, moe_b_grp, moe_w_rte, moe_b_rte, moe_w_gate, moe_w_up, moe_w_down):
    P = dict(w_in=w_in, b_in=b_in, w_out=w_out, b_out=b_out, ln1_g=ln1_g, ln1_b=ln1_b, ln2_g=ln2_g, ln2_b=ln2_b,
             conv_a_w=conv_a_w, conv_a_b=conv_a_b, ln_a_g=ln_a_g, ln_a_b=ln_a_b, conv_c_w=conv_c_w,
             nsa_wc=nsa_wc, nsa_pe=nsa_pe, rw_mu=rw_mu, rw_w0=rw_w0, rw_w2=rw_w2, rw_a0=rw_a0, rw_a2=rw_a2,
             rw_g2=rw_g2, rw_kk=rw_kk, rw_ka=rw_ka, rw_rk=rw_rk, rw_lnx_g=rw_lnx_g, rw_lnx_b=rw_lnx_b,
             moe_w_grp=moe_w_grp, moe_b_grp=moe_b_grp, moe_w_rte=moe_w_rte, moe_b_rte=moe_b_rte,
             moe_w_gate=moe_w_gate, moe_w_up=moe_w_up, moe_w_down=moe_w_down)
    depth = w_in.shape[0]
    alpha = (2.0 * depth) ** 0.25
    bp, tp, _ = x_prompt.shape
    bs, ts, _ = x_sample.shape
    ts_pad = 8
    xp = x_prompt
    xs = jnp.concatenate([x_sample, jnp.zeros((bs, ts_pad - ts, D_MODEL), F32)], axis=1)
    outs_p, outs_s = [], []
    views = _cache_views(cache_nsa, cache_win)
    for l in range(depth):
        lp = _layer_params(l, P)
        h_p, st_p = _group_layer(xp, lp, alpha, functools.partial(_nsa_prompt_group, lp=lp),
                                 jnp.zeros((bp, CONV_A_WIDTH - 1, GW), F32), jnp.zeros((bp, SHORT_CONV_WIDTH - 1, GW), F32),
                                 jnp.zeros((bp, HEADS, HEAD_DIM, HEAD_DIM), F32), jnp.zeros((bp, N_COLS_D), F32), tp)
        h_s, st_s = _group_layer(xs, lp, alpha,
                                 functools.partial(_nsa_sample_group, views=views, layer=l, page_table=page_table,
                                                   win_l=cache_win[l], lp=lp, t_real=ts),
                                 state_conv_a[l], state_conv_c[l], state_rwkv[l], state_rwkv_shift[l], ts)
        o_p, o_s = _moe_ln2([h_p, h_s], lp, P, l, alpha)
        xp = o_p.reshape(bp, tp, D_MODEL)
        xs = o_s.reshape(bs, ts_pad, D_MODEL)
        outs_p.append(st_p)
        outs_s.append(st_s)
    stk = lambda outs, i: jnp.stack([o[i] for o in outs], axis=0)
    return (xp, xs[:, :ts], stk(outs_p, 0), stk(outs_s, 0), stk(outs_p, 1), stk(outs_s, 1), stk(outs_p, 2), stk(outs_s, 2),
            stk(outs_p, 3), stk(outs_s, 3), stk(outs_p, 4), stk(outs_s, 4), stk(outs_p, 5), stk(outs_s, 5))
```

```python
import functools

import numpy as np
import jax
import jax.numpy as jnp
from jax import lax
from jax.experimental import pallas as pl
from jax.experimental.pallas import tpu as pltpu

F32 = jnp.float32
BF16 = jnp.bfloat16

D_MODEL = 2048
DEPTH = 2
HEAD_DIM = 64
GW = 512
HEADS = 8
CONV_A_WIDTH = 31
SHORT_CONV_WIDTH = 3
KV_HEADS = 2
Q_PER_KV = 4
CMP_STRIDE = 16
SLC_BLOCK = 64
N_SLC = 16
WINDOW = 512
Q_BLOCK = 128
PAGE = 128
RWKV_GN_EPS = 64e-5
LN_EPS = 1e-5
NEG = -1e30
FORCED = 1e4
N_EXPERTS = 32
EXP_PER_GROUP = 8
N_GROUPS = 4
D_EXPERT = 512
N_IN = 5560
N_COLS_D = 1696

N_Z = 6144
C_VAL, C_GATE = 0, 512
C_Q, C_ROWS, C_KW, C_GATES = 1024, 1536, 2048, 2304
C_BG, C_CG, C_H = 2560, 3072, 3584
C_R, C_K, C_V = 4096, 4608, 5120
C_WL, C_AL, C_GL = 5632, 5760, 5888

VMEM_LIMIT = 56 * 1024 * 1024
MOE_TILE = 256


def _cparams(sem, vmem=VMEM_LIMIT, disable_bounds_checks=False):
    return pltpu.CompilerParams(dimension_semantics=sem, vmem_limit_bytes=vmem,
                                disable_bounds_checks=disable_bounds_checks)


def _const_spec(shape):
    nd = len(shape)
    return pl.BlockSpec(shape, lambda *_: (0,) * nd)


def _split2(x):
    hi = x.astype(BF16)
    lo = (x - hi.astype(F32)).astype(BF16)
    return hi, lo


def _split3(x):
    hi = x.astype(BF16)
    r1 = x - hi.astype(F32)
    mid = r1.astype(BF16)
    lo = (r1 - mid.astype(F32)).astype(BF16)
    return hi, mid, lo


_NN = (((1,), (0,)), ((), ()))
_NT = (((1,), (1,)), ((), ()))
_TN = (((0,), (0,)), ((), ()))


def _dot(a, b, dims=_NN):
    return lax.dot_general(a, b, dims, preferred_element_type=F32)


def _dot3(a, b, dims=_NN):
    ah, al = _split2(a)
    bh, bl = _split2(b)
    return _dot(ah, bh, dims) + _dot(al, bh, dims) + _dot(ah, bl, dims)


def _dot_exact_lhs(a_bf16, b, dims=_NN):
    hi, mid, lo = _split3(b)
    return _dot(a_bf16, hi, dims) + _dot(a_bf16, mid, dims) + _dot(a_bf16, lo, dims)


def _sigmoid(x):
    return 1.0 / (1.0 + jnp.exp(-x))


def _silu(x):
    return x * _sigmoid(x)


def _layer_norm(x, g, b, eps):
    mu = jnp.mean(x, axis=-1, keepdims=True)
    xc = x - mu
    var = jnp.mean(xc * xc, axis=-1, keepdims=True)
    return xc * lax.rsqrt(var + eps) * g + b


def _masked_softmax(s, maskf):
    s = jnp.where(maskf > 0.5, s, NEG)
    e = jnp.exp(s - jnp.max(s, axis=-1, keepdims=True)) * maskf
    return e / jnp.maximum(jnp.sum(e, axis=-1, keepdims=True), 1e-30)


def _proj_in_kernel(x_ref, w_ref, b_ref, o_ref, xb_ref):
    @pl.when(pl.program_id(1) == 0)
    def _():
        xb_ref[...] = x_ref[...].astype(BF16)

    o_ref[...] = _dot(xb_ref[...], w_ref[...]) + b_ref[...]


def _proj_in(x2d, w_bf16, b_row):
    n = x2d.shape[0]
    tm = min(1024, n)
    tn = 512
    return pl.pallas_call(
        _proj_in_kernel,
        grid=(n // tm, N_Z // tn),
        in_specs=[pl.BlockSpec((tm, D_MODEL), lambda i, j: (i, 0)),
                  pl.BlockSpec((D_MODEL, tn), lambda i, j: (0, j)),
                  pl.BlockSpec((1, tn), lambda i, j: (0, j))],
        out_specs=pl.BlockSpec((tm, tn), lambda i, j: (i, j)),
        out_shape=jax.ShapeDtypeStruct((n, N_Z), F32),
        scratch_shapes=[pltpu.VMEM((tm, D_MODEL), BF16)],
        compiler_params=_cparams(("parallel", "arbitrary")),
        name="proj_in",
    )(x2d, w_bf16, b_row)


_CONV_ROWS = 32


def _conv_kernel(val_ref, gate_ref, bg_ref, cg_ref, h_ref, bufa_ref, bufc_ref, wa_ref, ba_ref, lag_ref,
                 lab_ref, wc_ref, ya_ref, yc_ref, na_ref, nc_ref, ea_ref, ec_ref, *, tt, t_last):
    it = pl.program_id(1)
    nt = pl.num_programs(1)
    ha = 32
    hc = 8

    @pl.when(it == 0)
    def _():
        ea_ref[0:ha, :] = jnp.zeros((ha, GW), F32)
        ea_ref[ha - (CONV_A_WIDTH - 1):ha, :] = bufa_ref[0]
        ec_ref[0:hc, :] = jnp.zeros((hc, GW), F32)
        ec_ref[hc - (SHORT_CONV_WIDTH - 1):hc, :] = bufc_ref[0]

    @pl.when(it > 0)
    def _():
        ea_ref[0:ha, :] = ea_ref[tt:tt + ha, :]
        ec_ref[0:hc, :] = ec_ref[tt:tt + hc, :]

    ea_ref[ha:ha + tt, :] = val_ref[0] * _sigmoid(gate_ref[0])
    ec_ref[hc:hc + tt, :] = cg_ref[0] * h_ref[0]

    rc = min(_CONV_ROWS, tt)
    offa = ha - (CONV_A_WIDTH - 1)
    offc = hc - (SHORT_CONV_WIDTH - 1)
    for c in range(tt // rc):
        r0 = c * rc
        acc = jnp.zeros((rc, GW), F32)
        for b in range(8):
            rows_b = rc if b == 0 else rc + 8
            part = None
            for a in range((offa + CONV_A_WIDTH - 1) // 8 + 1):
                j = 8 * a + b - offa
                if 0 <= j < CONV_A_WIDTH:
                    assert r0 + 8 * a + rows_b <= tt + ha
                    term = wa_ref[j:j + 1, :] * ea_ref[r0 + 8 * a:r0 + 8 * a + rows_b, :]
                    part = term if part is None else part + term
            acc = acc + part[b:b + rc]
        y = _layer_norm(acc + ba_ref[...], lag_ref[...], lab_ref[...], LN_EPS)
        ya_ref[0, r0:r0 + rc, :] = _silu(y)
        accc = jnp.zeros((rc, GW), F32)
        for j in range(SHORT_CONV_WIDTH):
            accc = accc + wc_ref[j:j + 1, :] * ec_ref[r0 + offc + j:r0 + offc + j + rc, :]
        yc_ref[0, r0:r0 + rc, :] = bg_ref[0, r0:r0 + rc, :] * accc

    @pl.when(it == nt - 1)
    def _():
        na_ref[0] = ea_ref[t_last + offa:t_last + ha, :]
        nc_ref[0] = ec_ref[t_last + offc:t_last + hc, :]


def _conv_mix(z, buf_a, buf_c, wa, ba, lag, lab, wc, t_real):
    B, T, _ = z.shape
    tt = min(256, T)
    nt = T // tt
    t_last = t_real - (nt - 1) * tt
    zspec = lambda col: pl.BlockSpec((1, tt, GW), lambda b, i: (b, i, col // GW))
    return pl.pallas_call(
        functools.partial(_conv_kernel, tt=tt, t_last=t_last),
        grid=(B, nt),
        in_specs=[zspec(C_VAL), zspec(C_GATE), zspec(C_BG), zspec(C_CG), zspec(C_H),
                  pl.BlockSpec((1, CONV_A_WIDTH - 1, GW), lambda b, i: (b, 0, 0)),
                  pl.BlockSpec((1, SHORT_CONV_WIDTH - 1, GW), lambda b, i: (b, 0, 0)),
                  _const_spec((CONV_A_WIDTH, GW)), _const_spec((1, GW)), _const_spec((1, GW)),
                  _const_spec((1, GW)), _const_spec((SHORT_CONV_WIDTH, GW))],
        out_specs=[pl.BlockSpec((1, tt, GW), lambda b, i: (b, i, 0)),
                   pl.BlockSpec((1, tt, GW), lambda b, i: (b, i, 0)),
                   pl.BlockSpec((1, CONV_A_WIDTH - 1, GW), lambda b, i: (b, 0, 0)),
                   pl.BlockSpec((1, SHORT_CONV_WIDTH - 1, GW), lambda b, i: (b, 0, 0))],
        out_shape=[jax.ShapeDtypeStruct((B, T, GW), F32), jax.ShapeDtypeStruct((B, T, GW), F32),
                   jax.ShapeDtypeStruct((B, CONV_A_WIDTH - 1, GW), F32),
                   jax.ShapeDtypeStruct((B, SHORT_CONV_WIDTH - 1, GW), F32)],
        scratch_shapes=[pltpu.VMEM((tt + 32, GW), F32), pltpu.VMEM((tt + 8, GW), F32)],
        compiler_params=_cparams(("parallel", "arbitrary")),
        name="conv_mix",
    )(z, z, z, z, z, buf_a, buf_c, wa, ba, lag, lab, wc)


def _rwkv_token_math(r_ref, k_ref, v_ref, wl_ref, al_ref, gl_ref, mu_ref, w0_ref, w2_ref, a0_ref, a2_ref, g2_ref,
                     kkp_ref, kap_ref, rk_ref, carry_ref, tt, t_last):
    row = lax.broadcasted_iota(jnp.int32, (tt, 1), 0)

    def mixed(x, c0):
        w = x.shape[1]
        prev = pltpu.roll(x, 1, 0) if tt > 1 else x
        prev = jnp.where(row == 0, carry_ref[:, c0:c0 + w], prev)
        return x + (prev - x) * mu_ref[:, c0:c0 + w]

    xr, xk, xv = r_ref[0], k_ref[0], v_ref[0]
    xwl, xal, xgl = wl_ref[0], al_ref[0], gl_ref[0]
    r = mixed(xr, 0)
    k = mixed(xk, GW)
    v = mixed(xv, 2 * GW)
    wl = mixed(xwl, 3 * GW)
    al = mixed(xal, 3 * GW + 128)
    gl = mixed(xgl, 3 * GW + 256)
    lr = t_last - 1
    carry_ref[:, 0:GW] = xr[lr:lr + 1]
    carry_ref[:, GW:2 * GW] = xk[lr:lr + 1]
    carry_ref[:, 2 * GW:3 * GW] = xv[lr:lr + 1]
    carry_ref[:, 3 * GW:3 * GW + 128] = xwl[lr:lr + 1]
    carry_ref[:, 3 * GW + 128:3 * GW + 256] = xal[lr:lr + 1]
    carry_ref[:, 3 * GW + 256:3 * GW + 384] = xgl[lr:lr + 1]

    y = w0_ref[...] + _dot3(jnp.tanh(wl), w2_ref[...])
    u = -y
    w_log = -(jnp.maximum(u, 0.0) + jnp.log(1.0 + jnp.exp(-jnp.abs(u)))) - 0.5
    log_decay = -jnp.exp(w_log)
    a = _sigmoid(a0_ref[...] + _dot3(al, a2_ref[...]))
    g = _dot3(_sigmoid(gl), g2_ref[...])
    kk = k * kkp_ref[...]
    k2 = k * (1.0 + (a - 1.0) * kap_ref[...])
    rkk = r * k2 * rk_ref[...]
    heads = [[] for _ in range(7)]
    for h in range(HEADS):
        sl = slice(h * HEAD_DIM, (h + 1) * HEAD_DIM)
        kk_h = kk[:, sl]
        nrm = jnp.sqrt(jnp.sum(kk_h * kk_h, axis=-1, keepdims=True))
        kk_h = kk_h / jnp.maximum(nrm, 1e-12)
        v_h = v[:, sl]
        per_head = (r[:, sl], log_decay[:, sl], k2[:, sl], v_h, kk_h, kk_h * a[:, sl],
                    jnp.sum(rkk[:, sl], axis=-1, keepdims=True) * v_h)
        for lst, val in zip(heads, per_head):
            lst.append(val)
    return g, heads


def _rwkv_prep_kernel(r_ref, k_ref, v_ref, wl_ref, al_ref, gl_ref, sh_ref, mu_ref, w0_ref, w2_ref, a0_ref,
                      a2_ref, g2_ref, kkp_ref, kap_ref, rk_ref,
                      ro_ref, wo_ref, ko_ref, vo_ref, kko_ref, bo_ref, bonus_ref, g_ref, carry_ref, *, tt, t_last):
    @pl.when(pl.program_id(1) == 0)
    def _():
        carry_ref[...] = sh_ref[0]

    g, heads = _rwkv_token_math(r_ref, k_ref, v_ref, wl_ref, al_ref, gl_ref, mu_ref, w0_ref, w2_ref, a0_ref, a2_ref,
                                g2_ref, kkp_ref, kap_ref, rk_ref, carry_ref, tt, t_last)
    g_ref[0] = g
    for out_ref, vals in zip((ro_ref, wo_ref, ko_ref, vo_ref, kko_ref, bo_ref, bonus_ref), heads):
        for h in range(HEADS):
            out_ref[0, h] = vals[h]


def _rwkv_prep(z, shift, mu, w0, w2p, a0, a2p, g2p, kkp, kap, rk, t_real):
    B, T, _ = z.shape
    tt = min(256, T)
    nt = T // tt
    t_last = t_real - (nt - 1) * tt
    wide = lambda col: pl.BlockSpec((1, tt, GW), lambda b, i: (b, i, col // GW))
    thin = lambda col: pl.BlockSpec((1, tt, 128), lambda b, i: (b, i, col // 128))
    head_out = pl.BlockSpec((1, HEADS, tt, HEAD_DIM), lambda b, i: (b, 0, i, 0))
    head_shape = jax.ShapeDtypeStruct((B, HEADS, T, HEAD_DIM), F32)
    dw = 3 * GW + 384
    return pl.pallas_call(
        functools.partial(_rwkv_prep_kernel, tt=tt, t_last=t_last),
        grid=(B, nt),
        in_specs=[wide(C_R), wide(C_K), wide(C_V), thin(C_WL), thin(C_AL), thin(C_GL),
                  pl.BlockSpec((1, 1, dw), lambda b, i: (b, 0, 0)),
                  _const_spec((1, dw)), _const_spec((1, GW)), _const_spec((128, GW)), _const_spec((1, GW)),
                  _const_spec((128, GW)), _const_spec((128, GW)), _const_spec((1, GW)), _const_spec((1, GW)),
                  _const_spec((1, GW))],
        out_specs=[head_out] * 7 + [pl.BlockSpec((1, tt, GW), lambda b, i: (b, i, 0))],
        out_shape=[head_shape] * 7 + [jax.ShapeDtypeStruct((B, T, GW), F32)],
        scratch_shapes=[pltpu.VMEM((1, dw), F32)],
        compiler_params=_cparams(("parallel", "arbitrary")),
        name="rwkv_prep",
    )(z, z, z, z, z, z, shift, mu, w0, w2p, a0, a2p, g2p, kkp, kap, rk)


def _rwkv_scan_kernel(r_ref, w_ref, k_ref, v_ref, kk_ref, b_ref, bonus_ref, s0_ref, lg_ref, lb_ref,
                      y_ref, s_ref, *, nb, tt, n_steps):
    it = pl.program_id(1)

    @pl.when(it == 0)
    def _():
        s_ref[...] = s0_ref[...]

    y_ref[...] = jnp.zeros(y_ref.shape, F32)
    ii = lax.broadcasted_iota(jnp.int32, (HEAD_DIM, HEAD_DIM), 0)
    jj = lax.broadcasted_iota(jnp.int32, (HEAD_DIM, HEAD_DIM), 1)
    eye = jnp.where(ii == jj, 1.0, 0.0).astype(F32)

    def step(t, carry):
        for n in range(nb):
            for h in range(HEADS):
                row = lambda ref: ref[n, h, pl.ds(t, 1), :]
                S = s_ref[n, h]
                sa = jnp.sum(S * row(kk_ref), axis=-1, keepdims=True)
                vcol = jnp.sum(eye * row(v_ref), axis=-1, keepdims=True)
                S = S * jnp.exp(row(w_ref)) - sa * row(b_ref) + vcol * row(k_ref)
                s_ref[n, h] = S
                ycol = jnp.sum(S * row(r_ref), axis=-1, keepdims=True)
                y_ref[n, h, pl.ds(t, 1), :] = jnp.sum(eye * ycol, axis=0, keepdims=True)
        return carry

    lax.fori_loop(0, n_steps, step, 0)

    for n in range(nb):
        for h in range(HEADS):
            y = y_ref[n, h]
            mu = jnp.mean(y, axis=-1, keepdims=True)
            yc = y - mu
            var = jnp.mean(yc * yc, axis=-1, keepdims=True)
            y_ref[n, h] = yc * lax.rsqrt(var + RWKV_GN_EPS) * lg_ref[h:h + 1, :] + lb_ref[h:h + 1, :] + bonus_ref[n, h]


def _rwkv_scan(r, w, k, v, kk, b, bonus, s0, lnx_g, lnx_b, t_real):
    B, H, T, _ = r.shape
    nb = 2
    tt = min(128, T)
    nt = T // tt
    n_steps = min(tt, t_real)
    blk = pl.BlockSpec((nb, H, tt, HEAD_DIM), lambda b_, i: (b_, 0, i, 0))
    sblk = pl.BlockSpec((nb, H, HEAD_DIM, HEAD_DIM), lambda b_, i: (b_, 0, 0, 0))
    return pl.pallas_call(
        functools.partial(_rwkv_scan_kernel, nb=nb, tt=tt, n_steps=n_steps),
        grid=(B // nb, nt),
        in_specs=[blk] * 7 + [sblk, _const_spec((H, HEAD_DIM)), _const_spec((H, HEAD_DIM))],
        out_specs=[blk, sblk],
        out_shape=[jax.ShapeDtypeStruct((B, H, T, HEAD_DIM), F32),
                   jax.ShapeDtypeStruct((B, H, HEAD_DIM, HEAD_DIM), F32)],
        compiler_params=_cparams(("parallel", "arbitrary")),
        name="rwkv_scan",
    )(r, w, k, v, kk, b, bonus, s0, lnx_g, lnx_b)


_RW_CHUNK = 64


def _rwkv_chunk_kernel(xr_ref, xk_ref, xv_ref, xwl_ref, xal_ref, xgl_ref, sh_ref, mu_ref, w0_ref, w2_ref, a0_ref,
                       a2_ref, g2_ref, kkp_ref, kap_ref, rk_ref, s0_ref, lg_ref, lb_ref,
                       y_ref, g_ref, s_ref, carry_ref):
    C = xr_ref.shape[1]

    @pl.when(pl.program_id(1) == 0)
    def _():
        s_ref[...] = s0_ref[...]
        carry_ref[...] = sh_ref[0]

    g_out, (r, lw, k, v, kk, b, bonus) = _rwkv_token_math(
        xr_ref, xk_ref, xv_ref, xwl_ref, xal_ref, xgl_ref, mu_ref, w0_ref, w2_ref, a0_ref, a2_ref, g2_ref,
        kkp_ref, kap_ref, rk_ref, carry_ref, C, C)
    g_ref[0] = g_out
    ti = lax.broadcasted_iota(jnp.int32, (C, C), 0)
    si = lax.broadcasted_iota(jnp.int32, (C, C), 1)
    incl = ti >= si
    strict = ti > si
    tri_b = jnp.where(incl, 1.0, 0.0).astype(BF16)
    n_fac = C.bit_length() - 1
    hs = range(HEADS)
    cs = [_dot_exact_lhs(tri_b, lw[h]) for h in hs]
    cs_last = [c[C - 1:C, :] for c in cs]
    AR = [jnp.concatenate([kk[h] * jnp.exp(cs[h] - lw[h]), r[h] * jnp.exp(cs[h])], axis=0) for h in hs]
    BK = []
    for h in hs:
        g_inv = jnp.exp(-cs[h])
        BK.append(jnp.concatenate([b[h] * g_inv, k[h] * g_inv], axis=0))
    G = [_dot3(AR[h], BK[h], _NT) for h in hs]
    S0 = [s_ref[0, h] for h in hs]
    ARS = [_dot3(AR[h], S0[h], _NT) for h in hs]
    LMV = [_dot3(jnp.concatenate([jnp.where(strict, G[h][0:C, C:2 * C], 0.0),
                                  jnp.where(incl, G[h][C:2 * C, C:2 * C], 0.0)], axis=0), v[h]) for h in hs]
    X = [ARS[h][0:C] + LMV[h][0:C] for h in hs]
    N = [-jnp.where(strict, G[h][0:C, 0:C], 0.0) for h in hs]
    for f in range(n_fac):
        X = [X[h] + _dot3(N[h], X[h]) for h in hs]
        if f + 1 < n_fac:
            N = [_dot3(N[h], N[h]) for h in hs]
    Y = [ARS[h][C:2 * C] + LMV[h][C:2 * C] - _dot3(jnp.where(incl, G[h][C:2 * C, 0:C], 0.0), X[h]) for h in hs]
    for h in hs:
        g_end = jnp.exp(cs_last[h] - cs[h])
        UV = jnp.concatenate([-X[h], v[h]], axis=0)
        BKg = jnp.concatenate([b[h] * g_end, k[h] * g_end], axis=0)
        s_ref[0, h] = S0[h] * jnp.exp(cs_last[h]) + _dot3(UV, BKg, _TN)
    out = []
    for h in hs:
        mu = jnp.mean(Y[h], axis=-1, keepdims=True)
        yc = Y[h] - mu
        var = jnp.mean(yc * yc, axis=-1, keepdims=True)
        out.append(yc * lax.rsqrt(var + RWKV_GN_EPS) * lg_ref[h:h + 1, :] + lb_ref[h:h + 1, :] + bonus[h])
    y_ref[0] = jnp.concatenate(out, axis=1)


def _rwkv_chunked(z, shift, s0, lp):
    B, T, _ = z.shape
    C = _RW_CHUNK
    wide = lambda col: pl.BlockSpec((1, C, GW), lambda b_, i: (b_, i, col // GW))
    thin = lambda col: pl.BlockSpec((1, C, 128), lambda b_, i: (b_, i, col // 128))
    sblk = pl.BlockSpec((1, HEADS, HEAD_DIM, HEAD_DIM), lambda b_, i: (b_, 0, 0, 0))
    tok = pl.BlockSpec((1, C, GW), lambda b_, i: (b_, i, 0))
    dw = 3 * GW + 384
    return pl.pallas_call(
        _rwkv_chunk_kernel,
        grid=(B, T // C),
        in_specs=[wide(C_R), wide(C_K), wide(C_V), thin(C_WL), thin(C_AL), thin(C_GL),
                  pl.BlockSpec((1, 1, dw), lambda b_, i: (b_, 0, 0)),
                  _const_spec((1, dw)), _const_spec((1, GW)), _const_spec((128, GW)), _const_spec((1, GW)),
                  _const_spec((128, GW)), _const_spec((128, GW)), _const_spec((1, GW)), _const_spec((1, GW)),
                  _const_spec((1, GW)), sblk, _const_spec((HEADS, HEAD_DIM)), _const_spec((HEADS, HEAD_DIM))],
        out_specs=[tok, tok, sblk],
        out_shape=[jax.ShapeDtypeStruct((B, T, GW), F32), jax.ShapeDtypeStruct((B, T, GW), F32),
                   jax.ShapeDtypeStruct((B, HEADS, HEAD_DIM, HEAD_DIM), F32)],
        scratch_shapes=[pltpu.VMEM((1, dw), F32)],
        compiler_params=_cparams(("parallel", "arbitrary")),
        name="rwkv_chunked",
    )(z, z, z, z, z, z, shift, lp["rw_mu"], lp["rw_w0"], lp["rw_w2"], lp["rw_a0"], lp["rw_a2"], lp["rw_g2"],
      lp["rw_kk"], lp["rw_ka"], lp["rw_rk"], s0, lp["rw_lnx_g"], lp["rw_lnx_b"])


def _compress_tail(F, cst):
    n = F.shape[0]
    nxt = pltpu.roll(F[:, HEAD_DIM:2 * HEAD_DIM], n - 1, 0)
    const = cst[0:1, 0:HEAD_DIM] + cst[1:2, HEAD_DIM:2 * HEAD_DIM]
    return F[:, 0:HEAD_DIM] + nxt + const


def _compress_kernel(x_ref, w_ref, pe_ref, o_ref):
    F = _dot3(x_ref[0, 0, 0], w_ref[0])
    cst = _dot3(pe_ref[0], w_ref[0])
    o_ref[0, 0, 0] = _compress_tail(F, cst)


def _compress_prompt(cmp_in, wcat, pe2):
    B, _, _, nc, _ = cmp_in.shape
    return pl.pallas_call(
        _compress_kernel,
        grid=(B, 2, 2),
        in_specs=[pl.BlockSpec((1, 1, 1, nc, 1024), lambda b, a, g: (b, a, g, 0, 0)),
                  pl.BlockSpec((1, 1024, 128), lambda b, a, g: (a, 0, 0)),
                  pl.BlockSpec((1, 8, 1024), lambda b, a, g: (a, 0, 0))],
        out_specs=pl.BlockSpec((1, 1, 1, nc, HEAD_DIM), lambda b, a, g: (b, a, g, 0, 0)),
        out_shape=jax.ShapeDtypeStruct((B, 2, 2, nc, HEAD_DIM), F32),
        compiler_params=_cparams(("parallel", "parallel", "parallel")),
        name="nsa_compress",
    )(cmp_in, wcat, pe2)


def _top_select(selT, j_f):
    v = selT
    sel = jnp.zeros(selT.shape, F32)
    for _ in range(N_SLC):
        m = jnp.max(v, axis=0, keepdims=True)
        first = jnp.min(jnp.where(v == m, j_f, 1e9), axis=0, keepdims=True)
        hit = j_f == first
        sel = jnp.where(hit, 1.0, sel)
        v = jnp.where(hit, -3e38, v)
    return sel


def _gated_heads(gts, g, o_cmp, o_slc, o_win, rows):
    out = []
    for m in range(Q_PER_KV):
        c0 = g * 3 * Q_PER_KV + m * 3
        sl = slice(m * rows, (m + 1) * rows)
        out.append(gts[:, c0:c0 + 1] * o_cmp[sl] + gts[:, c0 + 1:c0 + 2] * o_slc[sl] + gts[:, c0 + 2:c0 + 3] * o_win[sl])
    return out


_KT = 512


_LOG2E = 1.4426950408889634


def _nsa_prompt_kernel(qT_ref, gT_ref, ckv_ref, ks_ref, vsT_ref, kw_ref, vwT_ref, mt_ref, oT_ref, *, T):
    i = pl.program_id(1)
    start = i * Q_BLOCK
    nc = T // CMP_STRIDE
    ns = T // SLC_BLOCK
    cols = Q_PER_KV * Q_BLOCK
    gs = range(KV_HEADS)
    t_lane = start + lax.broadcasted_iota(jnp.int32, (1, Q_BLOCK), 1)
    t_cols = jnp.concatenate([t_lane] * Q_PER_KV, axis=1)
    gts = _sigmoid(gT_ref[0])
    j_col = lax.broadcasted_iota(jnp.int32, (ns, 1), 0)
    j_f = j_col.astype(F32)
    cur = t_lane >> 6
    forced = jnp.where(j_col == 0, 1.0, jnp.where(j_col == cur, 1.0, jnp.where(j_col == cur - 1, 1.0, 0.0)))
    QT = [jnp.concatenate([qT_ref[0, (g * Q_PER_KV + m) * HEAD_DIM:(g * Q_PER_KV + m + 1) * HEAD_DIM, :]
                           for m in range(Q_PER_KV)], axis=1) * (HEAD_DIM ** -0.5 * _LOG2E) for g in gs]
    normed = lambda a: a[0:HEAD_DIM] * (1.0 / a[HEAD_DIM:HEAD_DIM + 1])

    n_col = lax.broadcasted_iota(jnp.int32, (nc, 1), 0)
    bias_c = jnp.where(n_col * CMP_STRIDE + (2 * CMP_STRIDE - 1) <= t_cols, 0.0, NEG)
    any_c = jnp.where(t_cols >= 2 * CMP_STRIDE - 1, 1.0, 0.0)
    pcT = []
    for g in gs:
        s = _dot3(ckv_ref[0, 0, g], QT[g]) + bias_c
        e = jnp.exp2(s - jnp.max(s, axis=0, keepdims=True))
        pcT.append(e * (any_c / jnp.sum(e, axis=0, keepdims=True)))
    o_cmp = [_dot(ckv_ref[0, 1, g].astype(BF16), pcT[g].astype(BF16), _TN) for g in gs]
    rhs = []
    for g in gs:
        psT = pcT[g][:, 0:Q_BLOCK]
        for m in range(1, Q_PER_KV):
            psT = psT + pcT[g][:, m * Q_BLOCK:(m + 1) * Q_BLOCK]
        selT = _dot_exact_lhs(mt_ref[...], psT)
        selT = jnp.where(forced > 0.5, FORCED, jnp.where(j_col * SLC_BLOCK <= t_lane, selT, -FORCED))
        selbT = jnp.where(_top_select(selT, j_f) > 0.5, 0.0, NEG).astype(BF16)
        rhs.append(jnp.concatenate([QT[g].astype(BF16), jnp.concatenate([selbT] * Q_PER_KV, axis=1)], axis=0))

    def key_tile(kt, carry, causal):
        if causal:
            pos = kt * _KT + lax.broadcasted_iota(jnp.int32, (_KT, 1), 0)
            bias_t = jnp.where(pos <= t_cols, 0.0, NEG)
        out = []
        for g in gs:
            m_run, acc = carry[g]
            s = _dot(ks_ref[0, g, pl.ds(pl.multiple_of(kt * _KT, _KT), _KT), :], rhs[g])
            if causal:
                s = s + bias_t
            m_new = jnp.maximum(m_run, jnp.max(s, axis=0, keepdims=True))
            p = jnp.exp2(s - m_new)
            out.append((m_new, jnp.exp2(m_run - m_new) * acc + _dot(vsT_ref[0, g, kt], p.astype(BF16))))
        return tuple(out)

    init = tuple((jnp.full((1, cols), NEG, F32), jnp.zeros((2 * HEAD_DIM, cols), F32)) for _ in gs)
    last = start // _KT
    carry = lax.fori_loop(0, last, lambda kt, c: key_tile(kt, c, False), init)
    carry = key_tile(last, carry, True)
    o_slc = [normed(carry[g][1]) for g in gs]

    nwt = WINDOW // Q_BLOCK + 1
    wb = jnp.maximum(i - WINDOW // Q_BLOCK, 0)
    dt = t_cols - (wb * Q_BLOCK + lax.broadcasted_iota(jnp.int32, (nwt * Q_BLOCK, 1), 0))
    bias_w = jnp.where(dt >= 0, jnp.where(dt < WINDOW, 0.0, NEG), NEG)
    for g in gs:
        kwt = kw_ref[0, g, pl.ds(pl.multiple_of(wb * Q_BLOCK, Q_BLOCK), nwt * Q_BLOCK), :]
        sw = _dot(kwt, rhs[g][0:HEAD_DIM]) + bias_w
        pw = jnp.exp2(sw - jnp.max(sw, axis=0, keepdims=True))
        vwT = jnp.concatenate([vwT_ref[0, g, wb + u] for u in range(nwt)], axis=1)
        o_win = normed(_dot(vwT, pw.astype(BF16)))
        for m in range(Q_PER_KV):
            c0 = g * 3 * Q_PER_KV + m * 3
            sl = slice(m * Q_BLOCK, (m + 1) * Q_BLOCK)
            head = g * Q_PER_KV + m
            oT_ref[0, head * HEAD_DIM:(head + 1) * HEAD_DIM, :] = (
                gts[c0:c0 + 1] * o_cmp[g][:, sl] + gts[c0 + 1:c0 + 2] * o_slc[g][:, sl] + gts[c0 + 2:c0 + 3] * o_win[:, sl])


def _sel_matrix(n_rows, n_chunks):
    j = np.arange(n_rows)[:, None]
    c = np.arange(n_chunks)[None, :]
    per = SLC_BLOCK // CMP_STRIDE
    m = 0.5 * ((c // per == j).astype(np.float32) + ((c + 1) // per == j).astype(np.float32))
    return jnp.asarray(m, BF16)


def _nsa_prompt(qT, gT, ckv, ks, vsT, kw, vwT):
    B, _, T = qT.shape
    nc, ns = T // CMP_STRIDE, T // SLC_BLOCK
    nq = T // Q_BLOCK
    mt = _sel_matrix(ns, nc)
    per_b = lambda shape: pl.BlockSpec((1,) + shape, lambda b, i: (b,) + (0,) * len(shape))
    return pl.pallas_call(
        functools.partial(_nsa_prompt_kernel, T=T),
        grid=(B, nq),
        in_specs=[pl.BlockSpec((1, GW, Q_BLOCK), lambda b, i: (b, 0, i)),
                  pl.BlockSpec((1, gT.shape[1], Q_BLOCK), lambda b, i: (b, 0, i)),
                  per_b((2, KV_HEADS, nc, HEAD_DIM)),
                  per_b((KV_HEADS, T, HEAD_DIM + ns)),
                  per_b((KV_HEADS, T // _KT, 2 * HEAD_DIM, _KT)),
                  per_b((KV_HEADS, T, HEAD_DIM)),
                  per_b((KV_HEADS, nq, 2 * HEAD_DIM, Q_BLOCK)),
                  _const_spec((ns, nc))],
        out_specs=pl.BlockSpec((1, GW, Q_BLOCK), lambda b, i: (b, 0, i)),
        out_shape=jax.ShapeDtypeStruct((B, GW, T), F32),
        compiler_params=_cparams(("parallel", "arbitrary")),
        name="nsa_prompt",
    )(qT, gT, ckv, ks, vsT, kw, vwT, mt)


_PPS = 16
_SEL_ROWS = 384


def _nsa_sample_a_kernel(pt_ref, *refs, n_pages, past_len, t_pad):
    pages = refs[:_PPS]
    w_ref, pe_ref, q_ref, mt_ref = refs[_PPS:_PPS + 4]
    ocmp_ref, selm_ref = refs[_PPS + 4:_PPS + 6]
    x_ref = refs[_PPS + 6]
    s = pl.program_id(1)
    cpp = PAGE // CMP_STRIDE
    for k in range(_PPS):
        x_ref[:, pl.ds(pl.multiple_of((s * _PPS + k) * cpp, cpp), cpp), :] = pages[k][0]

    @pl.when(s == pl.num_programs(1) - 1)
    def _():
        nc = n_pages * cpp
        n_blocks = past_len // SLC_BLOCK + 1
        lanes = 128
        ps_rows = []
        for g in range(KV_HEADS):
            ck = _compress_tail(_dot3(x_ref[g], w_ref[0]), _dot3(pe_ref[0], w_ref[0]))
            cv = _compress_tail(_dot(x_ref[KV_HEADS + g].astype(BF16), w_ref[1].astype(BF16)), _dot3(pe_ref[1], w_ref[1]))
            Q = jnp.concatenate([q_ref[0, :, (g * Q_PER_KV + m) * HEAD_DIM:(g * Q_PER_KV + m + 1) * HEAD_DIM]
                                 for m in range(Q_PER_KV)], axis=0) * (HEAD_DIM ** -0.5)
            t_rows = past_len + (lax.broadcasted_iota(jnp.int32, (Q_PER_KV * t_pad, 1), 0) & (t_pad - 1))
            sc = _dot3(Q, ck, _NT)
            n_lane = lax.broadcasted_iota(jnp.int32, (1, nc), 1)
            mask_c = jnp.where(n_lane * CMP_STRIDE + (2 * CMP_STRIDE - 1) <= t_rows, 1.0, 0.0)
            pc = _masked_softmax(sc, mask_c)
            ocmp_ref[0, g] = _dot(pc.astype(BF16), cv.astype(BF16))
            ps_rows.append(pc[0:t_pad] + pc[t_pad:2 * t_pad] + pc[2 * t_pad:3 * t_pad] + pc[3 * t_pad:4 * t_pad])
        ps = jnp.concatenate(ps_rows + [jnp.zeros((lanes - KV_HEADS * t_pad, nc), F32)], axis=0)
        selT = _dot_exact_lhs(mt_ref[...], ps, _NT)
        j_col = lax.broadcasted_iota(jnp.int32, (_SEL_ROWS, 1), 0)
        t_lane = past_len + (lax.broadcasted_iota(jnp.int32, (1, lanes), 1) & (t_pad - 1))
        cur = t_lane >> 6
        forced = jnp.where(j_col == 0, 1.0, jnp.where(j_col == cur, 1.0, jnp.where(j_col == cur - 1, 1.0, 0.0)))
        selT = jnp.where(forced > 0.5, FORCED, jnp.where(j_col * SLC_BLOCK <= t_lane, selT, -FORCED))
        selT = jnp.where(j_col < n_blocks, selT, -3.0 * FORCED)
        selmT = _top_select(selT, j_col.astype(F32))
        r = lax.broadcasted_iota(jnp.int32, (lanes, lanes), 0)
        c = lax.broadcasted_iota(jnp.int32, (lanes, lanes), 1)
        eye = jnp.where(r == c, 1.0, 0.0).astype(BF16)
        selm = _dot(eye, selmT.astype(BF16), _NT)
        selm_ref[0] = selm[0:KV_HEADS * t_pad]


def _nsa_sample_a(page_table, cmp_pages, layer, wcat, pe2, z, past_len):
    B, t_pad, _ = z.shape
    n_pages = page_table.shape[1]
    nc = n_pages * (PAGE // CMP_STRIDE)
    mt = _sel_matrix(_SEL_ROWS, nc)
    page_spec = lambda k: pl.BlockSpec((None, 1, 2 * KV_HEADS, PAGE // CMP_STRIDE, 1024),
                                       lambda b, s, pt: (layer, pt[b, s * _PPS + k], 0, 0, 0))
    cst = lambda shape: pl.BlockSpec(shape, lambda b, s, pt: (0,) * len(shape))
    grid_spec = pltpu.PrefetchScalarGridSpec(
        num_scalar_prefetch=1,
        grid=(B, n_pages // _PPS),
        in_specs=[page_spec(k) for k in range(_PPS)] + [
            cst((2, 1024, 128)), cst((2, 8, 1024)),
            pl.BlockSpec((1, t_pad, GW), lambda b, s, pt: (b, 0, C_Q // GW)),
            cst((_SEL_ROWS, nc))],
        out_specs=[pl.BlockSpec((1, KV_HEADS, Q_PER_KV * t_pad, HEAD_DIM), lambda b, s, pt: (b, 0, 0, 0)),
                   pl.BlockSpec((1, KV_HEADS * t_pad, _SEL_ROWS), lambda b, s, pt: (b, 0, 0))],
        scratch_shapes=[pltpu.VMEM((2 * KV_HEADS, nc, 1024), F32)])
    return pl.pallas_call(
        functools.partial(_nsa_sample_a_kernel, n_pages=n_pages, past_len=past_len, t_pad=t_pad),
        grid_spec=grid_spec,
        out_shape=[jax.ShapeDtypeStruct((B, KV_HEADS, Q_PER_KV * t_pad, HEAD_DIM), F32),
                   jax.ShapeDtypeStruct((B, KV_HEADS * t_pad, _SEL_ROWS), F32)],
        compiler_params=_cparams(("parallel", "arbitrary")),
        name="nsa_sample_select",
    )(page_table, *([cmp_pages] * _PPS), wcat, pe2, z, mt)


def _nsa_sample_b_kernel(pt_ref, *refs, n_pages, past_len, t_pad, t_real):
    kpages = refs[:_PPS]
    vpages = refs[_PPS:2 * _PPS]
    q_ref, rows_ref, kwvw_ref, gt_ref, selm_ref, ocmp_ref, cw_ref = refs[2 * _PPS:2 * _PPS + 7]
    o_ref = refs[2 * _PPS + 7]
    m_ref, l_ref, acc_ref = refs[2 * _PPS + 8:]
    s = pl.program_id(1)
    n_steps = n_pages // _PPS
    rows = Q_PER_KV * t_pad
    t_rows = past_len + (lax.broadcasted_iota(jnp.int32, (rows, 1), 0) & (t_pad - 1))

    @pl.when(s == 0)
    def _():
        m_ref[...] = jnp.full(m_ref.shape, NEG, F32)
        l_ref[...] = jnp.zeros(l_ref.shape, F32)
        acc_ref[...] = jnp.zeros(acc_ref.shape, F32)

    def q_rows(g):
        Q = jnp.concatenate([q_ref[0, :, (g * Q_PER_KV + m) * HEAD_DIM:(g * Q_PER_KV + m + 1) * HEAD_DIM]
                             for m in range(Q_PER_KV)], axis=0) * (HEAD_DIM ** -0.5)
        return Q.astype(BF16)

    def online(g, sc, ok, values):
        sc = jnp.where(ok > 0.5, sc, NEG)
        m_run = m_ref[g]
        m_new = jnp.maximum(m_run, jnp.max(sc, axis=-1, keepdims=True))
        p = jnp.exp(sc - m_new) * ok
        corr = jnp.exp(m_run - m_new)
        l_ref[g] = corr * l_ref[g] + jnp.sum(p, axis=-1, keepdims=True)
        acc_ref[g] = corr * acc_ref[g] + values(p.astype(BF16))
        m_ref[g] = m_new

    @pl.when(s < n_steps)
    def _():
        keys = _PPS * PAGE
        jr = lax.broadcasted_iota(jnp.int32, (_SEL_ROWS, keys), 0)
        cl = lax.broadcasted_iota(jnp.int32, (_SEL_ROWS, keys), 1)
        expand = jnp.where(jr == s * (keys // SLC_BLOCK) + (cl >> 6), 1.0, 0.0).astype(BF16)
        mv = _dot(selm_ref[0].astype(BF16), expand)
        pos = s * keys + lax.broadcasted_iota(jnp.int32, (1, keys), 1)
        for g in range(KV_HEADS):
            Qb = q_rows(g)
            sc = jnp.concatenate([_dot(Qb, kpages[k][g].astype(BF16)) for k in range(_PPS)], axis=1)
            mvg = jnp.concatenate([mv[g * t_pad:(g + 1) * t_pad]] * Q_PER_KV, axis=0)
            ok = jnp.where(pos <= t_rows, mvg, 0.0)

            def values(pb, g=g):
                out = _dot(pb[:, 0:PAGE], vpages[0][g].astype(BF16), _NT)
                for k in range(1, _PPS):
                    out = out + _dot(pb[:, k * PAGE:(k + 1) * PAGE], vpages[k][g].astype(BF16), _NT)
                return out

            online(g, sc, ok, values)

    @pl.when(s == n_steps)
    def _():
        gts = _sigmoid(gt_ref[0])
        last_blk = past_len // SLC_BLOCK
        r_lane = lax.broadcasted_iota(jnp.int32, (1, t_pad), 1)
        pieces = []
        for g in range(KV_HEADS):
            Qb = q_rows(g)
            k_new = rows_ref[0, :, (2 * KV_HEADS + g) * HEAD_DIM:(2 * KV_HEADS + g + 1) * HEAD_DIM]
            v_new = rows_ref[0, :, (3 * KV_HEADS + g) * HEAD_DIM:(3 * KV_HEADS + g + 1) * HEAD_DIM]
            sc = _dot(Qb, k_new.astype(BF16), _NT)
            sel_last = jnp.concatenate([selm_ref[0, g * t_pad:(g + 1) * t_pad, last_blk:last_blk + 1]] * Q_PER_KV, axis=0)
            vis = jnp.where(past_len + r_lane <= t_rows, jnp.where(r_lane < t_real, 1.0, 0.0), 0.0)
            online(g, sc, vis * sel_last, lambda pb, v_new=v_new: _dot(pb, v_new.astype(BF16)))
            o_slc = acc_ref[g] / jnp.maximum(l_ref[g], 1e-30)
            wrows = cw_ref.shape[3]
            kw_new = kwvw_ref[0, :, g * HEAD_DIM:(g + 1) * HEAD_DIM]
            vw_new = kwvw_ref[0, :, (KV_HEADS + g) * HEAD_DIM:(KV_HEADS + g + 1) * HEAD_DIM]
            sw = jnp.concatenate([_dot(Qb, cw_ref[0, g].astype(BF16)),
                                  _dot(Qb, kw_new.astype(BF16), _NT)], axis=1)
            idx = lax.broadcasted_iota(jnp.int32, (1, wrows + t_pad), 1)
            spos = past_len - wrows + idx
            dt = t_rows - spos
            real = jnp.where(idx < wrows + t_real, 1.0, 0.0)
            mask_w = jnp.where(dt >= 0, jnp.where(dt < WINDOW, real, 0.0), 0.0)
            pw = _masked_softmax(sw, mask_w).astype(BF16)
            o_win = _dot(pw[:, 0:wrows], cw_ref[1, g].astype(BF16), _NT) + _dot(pw[:, wrows:], vw_new.astype(BF16))
            pieces += _gated_heads(gts, g, ocmp_ref[0, g], o_slc, o_win, t_pad)
        o_ref[0] = jnp.concatenate(pieces, axis=1)


def _nsa_sample_b(page_table, cache_t, win_t, layer, z, selm, ocmp, past_len, t_real):
    B, t_pad, _ = z.shape
    n_pages = page_table.shape[1]
    n_steps = n_pages // _PPS
    rows = Q_PER_KV * t_pad
    wrows = win_t.shape[5]
    pidx = lambda b, s, pt, k: pt[b, jnp.minimum(s, n_steps - 1) * _PPS + k]
    page = lambda which, k: pl.BlockSpec((None, None, None, KV_HEADS, HEAD_DIM, PAGE),
                                         lambda b, s, pt: (layer, pidx(b, s, pt, k), which, 0, 0, 0))
    kspec = lambda k: page(2, k)
    vspec = lambda k: page(3, k)
    zspec = lambda col, w: pl.BlockSpec((1, t_pad, w), lambda b, s, pt: (b, 0, col // w))
    per_b = lambda shape: pl.BlockSpec((1,) + shape, lambda b, s, pt: (b,) + (0,) * len(shape))
    grid_spec = pltpu.PrefetchScalarGridSpec(
        num_scalar_prefetch=1,
        grid=(B, n_steps + 1),
        in_specs=[kspec(k) for k in range(_PPS)] + [vspec(k) for k in range(_PPS)] + [
            zspec(C_Q, GW), zspec(C_ROWS, GW), zspec(C_KW, 256), zspec(C_GATES, 256),
            per_b((KV_HEADS * t_pad, _SEL_ROWS)), per_b((KV_HEADS, rows, HEAD_DIM)),
            pl.BlockSpec((None, None, 2, KV_HEADS, HEAD_DIM, wrows), lambda b, s, pt: (layer, b, 0, 0, 0, 0))],
        out_specs=pl.BlockSpec((1, t_pad, GW), lambda b, s, pt: (b, 0, 0)),
        scratch_shapes=[pltpu.VMEM((KV_HEADS, rows, 1), F32), pltpu.VMEM((KV_HEADS, rows, 1), F32),
                        pltpu.VMEM((KV_HEADS, rows, HEAD_DIM), F32)])
    return pl.pallas_call(
        functools.partial(_nsa_sample_b_kernel, n_pages=n_pages, past_len=past_len, t_pad=t_pad, t_real=t_real),
        grid_spec=grid_spec,
        out_shape=jax.ShapeDtypeStruct((B, t_pad, GW), F32),
        compiler_params=_cparams(("parallel", "arbitrary")),
        name="nsa_sample_attend",
    )(page_table, *([cache_t] * (2 * _PPS)), z, z, z, z, selm, ocmp, win_t)


def _route(h, w, b):
    logits = _dot3(h, w) + b
    lane = lax.broadcasted_iota(jnp.int32, logits.shape, 1).astype(F32)
    big = 1e6
    is_g = lane < N_GROUPS
    gl = jnp.where(is_g, logits, NEG)
    gmax = jnp.max(gl, axis=-1, keepdims=True)
    gsel = jnp.min(jnp.where(is_g, jnp.where(gl == gmax, lane, big), big), axis=-1, keepdims=True)
    gp = 1.0 / jnp.sum(jnp.where(is_g, jnp.exp(gl - gmax), 0.0), axis=-1, keepdims=True)
    lo = N_GROUPS + gsel * EXP_PER_GROUP
    in_grp = jnp.where(lane >= lo, jnp.where(lane < lo + EXP_PER_GROUP, 1.0, 0.0), 0.0)
    el = jnp.where(in_grp > 0.5, logits, NEG)
    e = jnp.exp(el - jnp.max(el, axis=-1, keepdims=True)) * in_grp
    p = e / jnp.sum(e, axis=-1, keepdims=True)
    pm = jnp.where(in_grp > 0.5, p, -1.0)
    v1 = jnp.max(pm, axis=-1, keepdims=True)
    i1 = jnp.min(jnp.where(pm == v1, lane, big), axis=-1, keepdims=True)
    pm2 = jnp.where(lane == i1, -1.0, pm)
    v2 = jnp.max(pm2, axis=-1, keepdims=True)
    i2 = jnp.min(jnp.where(pm2 == v2, lane, big), axis=-1, keepdims=True)
    tot = v1 + v2
    return jnp.where(lane == 0, i1 - N_GROUPS,
                     jnp.where(lane == 1, i2 - N_GROUPS,
                               jnp.where(lane == 2, v1 / tot * gp, jnp.where(lane == 3, v2 / tot * gp, 0.0))))


def _proj_out_kernel(x_ref, ya_ref, yb_ref, yc_ref, yd_ref, g_ref, w_ref, b_ref, lg_ref, lb_ref, rw_ref, rb_ref,
                     h_ref, r_ref, *, alpha):
    yd = yd_ref[...] * g_ref[...]
    acc = _dot(ya_ref[...].astype(BF16), w_ref[0:GW, :])
    acc = acc + _dot(yb_ref[...].astype(BF16), w_ref[GW:2 * GW, :])
    acc = acc + _dot(yc_ref[...].astype(BF16), w_ref[2 * GW:3 * GW, :])
    acc = acc + _dot(yd.astype(BF16), w_ref[3 * GW:4 * GW, :])
    h = _layer_norm(alpha * x_ref[...] + acc + b_ref[...], lg_ref[...], lb_ref[...], LN_EPS)
    h_ref[...] = h
    r_ref[...] = _route(h, rw_ref[...], rb_ref[...])


def _proj_out(x2d, ya, yb, yc, yd, g, lp, alpha):
    n = x2d.shape[0]
    tm = min(256, n)
    row = lambda w: pl.BlockSpec((tm, w), lambda i: (i, 0))
    return pl.pallas_call(
        functools.partial(_proj_out_kernel, alpha=alpha),
        grid=(n // tm,),
        in_specs=[row(D_MODEL), row(GW), row(GW), row(GW), row(GW), row(GW),
                  _const_spec((D_MODEL, D_MODEL)), _const_spec((1, D_MODEL)), _const_spec((1, D_MODEL)),
                  _const_spec((1, D_MODEL)), _const_spec((D_MODEL, 128)), _const_spec((1, 128))],
        out_specs=[row(D_MODEL), row(128)],
        out_shape=[jax.ShapeDtypeStruct((n, D_MODEL), F32), jax.ShapeDtypeStruct((n, 128), F32)],
        compiler_params=_cparams(("parallel",)),
        name="proj_out_ln1",
    )(x2d, ya, yb, yc, yd, g, lp["w_out"], lp["b_out"], lp["ln1_g"], lp["ln1_b"], lp["router_w"], lp["router_b"])


def _for_rows(n_rows, body):
    if isinstance(n_rows, int):
        lax.fori_loop(0, n_rows, lambda r, c: (body(r), c)[1], 0, unroll=8)
        return
    groups = n_rows // 8

    def group(i, c):
        for u in range(8):
            body(i * 8 + u)
        return c

    lax.fori_loop(0, groups, group, 0)
    lax.fori_loop(groups * 8, n_rows, lambda r, c: (body(r), c)[1], 0)


def _issue_rows(idx_fn, n_rows, src_ref, dst_ref, sem):
    _for_rows(n_rows, lambda r: pltpu.make_async_copy(src_ref.at[pl.ds(idx_fn(r), 1), :],
                                                      dst_ref.at[pl.ds(r, 1), :], sem).start())


def _wait_rows(n_rows, src_ref, dst_ref, sem):
    _for_rows(n_rows, lambda r: pltpu.make_async_copy(src_ref.at[pl.ds(0, 1), :],
                                                      dst_ref.at[pl.ds(r, 1), :], sem).wait())


def _moe_kernel(te_ref, nu_ref, src_ref, cnt_ref, first_ref, nxt_ref, wslot_ref, tok_ref,
                h_ref, wg_ref, wu_ref, wd_ref, o_ref,
                xbuf_ref, sems, wgs_ref, wus_ref, wds_ref, wsems, wgb_ref, wub_ref, wdb_ref, *, layer):
    t = pl.program_id(0)
    tm = o_ref.shape[0]
    slot = t % 2
    n_used = nu_ref[0]

    def fetch(tile, to_slot):
        src = src_ref[tile]
        _issue_rows(lambda r: tok_ref[src + r], cnt_ref[tile], h_ref, xbuf_ref.at[to_slot], sems.at[to_slot])

    def weight_copies(e, s):
        return (pltpu.make_async_copy(wg_ref.at[layer, e], wgs_ref.at[s], wsems.at[s, 0]),
                pltpu.make_async_copy(wu_ref.at[layer, e], wus_ref.at[s], wsems.at[s, 1]),
                pltpu.make_async_copy(wd_ref.at[layer, e], wds_ref.at[s], wsems.at[s, 2]))

    @pl.when(t == 0)
    def _():
        for c in weight_copies(te_ref[0], 0):
            c.start()
        xbuf_ref[...] = jnp.zeros(xbuf_ref.shape, F32)
        fetch(0, 0)

    @pl.when(t + 1 < n_used)
    def _():
        fetch(t + 1, 1 - slot)

    @pl.when(first_ref[t] == 1)
    def _():
        s = wslot_ref[t]
        for c in weight_copies(te_ref[t], s):
            c.wait()

        @pl.when(nxt_ref[t] >= 0)
        def _():
            for c in weight_copies(nxt_ref[t], 1 - s):
                c.start()

        wgb_ref[...] = wgs_ref[s].astype(BF16)
        wub_ref[...] = wus_ref[s].astype(BF16)
        wdb_ref[...] = wds_ref[s].astype(BF16)

    @pl.when(jnp.logical_or(t == 0, t < n_used))
    def _():
        _wait_rows(cnt_ref[t], h_ref, xbuf_ref.at[slot], sems.at[slot])
        xb = xbuf_ref[slot].astype(BF16)
        hg = _silu(_dot(xb, wgb_ref[...])) * _dot(xb, wub_ref[...])
        o_ref[...] = _dot(hg.astype(BF16), wdb_ref[...])

    @pl.when(jnp.logical_and(t > 0, t >= n_used))
    def _():
        o_ref[...] = jnp.zeros(o_ref.shape, F32)


def _moe_experts(sched, h, w_gate, w_up, w_down, layer):
    tm = MOE_TILE
    r = sched[0].shape[0] * tm
    hbm = pl.BlockSpec(memory_space=pl.ANY)
    grid_spec = pltpu.PrefetchScalarGridSpec(
        num_scalar_prefetch=len(sched),
        grid=(r // tm,),
        in_specs=[hbm, hbm, hbm, hbm],
        out_specs=pl.BlockSpec((tm, D_MODEL), lambda t, *_: (t, 0)),
        scratch_shapes=[pltpu.VMEM((2, tm, D_MODEL), F32), pltpu.SemaphoreType.DMA((2,)),
                        pltpu.VMEM((2, D_MODEL, D_EXPERT), F32), pltpu.VMEM((2, D_MODEL, D_EXPERT), F32),
                        pltpu.VMEM((2, D_EXPERT, D_MODEL), F32), pltpu.SemaphoreType.DMA((2, 3)),
                        pltpu.VMEM((D_MODEL, D_EXPERT), BF16), pltpu.VMEM((D_MODEL, D_EXPERT), BF16),
                        pltpu.VMEM((D_EXPERT, D_MODEL), BF16)])
    return pl.pallas_call(
        functools.partial(_moe_kernel, layer=layer),
        grid_spec=grid_spec,
        out_shape=jax.ShapeDtypeStruct((r, D_MODEL), F32),
        compiler_params=_cparams(("arbitrary",), disable_bounds_checks=True),
        name="moe_experts",
    )(*sched, h, w_gate, w_up, w_down)


def _moe_schedule(eid):
    n = eid.shape[0]
    tm = MOE_TILE
    n_tiles = (2 * n + N_EXPERTS * (tm - 1)) // tm + 1
    flat_e = eid.reshape(-1)
    order = jnp.argsort(flat_e, stable=True).astype(jnp.int32)
    rank = jnp.argsort(order).astype(jnp.int32)
    onehot = (flat_e[:, None] == jnp.arange(N_EXPERTS, dtype=jnp.int32)[None, :]).astype(jnp.int32)
    counts = jnp.sum(onehot, axis=0)
    padded = (counts + tm - 1) // tm * tm
    pad_end = jnp.cumsum(padded)
    g_start = pad_end - padded
    u_start = jnp.cumsum(counts) - counts
    pos = (rank + jnp.sum(onehot * (g_start - u_start)[None, :], axis=1)).reshape(n, 2)
    n_used = (pad_end[-1] // tm).astype(jnp.int32)
    tile_id = jnp.arange(n_tiles, dtype=jnp.int32)
    tile_expert = jnp.sum((pad_end[None, :] <= (tile_id * tm)[:, None]).astype(jnp.int32), axis=1)
    tile_expert = jnp.minimum(tile_expert, N_EXPERTS - 1)
    in_tile = (tile_expert[:, None] == jnp.arange(N_EXPERTS, dtype=jnp.int32)[None, :]).astype(jnp.int32)
    pick = lambda table: jnp.sum(in_tile * table[None, :], axis=1)
    off = tile_id * tm - pick(g_start)
    tile_src = (pick(u_start) + off).astype(jnp.int32)
    tile_cnt = jnp.clip(pick(counts) - off, 0, tm).astype(jnp.int32)
    live = tile_id < n_used
    tile_cnt = jnp.where(live, tile_cnt, 0)
    ar = jnp.arange(N_EXPERTS, dtype=jnp.int32)
    used = counts > 0
    nxt_e = jnp.min(jnp.where(jnp.logical_and(ar[None, :] > ar[:, None], used[None, :]), ar[None, :], N_EXPERTS), axis=1)
    nxt_e = jnp.where(nxt_e < N_EXPERTS, nxt_e, -1)
    first = jnp.logical_and(live, off == 0).astype(jnp.int32)
    wslot = (pick(jnp.cumsum(used.astype(jnp.int32)) - 1) % 2).astype(jnp.int32)
    sched = (tile_expert.astype(jnp.int32), n_used.reshape(1), tile_src, tile_cnt, first,
             pick(nxt_e).astype(jnp.int32), wslot, order // 2)
    return sched, pos


def _ln2_kernel(idx_ref, h_ref, rt_ref, ys_ref, g_ref, b_ref, o_ref, ybuf_ref, sems, *, alpha):
    i = pl.program_id(0)
    tm = h_ref.shape[0]
    slot = i % 2

    def fetch(tile, to_slot):
        base = tile * 2 * tm
        _issue_rows(lambda r: idx_ref[base + r], 2 * tm, ys_ref, ybuf_ref.at[to_slot], sems.at[to_slot])

    @pl.when(i == 0)
    def _():
        fetch(0, 0)

    @pl.when(i + 1 < pl.num_programs(0))
    def _():
        fetch(i + 1, 1 - slot)

    _wait_rows(2 * tm, ys_ref, ybuf_ref.at[slot], sems.at[slot])
    w0 = rt_ref[:, 2:3]
    w1 = rt_ref[:, 3:4]
    moe = w0 * ybuf_ref[slot, 0:tm, :] + w1 * ybuf_ref[slot, tm:2 * tm, :]
    o_ref[...] = _layer_norm(alpha * h_ref[...] + moe, g_ref[...], b_ref[...], LN_EPS)


def _combine_ln2(h, routed, pos, ys, g, b, alpha):
    n = h.shape[0]
    tm = min(128, n)
    nt = n // tm
    idx = jnp.transpose(pos.reshape(nt, tm, 2), (0, 2, 1)).reshape(-1)
    grid_spec = pltpu.PrefetchScalarGridSpec(
        num_scalar_prefetch=1,
        grid=(nt,),
        in_specs=[pl.BlockSpec((tm, D_MODEL), lambda i, idx_ref: (i, 0)),
                  pl.BlockSpec((tm, 128), lambda i, idx_ref: (i, 0)),
                  pl.BlockSpec(memory_space=pl.ANY),
                  pl.BlockSpec((1, D_MODEL), lambda i, idx_ref: (0, 0)),
                  pl.BlockSpec((1, D_MODEL), lambda i, idx_ref: (0, 0))],
        out_specs=pl.BlockSpec((tm, D_MODEL), lambda i, idx_ref: (i, 0)),
        scratch_shapes=[pltpu.VMEM((2, 2 * tm, D_MODEL), F32), pltpu.SemaphoreType.DMA((2,))])
    return pl.pallas_call(
        functools.partial(_ln2_kernel, alpha=alpha),
        grid_spec=grid_spec,
        out_shape=jax.ShapeDtypeStruct((n, D_MODEL), F32),
        compiler_params=_cparams(("arbitrary",), disable_bounds_checks=True),
        name="combine_ln2",
    )(idx, h, routed, ys, g, b)


def _relayout_cols(w):
    pad = lambda n: jnp.zeros(w.shape[:-1] + (n,), w.dtype)
    d0 = 3864
    return jnp.concatenate([
        w[..., 0:2328], pad(C_BG - 2328),
        w[..., 2328:3864],
        w[..., d0:d0 + 1536],
        w[..., d0 + 1536:d0 + 1568], pad(96),
        w[..., d0 + 1568:d0 + 1600], pad(96),
        w[..., d0 + 1600:d0 + 1696], pad(N_Z - C_GL - 96)], axis=-1)


def _relayout_d(v):
    pad = lambda n: jnp.zeros(v.shape[:-1] + (n,), v.dtype)
    return jnp.concatenate([v[..., 0:1536], v[..., 1536:1568], pad(96), v[..., 1568:1600], pad(96),
                            v[..., 1600:1696], pad(32)], axis=-1)


def _shift_cols(zrow):
    return jnp.concatenate([zrow[..., C_R:C_R + 1536], zrow[..., C_WL:C_WL + 32], zrow[..., C_AL:C_AL + 32],
                            zrow[..., C_GL:C_GL + 96]], axis=-1)


def _pad_rows(w, rows):
    return jnp.concatenate([w, jnp.zeros((rows - w.shape[0],) + w.shape[1:], w.dtype)], axis=0)


def _layer_params(l, P):
    wc = P["nsa_wc"][l]
    half = 2 * CMP_STRIDE // 2
    wcat = jnp.concatenate([wc[:, :half].reshape(2, half * HEAD_DIM, HEAD_DIM),
                            wc[:, half:].reshape(2, half * HEAD_DIM, HEAD_DIM)], axis=-1)
    pe = P["nsa_pe"][l]
    pe2 = jnp.stack([pe[:, :half].reshape(2, half * HEAD_DIM), pe[:, half:].reshape(2, half * HEAD_DIM)], axis=1)
    pe2 = jnp.concatenate([pe2, jnp.zeros((2, 6, half * HEAD_DIM), F32)], axis=1)
    router_w = jnp.concatenate([P["moe_w_grp"][l], P["moe_w_rte"][l],
                                jnp.zeros((D_MODEL, 128 - N_GROUPS - N_EXPERTS), F32)], axis=1)
    router_b = jnp.concatenate([P["moe_b_grp"][l], P["moe_b_rte"][l],
                                jnp.zeros((128 - N_GROUPS - N_EXPERTS,), F32)])[None, :]
    row = lambda name: P[name][l][None, :]
    return dict(
        w_in=_relayout_cols(P["w_in"][l]).astype(BF16), b_in=_relayout_cols(P["b_in"][l])[None, :],
        w_out=P["w_out"][l].astype(BF16), b_out=row("b_out"),
        ln1_g=row("ln1_g"), ln1_b=row("ln1_b"), ln2_g=row("ln2_g"), ln2_b=row("ln2_b"),
        conv_a_w=P["conv_a_w"][l], conv_a_b=row("conv_a_b"), ln_a_g=row("ln_a_g"), ln_a_b=row("ln_a_b"),
        conv_c_w=P["conv_c_w"][l], wcat=wcat, pe2=pe2,
        rw_mu=_relayout_d(P["rw_mu"][l])[None, :], rw_w0=row("rw_w0"), rw_w2=_pad_rows(P["rw_w2"][l], 128),
        rw_a0=row("rw_a0"), rw_a2=_pad_rows(P["rw_a2"][l], 128), rw_g2=_pad_rows(P["rw_g2"][l], 128),
        rw_kk=row("rw_kk"), rw_ka=row("rw_ka"), rw_rk=P["rw_rk"][l].reshape(1, GW),
        rw_lnx_g=P["rw_lnx_g"][l].reshape(HEADS, HEAD_DIM), rw_lnx_b=P["rw_lnx_b"][l].reshape(HEADS, HEAD_DIM),
        router_w=router_w, router_b=router_b)


def _mix_rwkv(z, shift, s0, lp, t_real):
    B, T, _ = z.shape
    if t_real == T and T % _RW_CHUNK == 0:
        return _rwkv_chunked(z, shift, s0, lp)
    r, w, k, v, kk, b, bonus, g = _rwkv_prep(z, shift, lp["rw_mu"], lp["rw_w0"], lp["rw_w2"], lp["rw_a0"],
                                             lp["rw_a2"], lp["rw_g2"], lp["rw_kk"], lp["rw_ka"], lp["rw_rk"], t_real)
    y, s_new = _rwkv_scan(r, w, k, v, kk, b, bonus, s0, lp["rw_lnx_g"], lp["rw_lnx_b"], t_real)
    yd = jnp.transpose(y, (0, 2, 1, 3)).reshape(B, T, GW)
    return yd, g, s_new


def _nsa_prompt_group(z, lp):
    B, T, _ = z.shape
    kv6 = z[:, :, C_ROWS:C_GATES].reshape(B, T, 6, KV_HEADS, HEAD_DIM)
    nc, ns = T // CMP_STRIDE, T // SLC_BLOCK
    cmp_in = kv6[:, :, 0:2].reshape(B, nc, CMP_STRIDE, 2, KV_HEADS, HEAD_DIM)
    cmp_in = jnp.transpose(cmp_in, (0, 3, 4, 1, 2, 5)).reshape(B, 2, KV_HEADS, nc, CMP_STRIDE * HEAD_DIM)
    ckv = _compress_prompt(cmp_in, lp["wcat"], lp["pe2"])
    rows_of = lambda x: jnp.transpose(x.astype(BF16), (0, 2, 1, 3))
    block_id = jnp.arange(T, dtype=jnp.int32) // SLC_BLOCK
    onehot = (block_id[:, None] == jnp.arange(ns, dtype=jnp.int32)[None, :]).astype(BF16)
    ks = jnp.concatenate([rows_of(kv6[:, :, 2]), jnp.broadcast_to(onehot, (B, KV_HEADS, T, ns))], axis=-1)

    def value_tiles(x, tile):
        vT = jnp.transpose(x.astype(BF16), (0, 2, 3, 1))
        aug = jnp.concatenate([vT, jnp.ones((B, KV_HEADS, 1, T), BF16), jnp.zeros((B, KV_HEADS, HEAD_DIM - 1, T), BF16)], axis=2)
        return jnp.transpose(aug.reshape(B, KV_HEADS, 2 * HEAD_DIM, T // tile, tile), (0, 1, 3, 2, 4))

    qT = jnp.transpose(z[:, :, C_Q:C_Q + GW], (0, 2, 1))
    gT = jnp.transpose(z[:, :, C_GATES:C_GATES + 32], (0, 2, 1))
    ybT = _nsa_prompt(qT, gT, ckv, ks, value_tiles(kv6[:, :, 3], _KT), rows_of(kv6[:, :, 4]), value_tiles(kv6[:, :, 5], Q_BLOCK))
    rows = kv6[:, :, 0:4]
    win = kv6[:, T - min(WINDOW, T):, 4:6]
    return jnp.transpose(ybT, (0, 2, 1)), rows, win


def _cache_views(cache_nsa, cache_win):
    L, n_pool = cache_nsa.shape[0], cache_nsa.shape[1]
    cpp = PAGE // CMP_STRIDE
    cache_t = jnp.transpose(cache_nsa, (0, 1, 3, 4, 5, 2))
    win_t = jnp.transpose(cache_win, (0, 1, 3, 4, 5, 2))
    cmp_pages = cache_t[:, :, 0:2].reshape(L, n_pool, 2, KV_HEADS, HEAD_DIM, cpp, CMP_STRIDE)
    cmp_pages = jnp.transpose(cmp_pages, (0, 1, 2, 3, 5, 6, 4)).reshape(L, n_pool, 2 * KV_HEADS, cpp, CMP_STRIDE * HEAD_DIM)
    return cache_t, win_t, cmp_pages


def _nsa_sample_group(z, views, layer, page_table, win_l, lp, t_real):
    B, t_pad, _ = z.shape
    cache_t, win_t, cmp_pages = views
    past_len = page_table.shape[1] * PAGE
    ocmp, selm = _nsa_sample_a(page_table, cmp_pages, layer, lp["wcat"], lp["pe2"], z, past_len)
    yb = _nsa_sample_b(page_table, cache_t, win_t, layer, z, selm, ocmp, past_len, t_real)
    zr = z[:, :t_real]
    rows = zr[:, :, C_ROWS:C_KW].reshape(B, t_real, 4, KV_HEADS, HEAD_DIM)
    new_win = zr[:, :, C_KW:C_GATES].reshape(B, t_real, 2, KV_HEADS, HEAD_DIM)
    win = jnp.concatenate([win_l, new_win], axis=1)[:, t_real:]
    return yb, rows, win


def _group_layer(x, lp, alpha, nsa_fn, conv_a_buf, conv_c_buf, rw_state, rw_shift, t_real):
    B, T, _ = x.shape
    z = _proj_in(x.reshape(B * T, D_MODEL), lp["w_in"], lp["b_in"]).reshape(B, T, N_Z)
    ya, yc, conv_a_new, conv_c_new = _conv_mix(z, conv_a_buf, conv_c_buf, lp["conv_a_w"], lp["conv_a_b"],
                                               lp["ln_a_g"], lp["ln_a_b"], lp["conv_c_w"], t_real)
    yb, nsa_rows, win_new = nsa_fn(z)
    yd, g, rw_state_new = _mix_rwkv(z, _relayout_d(rw_shift)[:, None, :], rw_state, lp, t_real)
    flat = lambda a: a.reshape(B * T, a.shape[-1])
    h, routed = _proj_out(flat(x), flat(ya), flat(yb), flat(yc), flat(yd), flat(g), lp, alpha)
    shift_new = _shift_cols(z[:, t_real - 1])
    return (h, routed), (nsa_rows, win_new, conv_a_new, conv_c_new, rw_state_new, shift_new)


def _moe_ln2(groups, lp, P, layer, alpha):
    h_list = [g[0] for g in groups]
    routed = [g[1] for g in groups]
    eid = jnp.concatenate([r[:, 0:2] for r in routed], axis=0).astype(jnp.int32)
    h_all = jnp.concatenate(h_list, axis=0)
    sched, pos = _moe_schedule(eid)
    ys = _moe_experts(sched, h_all, P["moe_w_gate"], P["moe_w_up"], P["moe_w_down"], layer)
    outs = []
    off = 0
    for h, rt in zip(h_list, routed):
        n = h.shape[0]
        outs.append(_combine_ln2(h, rt, pos[off:off + n], ys, lp["ln2_g"], lp["ln2_b"], alpha))
        off += n
    return outs


def kernel(x_prompt, x_sample, cache_nsa, cache_win, state_conv_a, state_conv_c, state_rwkv, state_rwkv_shift, page_table, w_in, b_in, w_out, b_out, ln1_g, ln1_b, ln2_g, ln2_b, conv_a_w, conv_a_b, ln_a_g, ln_a_b, conv_c_w, nsa_wc, nsa_pe, rw_mu, rw_w0, rw_w2, rw_a0, rw_a2, rw_g2, rw_kk, rw_ka, rw_rk, rw_lnx_g, rw_lnx_b, moe_w_grp, moe_b_grp, moe_w_rte, moe_b_rte, moe_w_gate, moe_w_up, moe_w_down):
    P = dict(w_in=w_in, b_in=b_in, w_out=w_out, b_out=b_out, ln1_g=ln1_g, ln1_b=ln1_b, ln2_g=ln2_g, ln2_b=ln2_b,
             conv_a_w=conv_a_w, conv_a_b=conv_a_b, ln_a_g=ln_a_g, ln_a_b=ln_a_b, conv_c_w=conv_c_w,
             nsa_wc=nsa_wc, nsa_pe=nsa_pe, rw_mu=rw_mu, rw_w0=rw_w0, rw_w2=rw_w2, rw_a0=rw_a0, rw_a2=rw_a2,
             rw_g2=rw_g2, rw_kk=rw_kk, rw_ka=rw_ka, rw_rk=rw_rk, rw_lnx_g=rw_lnx_g, rw_lnx_b=rw_lnx_b,
             moe_w_grp=moe_w_grp, moe_b_grp=moe_b_grp, moe_w_rte=moe_w_rte, moe_b_rte=moe_b_rte,
             moe_w_gate=moe_w_gate, moe_w_up=moe_w_up, moe_w_down=moe_w_down)
    depth = w_in.shape[0]
    alpha = (2.0 * depth) ** 0.25
    bp, tp, _ = x_prompt.shape
    bs, ts, _ = x_sample.shape
    ts_pad = 8
    xp = x_prompt
    xs = jnp.concatenate([x_sample, jnp.zeros((bs, ts_pad - ts, D_MODEL), F32)], axis=1)
    outs_p, outs_s = [], []
    views = _cache_views(cache_nsa, cache_win)
    for l in range(depth):
        lp = _layer_params(l, P)
        h_p, st_p = _group_layer(xp, lp, alpha, functools.partial(_nsa_prompt_group, lp=lp),
                                 jnp.zeros((bp, CONV_A_WIDTH - 1, GW), F32), jnp.zeros((bp, SHORT_CONV_WIDTH - 1, GW), F32),
                                 jnp.zeros((bp, HEADS, HEAD_DIM, HEAD_DIM), F32), jnp.zeros((bp, N_COLS_D), F32), tp)
        h_s, st_s = _group_layer(xs, lp, alpha,
                                 functools.partial(_nsa_sample_group, views=views, layer=l, page_table=page_table,
                                                   win_l=cache_win[l], lp=lp, t_real=ts),
                                 state_conv_a[l], state_conv_c[l], state_rwkv[l], state_rwkv_shift[l], ts)
        o_p, o_s = _moe_ln2([h_p, h_s], lp, P, l, alpha)
        xp = o_p.reshape(bp, tp, D_MODEL)
        xs = o_s.reshape(bs, ts_pad, D_MODEL)
        outs_p.append(st_p)
        outs_s.append(st_s)
    stk = lambda outs, i: jnp.stack([o[i] for o in outs], axis=0)
    return (xp, xs[:, :ts], stk(outs_p, 0), stk(outs_s, 0), stk(outs_p, 1), stk(outs_s, 1), stk(outs_p, 2), stk(outs_s, 2),
            stk(outs_p, 3), stk(outs_s, 3), stk(outs_p, 4), stk(outs_s, 4), stk(outs_p, 5), stk(outs_s, 5))
```

```python
import functools

import numpy as np
import jax
import jax.numpy as jnp
from jax import lax
from jax.experimental import pallas as pl
from jax.experimental.pallas import tpu as pltpu

F32 = jnp.float32
BF16 = jnp.bfloat16

D_MODEL = 2048
DEPTH = 2
HEAD_DIM = 64
GW = 512
HEADS = 8
CONV_A_WIDTH = 31
SHORT_CONV_WIDTH = 3
KV_HEADS = 2
Q_PER_KV = 4
CMP_STRIDE = 16
SLC_BLOCK = 64
N_SLC = 16
WINDOW = 512
Q_BLOCK = 128
PAGE = 128
RWKV_GN_EPS = 64e-5
LN_EPS = 1e-5
NEG = -1e30
FORCED = 1e4
N_EXPERTS = 32
EXP_PER_GROUP = 8
N_GROUPS = 4
D_EXPERT = 512
N_IN = 5560
N_COLS_D = 1696

N_Z = 6144
C_VAL, C_GATE = 0, 512
C_Q, C_ROWS, C_KW, C_GATES = 1024, 1536, 2048, 2304
C_BG, C_CG, C_H = 2560, 3072, 3584
C_R, C_K, C_V = 4096, 4608, 5120
C_WL, C_AL, C_GL = 5632, 5760, 5888

VMEM_LIMIT = 56 * 1024 * 1024
MOE_TILE = 256


def _cparams(sem, vmem=VMEM_LIMIT, disable_bounds_checks=False):
    return pltpu.CompilerParams(dimension_semantics=sem, vmem_limit_bytes=vmem,
                                disable_bounds_checks=disable_bounds_checks)


def _const_spec(shape):
    nd = len(shape)
    return pl.BlockSpec(shape, lambda *_: (0,) * nd)


def _split2(x):
    hi = x.astype(BF16)
    lo = (x - hi.astype(F32)).astype(BF16)
    return hi, lo


def _split3(x):
    hi = x.astype(BF16)
    r1 = x - hi.astype(F32)
    mid = r1.astype(BF16)
    lo = (r1 - mid.astype(F32)).astype(BF16)
    return hi, mid, lo


_NN = (((1,), (0,)), ((), ()))
_NT = (((1,), (1,)), ((), ()))
_TN = (((0,), (0,)), ((), ()))


def _dot(a, b, dims=_NN):
    return lax.dot_general(a, b, dims, preferred_element_type=F32)


def _dot3(a, b, dims=_NN):
    ah, al = _split2(a)
    bh, bl = _split2(b)
    return _dot(ah, bh, dims) + _dot(al, bh, dims) + _dot(ah, bl, dims)


def _dot2(a, b, dims=_NN):
    bh, bl = _split2(b)
    ah = a.astype(BF16)
    return _dot(ah, bh, dims) + _dot(ah, bl, dims)


def _dot_exact_lhs(a_bf16, b, dims=_NN):
    hi, mid, lo = _split3(b)
    return _dot(a_bf16, hi, dims) + _dot(a_bf16, mid, dims) + _dot(a_bf16, lo, dims)


def _sigmoid(x):
    return 1.0 / (1.0 + jnp.exp(-x))


def _silu(x):
    return x * _sigmoid(x)


def _layer_norm(x, g, b, eps):
    mu = jnp.mean(x, axis=-1, keepdims=True)
    xc = x - mu
    var = jnp.mean(xc * xc, axis=-1, keepdims=True)
    return xc * lax.rsqrt(var + eps) * g + b


def _masked_softmax(s, maskf):
    s = jnp.where(maskf > 0.5, s, NEG)
    e = jnp.exp(s - jnp.max(s, axis=-1, keepdims=True)) * maskf
    return e / jnp.maximum(jnp.sum(e, axis=-1, keepdims=True), 1e-30)


def _proj_in_kernel(x_ref, w_ref, b_ref, o_ref, xb_ref):
    @pl.when(pl.program_id(1) == 0)
    def _():
        xb_ref[...] = x_ref[...].astype(BF16)

    o_ref[...] = _dot(xb_ref[...], w_ref[...]) + b_ref[...]


def _proj_in(x2d, w_bf16, b_row):
    n = x2d.shape[0]
    tm = min(1024, n)
    tn = 512
    return pl.pallas_call(
        _proj_in_kernel,
        grid=(n // tm, N_Z // tn),
        in_specs=[pl.BlockSpec((tm, D_MODEL), lambda i, j: (i, 0)),
                  pl.BlockSpec((D_MODEL, tn), lambda i, j: (0, j)),
                  pl.BlockSpec((1, tn), lambda i, j: (0, j))],
        out_specs=pl.BlockSpec((tm, tn), lambda i, j: (i, j)),
        out_shape=jax.ShapeDtypeStruct((n, N_Z), F32),
        scratch_shapes=[pltpu.VMEM((tm, D_MODEL), BF16)],
        compiler_params=_cparams(("parallel", "arbitrary")),
        name="proj_in",
    )(x2d, w_bf16, b_row)


_CONV_ROWS = 32


def _conv_kernel(val_ref, gate_ref, bg_ref, cg_ref, h_ref, bufa_ref, bufc_ref, wa_ref, ba_ref, lag_ref,
                 lab_ref, wc_ref, ya_ref, yc_ref, na_ref, nc_ref, ea_ref, ec_ref, *, tt, t_last):
    it = pl.program_id(1)
    nt = pl.num_programs(1)
    ha = 32
    hc = 8

    @pl.when(it == 0)
    def _():
        ea_ref[0:ha, :] = jnp.zeros((ha, GW), F32)
        ea_ref[ha - (CONV_A_WIDTH - 1):ha, :] = bufa_ref[0]
        ec_ref[0:hc, :] = jnp.zeros((hc, GW), F32)
        ec_ref[hc - (SHORT_CONV_WIDTH - 1):hc, :] = bufc_ref[0]

    @pl.when(it > 0)
    def _():
        ea_ref[0:ha, :] = ea_ref[tt:tt + ha, :]
        ec_ref[0:hc, :] = ec_ref[tt:tt + hc, :]

    ea_ref[ha:ha + tt, :] = val_ref[0] * _sigmoid(gate_ref[0])
    ec_ref[hc:hc + tt, :] = cg_ref[0] * h_ref[0]

    rc = min(_CONV_ROWS, tt)
    offa = ha - (CONV_A_WIDTH - 1)
    offc = hc - (SHORT_CONV_WIDTH - 1)
    for c in range(tt // rc):
        r0 = c * rc
        acc = jnp.zeros((rc, GW), F32)
        for b in range(8):
            rows_b = rc if b == 0 else rc + 8
            part = None
            for a in range((offa + CONV_A_WIDTH - 1) // 8 + 1):
                j = 8 * a + b - offa
                if 0 <= j < CONV_A_WIDTH:
                    assert r0 + 8 * a + rows_b <= tt + ha
                    term = wa_ref[j:j + 1, :] * ea_ref[r0 + 8 * a:r0 + 8 * a + rows_b, :]
                    part = term if part is None else part + term
            acc = acc + part[b:b + rc]
        y = _layer_norm(acc + ba_ref[...], lag_ref[...], lab_ref[...], LN_EPS)
        ya_ref[0, r0:r0 + rc, :] = _silu(y)
        accc = jnp.zeros((rc, GW), F32)
        for j in range(SHORT_CONV_WIDTH):
            accc = accc + wc_ref[j:j + 1, :] * ec_ref[r0 + offc + j:r0 + offc + j + rc, :]
        yc_ref[0, r0:r0 + rc, :] = bg_ref[0, r0:r0 + rc, :] * accc

    @pl.when(it == nt - 1)
    def _():
        na_ref[0] = ea_ref[t_last + offa:t_last + ha, :]
        nc_ref[0] = ec_ref[t_last + offc:t_last + hc, :]


def _conv_mix(z, buf_a, buf_c, wa, ba, lag, lab, wc, t_real):
    B, T, _ = z.shape
    tt = min(256, T)
    nt = T // tt
    t_last = t_real - (nt - 1) * tt
    zspec = lambda col: pl.BlockSpec((1, tt, GW), lambda b, i: (b, i, col // GW))
    return pl.pallas_call(
        functools.partial(_conv_kernel, tt=tt, t_last=t_last),
        grid=(B, nt),
        in_specs=[zspec(C_VAL), zspec(C_GATE), zspec(C_BG), zspec(C_CG), zspec(C_H),
                  pl.BlockSpec((1, CONV_A_WIDTH - 1, GW), lambda b, i: (b, 0, 0)),
                  pl.BlockSpec((1, SHORT_CONV_WIDTH - 1, GW), lambda b, i: (b, 0, 0)),
                  _const_spec((CONV_A_WIDTH, GW)), _const_spec((1, GW)), _const_spec((1, GW)),
                  _const_spec((1, GW)), _const_spec((SHORT_CONV_WIDTH, GW))],
        out_specs=[pl.BlockSpec((1, tt, GW), lambda b, i: (b, i, 0)),
                   pl.BlockSpec((1, tt, GW), lambda b, i: (b, i, 0)),
                   pl.BlockSpec((1, CONV_A_WIDTH - 1, GW), lambda b, i: (b, 0, 0)),
                   pl.BlockSpec((1, SHORT_CONV_WIDTH - 1, GW), lambda b, i: (b, 0, 0))],
        out_shape=[jax.ShapeDtypeStruct((B, T, GW), F32), jax.ShapeDtypeStruct((B, T, GW), F32),
                   jax.ShapeDtypeStruct((B, CONV_A_WIDTH - 1, GW), F32),
                   jax.ShapeDtypeStruct((B, SHORT_CONV_WIDTH - 1, GW), F32)],
        scratch_shapes=[pltpu.VMEM((tt + 32, GW), F32), pltpu.VMEM((tt + 8, GW), F32)],
        compiler_params=_cparams(("parallel", "arbitrary")),
        name="conv_mix",
    )(z, z, z, z, z, buf_a, buf_c, wa, ba, lag, lab, wc)


def _rwkv_token_math(r_ref, k_ref, v_ref, wl_ref, al_ref, gl_ref, mu_ref, w0_ref, w2_ref, a0_ref, a2_ref, g2_ref,
                     kkp_ref, kap_ref, rk_ref, carry_ref, tt, t_last):
    row = lax.broadcasted_iota(jnp.int32, (tt, 1), 0)

    def mixed(x, c0):
        w = x.shape[1]
        prev = pltpu.roll(x, 1, 0) if tt > 1 else x
        prev = jnp.where(row == 0, carry_ref[:, c0:c0 + w], prev)
        return x + (prev - x) * mu_ref[:, c0:c0 + w]

    xr, xk, xv = r_ref[0], k_ref[0], v_ref[0]
    xwl, xal, xgl = wl_ref[0], al_ref[0], gl_ref[0]
    r = mixed(xr, 0)
    k = mixed(xk, GW)
    v = mixed(xv, 2 * GW)
    wl = mixed(xwl, 3 * GW)
    al = mixed(xal, 3 * GW + 128)
    gl = mixed(xgl, 3 * GW + 256)
    lr = t_last - 1
    carry_ref[:, 0:GW] = xr[lr:lr + 1]
    carry_ref[:, GW:2 * GW] = xk[lr:lr + 1]
    carry_ref[:, 2 * GW:3 * GW] = xv[lr:lr + 1]
    carry_ref[:, 3 * GW:3 * GW + 128] = xwl[lr:lr + 1]
    carry_ref[:, 3 * GW + 128:3 * GW + 256] = xal[lr:lr + 1]
    carry_ref[:, 3 * GW + 256:3 * GW + 384] = xgl[lr:lr + 1]

    y = w0_ref[...] + _dot3(jnp.tanh(wl), w2_ref[...])
    u = -y
    w_log = -(jnp.maximum(u, 0.0) + jnp.log(1.0 + jnp.exp(-jnp.abs(u)))) - 0.5
    log_decay = -jnp.exp(w_log)
    a = _sigmoid(a0_ref[...] + _dot3(al, a2_ref[...]))
    g = _dot3(_sigmoid(gl), g2_ref[...])
    kk = k * kkp_ref[...]
    k2 = k * (1.0 + (a - 1.0) * kap_ref[...])
    rkk = r * k2 * rk_ref[...]
    heads = [[] for _ in range(7)]
    for h in range(HEADS):
        sl = slice(h * HEAD_DIM, (h + 1) * HEAD_DIM)
        kk_h = kk[:, sl]
        nrm = jnp.sqrt(jnp.sum(kk_h * kk_h, axis=-1, keepdims=True))
        kk_h = kk_h / jnp.maximum(nrm, 1e-12)
        v_h = v[:, sl]
        per_head = (r[:, sl], log_decay[:, sl], k2[:, sl], v_h, kk_h, kk_h * a[:, sl],
                    jnp.sum(rkk[:, sl], axis=-1, keepdims=True) * v_h)
        for lst, val in zip(heads, per_head):
            lst.append(val)
    return g, heads


def _rwkv_prep_kernel(r_ref, k_ref, v_ref, wl_ref, al_ref, gl_ref, sh_ref, mu_ref, w0_ref, w2_ref, a0_ref,
                      a2_ref, g2_ref, kkp_ref, kap_ref, rk_ref,
                      ro_ref, wo_ref, ko_ref, vo_ref, kko_ref, bo_ref, bonus_ref, g_ref, carry_ref, *, tt, t_last):
    @pl.when(pl.program_id(1) == 0)
    def _():
        carry_ref[...] = sh_ref[0]

    g, heads = _rwkv_token_math(r_ref, k_ref, v_ref, wl_ref, al_ref, gl_ref, mu_ref, w0_ref, w2_ref, a0_ref, a2_ref,
                                g2_ref, kkp_ref, kap_ref, rk_ref, carry_ref, tt, t_last)
    g_ref[0] = g
    for out_ref, vals in zip((ro_ref, wo_ref, ko_ref, vo_ref, kko_ref, bo_ref, bonus_ref), heads):
        for h in range(HEADS):
            out_ref[0, h] = vals[h]


def _rwkv_prep(z, shift, mu, w0, w2p, a0, a2p, g2p, kkp, kap, rk, t_real):
    B, T, _ = z.shape
    tt = min(256, T)
    nt = T // tt
    t_last = t_real - (nt - 1) * tt
    wide = lambda col: pl.BlockSpec((1, tt, GW), lambda b, i: (b, i, col // GW))
    thin = lambda col: pl.BlockSpec((1, tt, 128), lambda b, i: (b, i, col // 128))
    head_out = pl.BlockSpec((1, HEADS, tt, HEAD_DIM), lambda b, i: (b, 0, i, 0))
    head_shape = jax.ShapeDtypeStruct((B, HEADS, T, HEAD_DIM), F32)
    dw = 3 * GW + 384
    return pl.pallas_call(
        functools.partial(_rwkv_prep_kernel, tt=tt, t_last=t_last),
        grid=(B, nt),
        in_specs=[wide(C_R), wide(C_K), wide(C_V), thin(C_WL), thin(C_AL), thin(C_GL),
                  pl.BlockSpec((1, 1, dw), lambda b, i: (b, 0, 0)),
                  _const_spec((1, dw)), _const_spec((1, GW)), _const_spec((128, GW)), _const_spec((1, GW)),
                  _const_spec((128, GW)), _const_spec((128, GW)), _const_spec((1, GW)), _const_spec((1, GW)),
                  _const_spec((1, GW))],
        out_specs=[head_out] * 7 + [pl.BlockSpec((1, tt, GW), lambda b, i: (b, i, 0))],
        out_shape=[head_shape] * 7 + [jax.ShapeDtypeStruct((B, T, GW), F32)],
        scratch_shapes=[pltpu.VMEM((1, dw), F32)],
        compiler_params=_cparams(("parallel", "arbitrary")),
        name="rwkv_prep",
    )(z, z, z, z, z, z, shift, mu, w0, w2p, a0, a2p, g2p, kkp, kap, rk)


def _rwkv_scan_kernel(r_ref, w_ref, k_ref, v_ref, kk_ref, b_ref, bonus_ref, s0_ref, lg_ref, lb_ref,
                      y_ref, s_ref, *, nb, tt, n_steps):
    it = pl.program_id(1)

    @pl.when(it == 0)
    def _():
        s_ref[...] = s0_ref[...]

    y_ref[...] = jnp.zeros(y_ref.shape, F32)
    ii = lax.broadcasted_iota(jnp.int32, (HEAD_DIM, HEAD_DIM), 0)
    jj = lax.broadcasted_iota(jnp.int32, (HEAD_DIM, HEAD_DIM), 1)
    eye = jnp.where(ii == jj, 1.0, 0.0).astype(F32)

    def step(t, carry):
        for n in range(nb):
            for h in range(HEADS):
                row = lambda ref: ref[n, h, pl.ds(t, 1), :]
                S = s_ref[n, h]
                sa = jnp.sum(S * row(kk_ref), axis=-1, keepdims=True)
                vcol = jnp.sum(eye * row(v_ref), axis=-1, keepdims=True)
                S = S * jnp.exp(row(w_ref)) - sa * row(b_ref) + vcol * row(k_ref)
                s_ref[n, h] = S
                ycol = jnp.sum(S * row(r_ref), axis=-1, keepdims=True)
                y_ref[n, h, pl.ds(t, 1), :] = jnp.sum(eye * ycol, axis=0, keepdims=True)
        return carry

    lax.fori_loop(0, n_steps, step, 0)

    for n in range(nb):
        for h in range(HEADS):
            y = y_ref[n, h]
            mu = jnp.mean(y, axis=-1, keepdims=True)
            yc = y - mu
            var = jnp.mean(yc * yc, axis=-1, keepdims=True)
            y_ref[n, h] = yc * lax.rsqrt(var + RWKV_GN_EPS) * lg_ref[h:h + 1, :] + lb_ref[h:h + 1, :] + bonus_ref[n, h]


def _rwkv_scan(r, w, k, v, kk, b, bonus, s0, lnx_g, lnx_b, t_real):
    B, H, T, _ = r.shape
    nb = 2
    tt = min(128, T)
    nt = T // tt
    n_steps = min(tt, t_real)
    blk = pl.BlockSpec((nb, H, tt, HEAD_DIM), lambda b_, i: (b_, 0, i, 0))
    sblk = pl.BlockSpec((nb, H, HEAD_DIM, HEAD_DIM), lambda b_, i: (b_, 0, 0, 0))
    return pl.pallas_call(
        functools.partial(_rwkv_scan_kernel, nb=nb, tt=tt, n_steps=n_steps),
        grid=(B // nb, nt),
        in_specs=[blk] * 7 + [sblk, _const_spec((H, HEAD_DIM)), _const_spec((H, HEAD_DIM))],
        out_specs=[blk, sblk],
        out_shape=[jax.ShapeDtypeStruct((B, H, T, HEAD_DIM), F32),
                   jax.ShapeDtypeStruct((B, H, HEAD_DIM, HEAD_DIM), F32)],
        compiler_params=_cparams(("parallel", "arbitrary")),
        name="rwkv_scan",
    )(r, w, k, v, kk, b, bonus, s0, lnx_g, lnx_b)


_RW_CHUNK = 64


def _rwkv_chunk_kernel(xr_ref, xk_ref, xv_ref, xwl_ref, xal_ref, xgl_ref, sh_ref, mu_ref, w0_ref, w2_ref, a0_ref,
                       a2_ref, g2_ref, kkp_ref, kap_ref, rk_ref, s0_ref, lg_ref, lb_ref,
                       y_ref, g_ref, s_ref, carry_ref):
    C = xr_ref.shape[1]

    @pl.when(pl.program_id(1) == 0)
    def _():
        s_ref[...] = s0_ref[...]
        carry_ref[...] = sh_ref[0]

    g_out, (r, lw, k, v, kk, b, bonus) = _rwkv_token_math(
        xr_ref, xk_ref, xv_ref, xwl_ref, xal_ref, xgl_ref, mu_ref, w0_ref, w2_ref, a0_ref, a2_ref, g2_ref,
        kkp_ref, kap_ref, rk_ref, carry_ref, C, C)
    g_ref[0] = g_out
    ti = lax.broadcasted_iota(jnp.int32, (C, C), 0)
    si = lax.broadcasted_iota(jnp.int32, (C, C), 1)
    incl = ti >= si
    strict = ti > si
    tri_b = jnp.where(incl, 1.0, 0.0).astype(BF16)
    n_fac = C.bit_length() - 1
    hs = range(HEADS)
    cs = [_dot_exact_lhs(tri_b, lw[h]) for h in hs]
    cs_last = [c[C - 1:C, :] for c in cs]
    AR = [jnp.concatenate([kk[h] * jnp.exp(cs[h] - lw[h]), r[h] * jnp.exp(cs[h])], axis=0) for h in hs]
    BK = []
    for h in hs:
        g_inv = jnp.exp(-cs[h])
        BK.append(jnp.concatenate([b[h] * g_inv, k[h] * g_inv], axis=0))
    G = [_dot3(AR[h], BK[h], _NT) for h in hs]
    S0 = [s_ref[0, h] for h in hs]
    X = [_dot3(AR[h][0:C], S0[h], _NT) + _dot3(jnp.where(strict, G[h][0:C, C:2 * C], 0.0), v[h]) for h in hs]
    N = [-jnp.where(strict, G[h][0:C, 0:C], 0.0) for h in hs]
    for f in range(n_fac):
        X = [X[h] + _dot2(N[h], X[h]) for h in hs]
        if f + 1 < n_fac:
            N = [_dot2(N[h], N[h]) for h in hs]
    b16 = lambda a: a.astype(BF16)
    Y = [_dot(b16(AR[h][C:2 * C]), b16(S0[h]), _NT) + _dot(b16(jnp.where(incl, G[h][C:2 * C, C:2 * C], 0.0)), b16(v[h]))
         - _dot(b16(jnp.where(incl, G[h][C:2 * C, 0:C], 0.0)), b16(X[h])) for h in hs]
    for h in hs:
        g_end = jnp.exp(cs_last[h] - cs[h])
        UV = jnp.concatenate([-X[h], v[h]], axis=0)
        BKg = jnp.concatenate([b[h] * g_end, k[h] * g_end], axis=0)
        s_ref[0, h] = S0[h] * jnp.exp(cs_last[h]) + _dot3(UV, BKg, _TN)
    out = []
    for h in hs:
        mu = jnp.mean(Y[h], axis=-1, keepdims=True)
        yc = Y[h] - mu
        var = jnp.mean(yc * yc, axis=-1, keepdims=True)
        out.append(yc * lax.rsqrt(var + RWKV_GN_EPS) * lg_ref[h:h + 1, :] + lb_ref[h:h + 1, :] + bonus[h])
    y_ref[0] = jnp.concatenate(out, axis=1)


def _rwkv_chunked(z, shift, s0, lp):
    B, T, _ = z.shape
    C = _RW_CHUNK
    wide = lambda col: pl.BlockSpec((1, C, GW), lambda b_, i: (b_, i, col // GW))
    thin = lambda col: pl.BlockSpec((1, C, 128), lambda b_, i: (b_, i, col // 128))
    sblk = pl.BlockSpec((1, HEADS, HEAD_DIM, HEAD_DIM), lambda b_, i: (b_, 0, 0, 0))
    tok = pl.BlockSpec((1, C, GW), lambda b_, i: (b_, i, 0))
    dw = 3 * GW + 384
    return pl.pallas_call(
        _rwkv_chunk_kernel,
        grid=(B, T // C),
        in_specs=[wide(C_R), wide(C_K), wide(C_V), thin(C_WL), thin(C_AL), thin(C_GL),
                  pl.BlockSpec((1, 1, dw), lambda b_, i: (b_, 0, 0)),
                  _const_spec((1, dw)), _const_spec((1, GW)), _const_spec((128, GW)), _const_spec((1, GW)),
                  _const_spec((128, GW)), _const_spec((128, GW)), _const_spec((1, GW)), _const_spec((1, GW)),
                  _const_spec((1, GW)), sblk, _const_spec((HEADS, HEAD_DIM)), _const_spec((HEADS, HEAD_DIM))],
        out_specs=[tok, tok, sblk],
        out_shape=[jax.ShapeDtypeStruct((B, T, GW), F32), jax.ShapeDtypeStruct((B, T, GW), F32),
                   jax.ShapeDtypeStruct((B, HEADS, HEAD_DIM, HEAD_DIM), F32)],
        scratch_shapes=[pltpu.VMEM((1, dw), F32)],
        compiler_params=_cparams(("parallel", "arbitrary")),
        name="rwkv_chunked",
    )(z, z, z, z, z, z, shift, lp["rw_mu"], lp["rw_w0"], lp["rw_w2"], lp["rw_a0"], lp["rw_a2"], lp["rw_g2"],
      lp["rw_kk"], lp["rw_ka"], lp["rw_rk"], s0, lp["rw_lnx_g"], lp["rw_lnx_b"])


def _compress_tail(F, cst):
    n = F.shape[0]
    nxt = pltpu.roll(F[:, HEAD_DIM:2 * HEAD_DIM], n - 1, 0)
    const = cst[0:1, 0:HEAD_DIM] + cst[1:2, HEAD_DIM:2 * HEAD_DIM]
    return F[:, 0:HEAD_DIM] + nxt + const


def _compress_kernel(x_ref, w_ref, pe_ref, o_ref):
    F = _dot3(x_ref[0, 0, 0], w_ref[0])
    cst = _dot3(pe_ref[0], w_ref[0])
    o_ref[0, 0, 0] = _compress_tail(F, cst)


def _compress_prompt(cmp_in, wcat, pe2):
    B, _, _, nc, _ = cmp_in.shape
    return pl.pallas_call(
        _compress_kernel,
        grid=(B, 2, 2),
        in_specs=[pl.BlockSpec((1, 1, 1, nc, 1024), lambda b, a, g: (b, a, g, 0, 0)),
                  pl.BlockSpec((1, 1024, 128), lambda b, a, g: (a, 0, 0)),
                  pl.BlockSpec((1, 8, 1024), lambda b, a, g: (a, 0, 0))],
        out_specs=pl.BlockSpec((1, 1, 1, nc, HEAD_DIM), lambda b, a, g: (b, a, g, 0, 0)),
        out_shape=jax.ShapeDtypeStruct((B, 2, 2, nc, HEAD_DIM), F32),
        compiler_params=_cparams(("parallel", "parallel", "parallel")),
        name="nsa_compress",
    )(cmp_in, wcat, pe2)


def _top_select(selT, j_f):
    v = selT
    sel = jnp.zeros(selT.shape, F32)
    for _ in range(N_SLC):
        m = jnp.max(v, axis=0, keepdims=True)
        first = jnp.min(jnp.where(v == m, j_f, 1e9), axis=0, keepdims=True)
        hit = j_f == first
        sel = jnp.where(hit, 1.0, sel)
        v = jnp.where(hit, -3e38, v)
    return sel


def _gated_heads(gts, g, o_cmp, o_slc, o_win, rows):
    out = []
    for m in range(Q_PER_KV):
        c0 = g * 3 * Q_PER_KV + m * 3
        sl = slice(m * rows, (m + 1) * rows)
        out.append(gts[:, c0:c0 + 1] * o_cmp[sl] + gts[:, c0 + 1:c0 + 2] * o_slc[sl] + gts[:, c0 + 2:c0 + 3] * o_win[sl])
    return out


_KT = 512


_LOG2E = 1.4426950408889634


def _nsa_prompt_kernel(qT_ref, gT_ref, ckv_ref, ks_ref, vsT_ref, kw_ref, vwT_ref, mt_ref, oT_ref, *, T):
    i = pl.program_id(1)
    start = i * Q_BLOCK
    nc = T // CMP_STRIDE
    ns = T // SLC_BLOCK
    cols = Q_PER_KV * Q_BLOCK
    gs = range(KV_HEADS)
    t_lane = start + lax.broadcasted_iota(jnp.int32, (1, Q_BLOCK), 1)
    t_cols = jnp.concatenate([t_lane] * Q_PER_KV, axis=1)
    gts = _sigmoid(gT_ref[0])
    j_col = lax.broadcasted_iota(jnp.int32, (ns, 1), 0)
    j_f = j_col.astype(F32)
    cur = t_lane >> 6
    forced = jnp.where(j_col == 0, 1.0, jnp.where(j_col == cur, 1.0, jnp.where(j_col == cur - 1, 1.0, 0.0)))
    QT = [jnp.concatenate([qT_ref[0, (g * Q_PER_KV + m) * HEAD_DIM:(g * Q_PER_KV + m + 1) * HEAD_DIM, :]
                           for m in range(Q_PER_KV)], axis=1) * (HEAD_DIM ** -0.5 * _LOG2E) for g in gs]
    normed = lambda a: a[0:HEAD_DIM] * (1.0 / a[HEAD_DIM:HEAD_DIM + 1])

    n_col = lax.broadcasted_iota(jnp.int32, (nc, 1), 0)
    bias_c = jnp.where(n_col * CMP_STRIDE + (2 * CMP_STRIDE - 1) <= t_cols, 0.0, NEG)
    any_c = jnp.where(t_cols >= 2 * CMP_STRIDE - 1, 1.0, 0.0)
    pcT = []
    for g in gs:
        s = _dot3(ckv_ref[0, 0, g], QT[g]) + bias_c
        e = jnp.exp2(s - jnp.max(s, axis=0, keepdims=True))
        pcT.append(e * (any_c / jnp.sum(e, axis=0, keepdims=True)))
    o_cmp = [_dot(ckv_ref[0, 1, g].astype(BF16), pcT[g].astype(BF16), _TN) for g in gs]
    rhs = []
    for g in gs:
        psT = pcT[g][:, 0:Q_BLOCK]
        for m in range(1, Q_PER_KV):
            psT = psT + pcT[g][:, m * Q_BLOCK:(m + 1) * Q_BLOCK]
        selT = _dot_exact_lhs(mt_ref[...], psT)
        selT = jnp.where(forced > 0.5, FORCED, jnp.where(j_col * SLC_BLOCK <= t_lane, selT, -FORCED))
        selbT = jnp.where(_top_select(selT, j_f) > 0.5, 0.0, NEG).astype(BF16)
        rhs.append(jnp.concatenate([QT[g].astype(BF16), jnp.concatenate([selbT] * Q_PER_KV, axis=1)], axis=0))

    def key_tile(kt, carry, causal):
        if causal:
            pos = kt * _KT + lax.broadcasted_iota(jnp.int32, (_KT, 1), 0)
            bias_t = jnp.where(pos <= t_cols, 0.0, NEG)
        out = []
        for g in gs:
            m_run, acc = carry[g]
            s = _dot(ks_ref[0, g, pl.ds(pl.multiple_of(kt * _KT, _KT), _KT), :], rhs[g])
            if causal:
                s = s + bias_t
            m_new = jnp.maximum(m_run, jnp.max(s, axis=0, keepdims=True))
            p = jnp.exp2(s - m_new)
            out.append((m_new, jnp.exp2(m_run - m_new) * acc + _dot(vsT_ref[0, g, kt], p.astype(BF16))))
        return tuple(out)

    init = tuple((jnp.full((1, cols), NEG, F32), jnp.zeros((2 * HEAD_DIM, cols), F32)) for _ in gs)
    last = start // _KT
    carry = lax.fori_loop(0, last, lambda kt, c: key_tile(kt, c, False), init)
    carry = key_tile(last, carry, True)
    o_slc = [normed(carry[g][1]) for g in gs]

    nwt = WINDOW // Q_BLOCK + 1
    wb = jnp.maximum(i - WINDOW // Q_BLOCK, 0)
    dt = t_cols - (wb * Q_BLOCK + lax.broadcasted_iota(jnp.int32, (nwt * Q_BLOCK, 1), 0))
    bias_w = jnp.where(dt >= 0, jnp.where(dt < WINDOW, 0.0, NEG), NEG)
    for g in gs:
        kwt = kw_ref[0, g, pl.ds(pl.multiple_of(wb * Q_BLOCK, Q_BLOCK), nwt * Q_BLOCK), :]
        sw = _dot(kwt, rhs[g][0:HEAD_DIM]) + bias_w
        pw = jnp.exp2(sw - jnp.max(sw, axis=0, keepdims=True))
        vwT = jnp.concatenate([vwT_ref[0, g, wb + u] for u in range(nwt)], axis=1)
        o_win = normed(_dot(vwT, pw.astype(BF16)))
        for m in range(Q_PER_KV):
            c0 = g * 3 * Q_PER_KV + m * 3
            sl = slice(m * Q_BLOCK, (m + 1) * Q_BLOCK)
            head = g * Q_PER_KV + m
            oT_ref[0, head * HEAD_DIM:(head + 1) * HEAD_DIM, :] = (
                gts[c0:c0 + 1] * o_cmp[g][:, sl] + gts[c0 + 1:c0 + 2] * o_slc[g][:, sl] + gts[c0 + 2:c0 + 3] * o_win[:, sl])


def _sel_matrix(n_rows, n_chunks):
    j = np.arange(n_rows)[:, None]
    c = np.arange(n_chunks)[None, :]
    per = SLC_BLOCK // CMP_STRIDE
    m = 0.5 * ((c // per == j).astype(np.float32) + ((c + 1) // per == j).astype(np.float32))
    return jnp.asarray(m, BF16)


def _nsa_prompt(qT, gT, ckv, ks, vsT, kw, vwT):
    B, _, T = qT.shape
    nc, ns = T // CMP_STRIDE, T // SLC_BLOCK
    nq = T // Q_BLOCK
    mt = _sel_matrix(ns, nc)
    per_b = lambda shape: pl.BlockSpec((1,) + shape, lambda b, i: (b,) + (0,) * len(shape))
    return pl.pallas_call(
        functools.partial(_nsa_prompt_kernel, T=T),
        grid=(B, nq),
        in_specs=[pl.BlockSpec((1, GW, Q_BLOCK), lambda b, i: (b, 0, i)),
                  pl.BlockSpec((1, gT.shape[1], Q_BLOCK), lambda b, i: (b, 0, i)),
                  per_b((2, KV_HEADS, nc, HEAD_DIM)),
                  per_b((KV_HEADS, T, HEAD_DIM + ns)),
                  per_b((KV_HEADS, T // _KT, 2 * HEAD_DIM, _KT)),
                  per_b((KV_HEADS, T, HEAD_DIM)),
                  per_b((KV_HEADS, nq, 2 * HEAD_DIM, Q_BLOCK)),
                  _const_spec((ns, nc))],
        out_specs=pl.BlockSpec((1, GW, Q_BLOCK), lambda b, i: (b, 0, i)),
        out_shape=jax.ShapeDtypeStruct((B, GW, T), F32),
        compiler_params=_cparams(("parallel", "arbitrary")),
        name="nsa_prompt",
    )(qT, gT, ckv, ks, vsT, kw, vwT, mt)


_PPS = 16
_SEL_ROWS = 384


def _nsa_sample_a_kernel(pt_ref, *refs, n_pages, past_len, t_pad):
    pages = refs[:_PPS]
    w_ref, pe_ref, q_ref, mt_ref = refs[_PPS:_PPS + 4]
    ocmp_ref, selm_ref = refs[_PPS + 4:_PPS + 6]
    x_ref = refs[_PPS + 6]
    s = pl.program_id(1)
    cpp = PAGE // CMP_STRIDE
    for k in range(_PPS):
        x_ref[:, pl.ds(pl.multiple_of((s * _PPS + k) * cpp, cpp), cpp), :] = pages[k][0]

    @pl.when(s == pl.num_programs(1) - 1)
    def _():
        nc = n_pages * cpp
        n_blocks = past_len // SLC_BLOCK + 1
        lanes = 128
        ps_rows = []
        for g in range(KV_HEADS):
            ck = _compress_tail(_dot3(x_ref[g], w_ref[0]), _dot3(pe_ref[0], w_ref[0]))
            cv = _compress_tail(_dot(x_ref[KV_HEADS + g].astype(BF16), w_ref[1].astype(BF16)), _dot3(pe_ref[1], w_ref[1]))
            Q = jnp.concatenate([q_ref[0, :, (g * Q_PER_KV + m) * HEAD_DIM:(g * Q_PER_KV + m + 1) * HEAD_DIM]
                                 for m in range(Q_PER_KV)], axis=0) * (HEAD_DIM ** -0.5)
            t_rows = past_len + (lax.broadcasted_iota(jnp.int32, (Q_PER_KV * t_pad, 1), 0) & (t_pad - 1))
            sc = _dot3(Q, ck, _NT)
            n_lane = lax.broadcasted_iota(jnp.int32, (1, nc), 1)
            mask_c = jnp.where(n_lane * CMP_STRIDE + (2 * CMP_STRIDE - 1) <= t_rows, 1.0, 0.0)
            pc = _masked_softmax(sc, mask_c)
            ocmp_ref[0, g] = _dot(pc.astype(BF16), cv.astype(BF16))
            ps_rows.append(pc[0:t_pad] + pc[t_pad:2 * t_pad] + pc[2 * t_pad:3 * t_pad] + pc[3 * t_pad:4 * t_pad])
        ps = jnp.concatenate(ps_rows + [jnp.zeros((lanes - KV_HEADS * t_pad, nc), F32)], axis=0)
        selT = _dot_exact_lhs(mt_ref[...], ps, _NT)
        j_col = lax.broadcasted_iota(jnp.int32, (_SEL_ROWS, 1), 0)
        t_lane = past_len + (lax.broadcasted_iota(jnp.int32, (1, lanes), 1) & (t_pad - 1))
        cur = t_lane >> 6
        forced = jnp.where(j_col == 0, 1.0, jnp.where(j_col == cur, 1.0, jnp.where(j_col == cur - 1, 1.0, 0.0)))
        selT = jnp.where(forced > 0.5, FORCED, jnp.where(j_col * SLC_BLOCK <= t_lane, selT, -FORCED))
        selT = jnp.where(j_col < n_blocks, selT, -3.0 * FORCED)
        selmT = _top_select(selT, j_col.astype(F32))
        r = lax.broadcasted_iota(jnp.int32, (lanes, lanes), 0)
        c = lax.broadcasted_iota(jnp.int32, (lanes, lanes), 1)
        eye = jnp.where(r == c, 1.0, 0.0).astype(BF16)
        selm = _dot(eye, selmT.astype(BF16), _NT)
        selm_ref[0] = selm[0:KV_HEADS * t_pad]


def _nsa_sample_a(page_table, cmp_pages, layer, wcat, pe2, z, past_len):
    B, t_pad, _ = z.shape
    n_pages = page_table.shape[1]
    nc = n_pages * (PAGE // CMP_STRIDE)
    mt = _sel_matrix(_SEL_ROWS, nc)
    page_spec = lambda k: pl.BlockSpec((None, 1, 2 * KV_HEADS, PAGE // CMP_STRIDE, 1024),
                                       lambda b, s, pt: (layer, pt[b, s * _PPS + k], 0, 0, 0))
    cst = lambda shape: pl.BlockSpec(shape, lambda b, s, pt: (0,) * len(shape))
    grid_spec = pltpu.PrefetchScalarGridSpec(
        num_scalar_prefetch=1,
        grid=(B, n_pages // _PPS),
        in_specs=[page_spec(k) for k in range(_PPS)] + [
            cst((2, 1024, 128)), cst((2, 8, 1024)),
            pl.BlockSpec((1, t_pad, GW), lambda b, s, pt: (b, 0, C_Q // GW)),
            cst((_SEL_ROWS, nc))],
        out_specs=[pl.BlockSpec((1, KV_HEADS, Q_PER_KV * t_pad, HEAD_DIM), lambda b, s, pt: (b, 0, 0, 0)),
                   pl.BlockSpec((1, KV_HEADS * t_pad, _SEL_ROWS), lambda b, s, pt: (b, 0, 0))],
        scratch_shapes=[pltpu.VMEM((2 * KV_HEADS, nc, 1024), F32)])
    return pl.pallas_call(
        functools.partial(_nsa_sample_a_kernel, n_pages=n_pages, past_len=past_len, t_pad=t_pad),
        grid_spec=grid_spec,
        out_shape=[jax.ShapeDtypeStruct((B, KV_HEADS, Q_PER_KV * t_pad, HEAD_DIM), F32),
                   jax.ShapeDtypeStruct((B, KV_HEADS * t_pad, _SEL_ROWS), F32)],
        compiler_params=_cparams(("parallel", "arbitrary")),
        name="nsa_sample_select",
    )(page_table, *([cmp_pages] * _PPS), wcat, pe2, z, mt)


def _nsa_sample_b_kernel(pt_ref, *refs, n_pages, past_len, t_pad, t_real):
    kpages = refs[:_PPS]
    vpages = refs[_PPS:2 * _PPS]
    q_ref, rows_ref, kwvw_ref, gt_ref, selm_ref, ocmp_ref, cw_ref = refs[2 * _PPS:2 * _PPS + 7]
    o_ref = refs[2 * _PPS + 7]
    m_ref, l_ref, acc_ref = refs[2 * _PPS + 8:]
    s = pl.program_id(1)
    n_steps = n_pages // _PPS
    rows = Q_PER_KV * t_pad
    t_rows = past_len + (lax.broadcasted_iota(jnp.int32, (rows, 1), 0) & (t_pad - 1))

    @pl.when(s == 0)
    def _():
        m_ref[...] = jnp.full(m_ref.shape, NEG, F32)
        l_ref[...] = jnp.zeros(l_ref.shape, F32)
        acc_ref[...] = jnp.zeros(acc_ref.shape, F32)

    def q_rows(g):
        Q = jnp.concatenate([q_ref[0, :, (g * Q_PER_KV + m) * HEAD_DIM:(g * Q_PER_KV + m + 1) * HEAD_DIM]
                             for m in range(Q_PER_KV)], axis=0) * (HEAD_DIM ** -0.5)
        return Q.astype(BF16)

    def online(g, sc, ok, values):
        sc = jnp.where(ok > 0.5, sc, NEG)
        m_run = m_ref[g]
        m_new = jnp.maximum(m_run, jnp.max(sc, axis=-1, keepdims=True))
        p = jnp.exp(sc - m_new) * ok
        corr = jnp.exp(m_run - m_new)
        l_ref[g] = corr * l_ref[g] + jnp.sum(p, axis=-1, keepdims=True)
        acc_ref[g] = corr * acc_ref[g] + values(p.astype(BF16))
        m_ref[g] = m_new

    @pl.when(s < n_steps)
    def _():
        keys = _PPS * PAGE
        jr = lax.broadcasted_iota(jnp.int32, (_SEL_ROWS, keys), 0)
        cl = lax.broadcasted_iota(jnp.int32, (_SEL_ROWS, keys), 1)
        expand = jnp.where(jr == s * (keys // SLC_BLOCK) + (cl >> 6), 1.0, 0.0).astype(BF16)
        mv = _dot(selm_ref[0].astype(BF16), expand)
        pos = s * keys + lax.broadcasted_iota(jnp.int32, (1, keys), 1)
        for g in range(KV_HEADS):
            Qb = q_rows(g)
            sc = jnp.concatenate([_dot(Qb, kpages[k][g].astype(BF16)) for k in range(_PPS)], axis=1)
            mvg = jnp.concatenate([mv[g * t_pad:(g + 1) * t_pad]] * Q_PER_KV, axis=0)
            ok = jnp.where(pos <= t_rows, mvg, 0.0)

            def values(pb, g=g):
                out = _dot(pb[:, 0:PAGE], vpages[0][g].astype(BF16), _NT)
                for k in range(1, _PPS):
                    out = out + _dot(pb[:, k * PAGE:(k + 1) * PAGE], vpages[k][g].astype(BF16), _NT)
                return out

            online(g, sc, ok, values)

    @pl.when(s == n_steps)
    def _():
        gts = _sigmoid(gt_ref[0])
        last_blk = past_len // SLC_BLOCK
        r_lane = lax.broadcasted_iota(jnp.int32, (1, t_pad), 1)
        pieces = []
        for g in range(KV_HEADS):
            Qb = q_rows(g)
            k_new = rows_ref[0, :, (2 * KV_HEADS + g) * HEAD_DIM:(2 * KV_HEADS + g + 1) * HEAD_DIM]
            v_new = rows_ref[0, :, (3 * KV_HEADS + g) * HEAD_DIM:(3 * KV_HEADS + g + 1) * HEAD_DIM]
            sc = _dot(Qb, k_new.astype(BF16), _NT)
            sel_last = jnp.concatenate([selm_ref[0, g * t_pad:(g + 1) * t_pad, last_blk:last_blk + 1]] * Q_PER_KV, axis=0)
            vis = jnp.where(past_len + r_lane <= t_rows, jnp.where(r_lane < t_real, 1.0, 0.0), 0.0)
            online(g, sc, vis * sel_last, lambda pb, v_new=v_new: _dot(pb, v_new.astype(BF16)))
            o_slc = acc_ref[g] / jnp.maximum(l_ref[g], 1e-30)
            wrows = cw_ref.shape[3]
            kw_new = kwvw_ref[0, :, g * HEAD_DIM:(g + 1) * HEAD_DIM]
            vw_new = kwvw_ref[0, :, (KV_HEADS + g) * HEAD_DIM:(KV_HEADS + g + 1) * HEAD_DIM]
            sw = jnp.concatenate([_dot(Qb, cw_ref[0, g].astype(BF16)),
                                  _dot(Qb, kw_new.astype(BF16), _NT)], axis=1)
            idx = lax.broadcasted_iota(jnp.int32, (1, wrows + t_pad), 1)
            spos = past_len - wrows + idx
            dt = t_rows - spos
            real = jnp.where(idx < wrows + t_real, 1.0, 0.0)
            mask_w = jnp.where(dt >= 0, jnp.where(dt < WINDOW, real, 0.0), 0.0)
            pw = _masked_softmax(sw, mask_w).astype(BF16)
            o_win = _dot(pw[:, 0:wrows], cw_ref[1, g].astype(BF16), _NT) + _dot(pw[:, wrows:], vw_new.astype(BF16))
            pieces += _gated_heads(gts, g, ocmp_ref[0, g], o_slc, o_win, t_pad)
        o_ref[0] = jnp.concatenate(pieces, axis=1)


def _nsa_sample_b(page_table, cache_t, win_t, layer, z, selm, ocmp, past_len, t_real):
    B, t_pad, _ = z.shape
    n_pages = page_table.shape[1]
    n_steps = n_pages // _PPS
    rows = Q_PER_KV * t_pad
    wrows = win_t.shape[5]
    pidx = lambda b, s, pt, k: pt[b, jnp.minimum(s, n_steps - 1) * _PPS + k]
    page = lambda which, k: pl.BlockSpec((None, None, None, KV_HEADS, HEAD_DIM, PAGE),
                                         lambda b, s, pt: (layer, pidx(b, s, pt, k), which, 0, 0, 0))
    kspec = lambda k: page(2, k)
    vspec = lambda k: page(3, k)
    zspec = lambda col, w: pl.BlockSpec((1, t_pad, w), lambda b, s, pt: (b, 0, col // w))
    per_b = lambda shape: pl.BlockSpec((1,) + shape, lambda b, s, pt: (b,) + (0,) * len(shape))
    grid_spec = pltpu.PrefetchScalarGridSpec(
        num_scalar_prefetch=1,
        grid=(B, n_steps + 1),
        in_specs=[kspec(k) for k in range(_PPS)] + [vspec(k) for k in range(_PPS)] + [
            zspec(C_Q, GW), zspec(C_ROWS, GW), zspec(C_KW, 256), zspec(C_GATES, 256),
            per_b((KV_HEADS * t_pad, _SEL_ROWS)), per_b((KV_HEADS, rows, HEAD_DIM)),
            pl.BlockSpec((None, None, 2, KV_HEADS, HEAD_DIM, wrows), lambda b, s, pt: (layer, b, 0, 0, 0, 0))],
        out_specs=pl.BlockSpec((1, t_pad, GW), lambda b, s, pt: (b, 0, 0)),
        scratch_shapes=[pltpu.VMEM((KV_HEADS, rows, 1), F32), pltpu.VMEM((KV_HEADS, rows, 1), F32),
                        pltpu.VMEM((KV_HEADS, rows, HEAD_DIM), F32)])
    return pl.pallas_call(
        functools.partial(_nsa_sample_b_kernel, n_pages=n_pages, past_len=past_len, t_pad=t_pad, t_real=t_real),
        grid_spec=grid_spec,
        out_shape=jax.ShapeDtypeStruct((B, t_pad, GW), F32),
        compiler_params=_cparams(("parallel", "arbitrary")),
        name="nsa_sample_attend",
    )(page_table, *([cache_t] * (2 * _PPS)), z, z, z, z, selm, ocmp, win_t)


def _route(h, w, b):
    logits = _dot3(h, w) + b
    lane = lax.broadcasted_iota(jnp.int32, logits.shape, 1).astype(F32)
    big = 1e6
    is_g = lane < N_GROUPS
    gl = jnp.where(is_g, logits, NEG)
    gmax = jnp.max(gl, axis=-1, keepdims=True)
    gsel = jnp.min(jnp.where(is_g, jnp.where(gl == gmax, lane, big), big), axis=-1, keepdims=True)
    gp = 1.0 / jnp.sum(jnp.where(is_g, jnp.exp(gl - gmax), 0.0), axis=-1, keepdims=True)
    lo = N_GROUPS + gsel * EXP_PER_GROUP
    in_grp = jnp.where(lane >= lo, jnp.where(lane < lo + EXP_PER_GROUP, 1.0, 0.0), 0.0)
    el = jnp.where(in_grp > 0.5, logits, NEG)
    e = jnp.exp(el - jnp.max(el, axis=-1, keepdims=True)) * in_grp
    p = e / jnp.sum(e, axis=-1, keepdims=True)
    pm = jnp.where(in_grp > 0.5, p, -1.0)
    v1 = jnp.max(pm, axis=-1, keepdims=True)
    i1 = jnp.min(jnp.where(pm == v1, lane, big), axis=-1, keepdims=True)
    pm2 = jnp.where(lane == i1, -1.0, pm)
    v2 = jnp.max(pm2, axis=-1, keepdims=True)
    i2 = jnp.min(jnp.where(pm2 == v2, lane, big), axis=-1, keepdims=True)
    tot = v1 + v2
    return jnp.where(lane == 0, i1 - N_GROUPS,
                     jnp.where(lane == 1, i2 - N_GROUPS,
                               jnp.where(lane == 2, v1 / tot * gp, jnp.where(lane == 3, v2 / tot * gp, 0.0))))


def _proj_out_kernel(x_ref, ya_ref, yb_ref, yc_ref, yd_ref, g_ref, w_ref, b_ref, lg_ref, lb_ref, rw_ref, rb_ref,
                     h_ref, r_ref, *, alpha):
    yd = yd_ref[...] * g_ref[...]
    acc = _dot(ya_ref[...].astype(BF16), w_ref[0:GW, :])
    acc = acc + _dot(yb_ref[...].astype(BF16), w_ref[GW:2 * GW, :])
    acc = acc + _dot(yc_ref[...].astype(BF16), w_ref[2 * GW:3 * GW, :])
    acc = acc + _dot(yd.astype(BF16), w_ref[3 * GW:4 * GW, :])
    h = _layer_norm(alpha * x_ref[...] + acc + b_ref[...], lg_ref[...], lb_ref[...], LN_EPS)
    h_ref[...] = h
    r_ref[...] = _route(h, rw_ref[...], rb_ref[...])


def _proj_out(x2d, ya, yb, yc, yd, g, lp, alpha):
    n = x2d.shape[0]
    tm = min(256, n)
    row = lambda w: pl.BlockSpec((tm, w), lambda i: (i, 0))
    return pl.pallas_call(
        functools.partial(_proj_out_kernel, alpha=alpha),
        grid=(n // tm,),
        in_specs=[row(D_MODEL), row(GW), row(GW), row(GW), row(GW), row(GW),
                  _const_spec((D_MODEL, D_MODEL)), _const_spec((1, D_MODEL)), _const_spec((1, D_MODEL)),
                  _const_spec((1, D_MODEL)), _const_spec((D_MODEL, 128)), _const_spec((1, 128))],
        out_specs=[row(D_MODEL), row(128)],
        out_shape=[jax.ShapeDtypeStruct((n, D_MODEL), F32), jax.ShapeDtypeStruct((n, 128), F32)],
        compiler_params=_cparams(("parallel",)),
        name="proj_out_ln1",
    )(x2d, ya, yb, yc, yd, g, lp["w_out"], lp["b_out"], lp["ln1_g"], lp["ln1_b"], lp["router_w"], lp["router_b"])


def _for_rows(n_rows, body):
    if isinstance(n_rows, int):
        lax.fori_loop(0, n_rows, lambda r, c: (body(r), c)[1], 0, unroll=8)
        return
    groups = n_rows // 8

    def group(i, c):
        for u in range(8):
            body(i * 8 + u)
        return c

    lax.fori_loop(0, groups, group, 0)
    lax.fori_loop(groups * 8, n_rows, lambda r, c: (body(r), c)[1], 0)


def _issue_rows(idx_fn, n_rows, src_ref, dst_ref, sem):
    _for_rows(n_rows, lambda r: pltpu.make_async_copy(src_ref.at[pl.ds(idx_fn(r), 1), :],
                                                      dst_ref.at[pl.ds(r, 1), :], sem).start())


def _wait_rows(n_rows, src_ref, dst_ref, sem):
    _for_rows(n_rows, lambda r: pltpu.make_async_copy(src_ref.at[pl.ds(0, 1), :],
                                                      dst_ref.at[pl.ds(r, 1), :], sem).wait())


def _moe_kernel(te_ref, nu_ref, src_ref, cnt_ref, first_ref, nxt_ref, wslot_ref, tok_ref,
                h_ref, wg_ref, wu_ref, wd_ref, o_ref,
                xbuf_ref, sems, wgs_ref, wus_ref, wds_ref, wsems, wgb_ref, wub_ref, wdb_ref, *, layer):
    t = pl.program_id(0)
    tm = o_ref.shape[0]
    slot = t % 2
    n_used = nu_ref[0]

    def fetch(tile, to_slot):
        src = src_ref[tile]
        _issue_rows(lambda r: tok_ref[src + r], cnt_ref[tile], h_ref, xbuf_ref.at[to_slot], sems.at[to_slot])

    def weight_copies(e, s):
        return (pltpu.make_async_copy(wg_ref.at[layer, e], wgs_ref.at[s], wsems.at[s, 0]),
                pltpu.make_async_copy(wu_ref.at[layer, e], wus_ref.at[s], wsems.at[s, 1]),
                pltpu.make_async_copy(wd_ref.at[layer, e], wds_ref.at[s], wsems.at[s, 2]))

    @pl.when(t == 0)
    def _():
        for c in weight_copies(te_ref[0], 0):
            c.start()
        xbuf_ref[...] = jnp.zeros(xbuf_ref.shape, F32)
        fetch(0, 0)

    @pl.when(t + 1 < n_used)
    def _():
        fetch(t + 1, 1 - slot)

    @pl.when(first_ref[t] == 1)
    def _():
        s = wslot_ref[t]
        for c in weight_copies(te_ref[t], s):
            c.wait()

        @pl.when(nxt_ref[t] >= 0)
        def _():
            for c in weight_copies(nxt_ref[t], 1 - s):
                c.start()

        wgb_ref[...] = wgs_ref[s].astype(BF16)
        wub_ref[...] = wus_ref[s].astype(BF16)
        wdb_ref[...] = wds_ref[s].astype(BF16)

    @pl.when(jnp.logical_or(t == 0, t < n_used))
    def _():
        _wait_rows(cnt_ref[t], h_ref, xbuf_ref.at[slot], sems.at[slot])
        xb = xbuf_ref[slot].astype(BF16)
        hg = _silu(_dot(xb, wgb_ref[...])) * _dot(xb, wub_ref[...])
        o_ref[...] = _dot(hg.astype(BF16), wdb_ref[...])

    @pl.when(jnp.logical_and(t > 0, t >= n_used))
    def _():
        o_ref[...] = jnp.zeros(o_ref.shape, F32)


def _moe_experts(sched, h, w_gate, w_up, w_down, layer):
    tm = MOE_TILE
    r = sched[0].shape[0] * tm
    hbm = pl.BlockSpec(memory_space=pl.ANY)
    grid_spec = pltpu.PrefetchScalarGridSpec(
        num_scalar_prefetch=len(sched),
        grid=(r // tm,),
        in_specs=[hbm, hbm, hbm, hbm],
        out_specs=pl.BlockSpec((tm, D_MODEL), lambda t, *_: (t, 0)),
        scratch_shapes=[pltpu.VMEM((2, tm, D_MODEL), F32), pltpu.SemaphoreType.DMA((2,)),
                        pltpu.VMEM((2, D_MODEL, D_EXPERT), F32), pltpu.VMEM((2, D_MODEL, D_EXPERT), F32),
                        pltpu.VMEM((2, D_EXPERT, D_MODEL), F32), pltpu.SemaphoreType.DMA((2, 3)),
                        pltpu.VMEM((D_MODEL, D_EXPERT), BF16), pltpu.VMEM((D_MODEL, D_EXPERT), BF16),
                        pltpu.VMEM((D_EXPERT, D_MODEL), BF16)])
    return pl.pallas_call(
        functools.partial(_moe_kernel, layer=layer),
        grid_spec=grid_spec,
        out_shape=jax.ShapeDtypeStruct((r, D_MODEL), F32),
        compiler_params=_cparams(("arbitrary",), disable_bounds_checks=True),
        name="moe_experts",
    )(*sched, h, w_gate, w_up, w_down)


def _moe_schedule(eid):
    n = eid.shape[0]
    tm = MOE_TILE
    n_tiles = (2 * n + N_EXPERTS * (tm - 1)) // tm + 1
    flat_e = eid.reshape(-1)
    order = jnp.argsort(flat_e, stable=True).astype(jnp.int32)
    rank = jnp.argsort(order).astype(jnp.int32)
    onehot = (flat_e[:, None] == jnp.arange(N_EXPERTS, dtype=jnp.int32)[None, :]).astype(jnp.int32)
    counts = jnp.sum(onehot, axis=0)
    padded = (counts + tm - 1) // tm * tm
    pad_end = jnp.cumsum(padded)
    g_start = pad_end - padded
    u_start = jnp.cumsum(counts) - counts
    pos = (rank + jnp.sum(onehot * (g_start - u_start)[None, :], axis=1)).reshape(n, 2)
    n_used = (pad_end[-1] // tm).astype(jnp.int32)
    tile_id = jnp.arange(n_tiles, dtype=jnp.int32)
    tile_expert = jnp.sum((pad_end[None, :] <= (tile_id * tm)[:, None]).astype(jnp.int32), axis=1)
    tile_expert = jnp.minimum(tile_expert, N_EXPERTS - 1)
    in_tile = (tile_expert[:, None] == jnp.arange(N_EXPERTS, dtype=jnp.int32)[None, :]).astype(jnp.int32)
    pick = lambda table: jnp.sum(in_tile * table[None, :], axis=1)
    off = tile_id * tm - pick(g_start)
    tile_src = (pick(u_start) + off).astype(jnp.int32)
    tile_cnt = jnp.clip(pick(counts) - off, 0, tm).astype(jnp.int32)
    live = tile_id < n_used
    tile_cnt = jnp.where(live, tile_cnt, 0)
    ar = jnp.arange(N_EXPERTS, dtype=jnp.int32)
    used = counts > 0
    nxt_e = jnp.min(jnp.where(jnp.logical_and(ar[None, :] > ar[:, None], used[None, :]), ar[None, :], N_EXPERTS), axis=1)
    nxt_e = jnp.where(nxt_e < N_EXPERTS, nxt_e, -1)
    first = jnp.logical_and(live, off == 0).astype(jnp.int32)
    wslot = (pick(jnp.cumsum(used.astype(jnp.int32)) - 1) % 2).astype(jnp.int32)
    sched = (tile_expert.astype(jnp.int32), n_used.reshape(1), tile_src, tile_cnt, first,
             pick(nxt_e).astype(jnp.int32), wslot, order // 2)
    return sched, pos


def _ln2_kernel(idx_ref, h_ref, rt_ref, ys_ref, g_ref, b_ref, o_ref, ybuf_ref, sems, *, alpha):
    i = pl.program_id(0)
    tm = h_ref.shape[0]
    slot = i % 2

    def fetch(tile, to_slot):
        base = tile * 2 * tm
        _issue_rows(lambda r: idx_ref[base + r], 2 * tm, ys_ref, ybuf_ref.at[to_slot], sems.at[to_slot])

    @pl.when(i == 0)
    def _():
        fetch(0, 0)

    @pl.when(i + 1 < pl.num_programs(0))
    def _():
        fetch(i + 1, 1 - slot)

    _wait_rows(2 * tm, ys_ref, ybuf_ref.at[slot], sems.at[slot])
    w0 = rt_ref[:, 2:3]
    w1 = rt_ref[:, 3:4]
    moe = w0 * ybuf_ref[slot, 0:tm, :] + w1 * ybuf_ref[slot, tm:2 * tm, :]
    o_ref[...] = _layer_norm(alpha * h_ref[...] + moe, g_ref[...], b_ref[...], LN_EPS)


def _combine_ln2(h, routed, pos, ys, g, b, alpha):
    n = h.shape[0]
    tm = min(128, n)
    nt = n // tm
    idx = jnp.transpose(pos.reshape(nt, tm, 2), (0, 2, 1)).reshape(-1)
    grid_spec = pltpu.PrefetchScalarGridSpec(
        num_scalar_prefetch=1,
        grid=(nt,),
        in_specs=[pl.BlockSpec((tm, D_MODEL), lambda i, idx_ref: (i, 0)),
                  pl.BlockSpec((tm, 128), lambda i, idx_ref: (i, 0)),
                  pl.BlockSpec(memory_space=pl.ANY),
                  pl.BlockSpec((1, D_MODEL), lambda i, idx_ref: (0, 0)),
                  pl.BlockSpec((1, D_MODEL), lambda i, idx_ref: (0, 0))],
        out_specs=pl.BlockSpec((tm, D_MODEL), lambda i, idx_ref: (i, 0)),
        scratch_shapes=[pltpu.VMEM((2, 2 * tm, D_MODEL), F32), pltpu.SemaphoreType.DMA((2,))])
    return pl.pallas_call(
        functools.partial(_ln2_kernel, alpha=alpha),
        grid_spec=grid_spec,
        out_shape=jax.ShapeDtypeStruct((n, D_MODEL), F32),
        compiler_params=_cparams(("arbitrary",), disable_bounds_checks=True),
        name="combine_ln2",
    )(idx, h, routed, ys, g, b)


def _relayout_cols(w):
    pad = lambda n: jnp.zeros(w.shape[:-1] + (n,), w.dtype)
    d0 = 3864
    return jnp.concatenate([
        w[..., 0:2328], pad(C_BG - 2328),
        w[..., 2328:3864],
        w[..., d0:d0 + 1536],
        w[..., d0 + 1536:d0 + 1568], pad(96),
        w[..., d0 + 1568:d0 + 1600], pad(96),
        w[..., d0 + 1600:d0 + 1696], pad(N_Z - C_GL - 96)], axis=-1)


def _relayout_d(v):
    pad = lambda n: jnp.zeros(v.shape[:-1] + (n,), v.dtype)
    return jnp.concatenate([v[..., 0:1536], v[..., 1536:1568], pad(96), v[..., 1568:1600], pad(96),
                            v[..., 1600:1696], pad(32)], axis=-1)


def _shift_cols(zrow):
    return jnp.concatenate([zrow[..., C_R:C_R + 1536], zrow[..., C_WL:C_WL + 32], zrow[..., C_AL:C_AL + 32],
                            zrow[..., C_GL:C_GL + 96]], axis=-1)


def _pad_rows(w, rows):
    return jnp.concatenate([w, jnp.zeros((rows - w.shape[0],) + w.shape[1:], w.dtype)], axis=0)


def _layer_params(l, P):
    wc = P["nsa_wc"][l]
    half = 2 * CMP_STRIDE // 2
    wcat = jnp.concatenate([wc[:, :half].reshape(2, half * HEAD_DIM, HEAD_DIM),
                            wc[:, half:].reshape(2, half * HEAD_DIM, HEAD_DIM)], axis=-1)
    pe = P["nsa_pe"][l]
    pe2 = jnp.stack([pe[:, :half].reshape(2, half * HEAD_DIM), pe[:, half:].reshape(2, half * HEAD_DIM)], axis=1)
    pe2 = jnp.concatenate([pe2, jnp.zeros((2, 6, half * HEAD_DIM), F32)], axis=1)
    router_w = jnp.concatenate([P["moe_w_grp"][l], P["moe_w_rte"][l],
                                jnp.zeros((D_MODEL, 128 - N_GROUPS - N_EXPERTS), F32)], axis=1)
    router_b = jnp.concatenate([P["moe_b_grp"][l], P["moe_b_rte"][l],
                                jnp.zeros((128 - N_GROUPS - N_EXPERTS,), F32)])[None, :]
    row = lambda name: P[name][l][None, :]
    return dict(
        w_in=_relayout_cols(P["w_in"][l]).astype(BF16), b_in=_relayout_cols(P["b_in"][l])[None, :],
        w_out=P["w_out"][l].astype(BF16), b_out=row("b_out"),
        ln1_g=row("ln1_g"), ln1_b=row("ln1_b"), ln2_g=row("ln2_g"), ln2_b=row("ln2_b"),
        conv_a_w=P["conv_a_w"][l], conv_a_b=row("conv_a_b"), ln_a_g=row("ln_a_g"), ln_a_b=row("ln_a_b"),
        conv_c_w=P["conv_c_w"][l], wcat=wcat, pe2=pe2,
        rw_mu=_relayout_d(P["rw_mu"][l])[None, :], rw_w0=row("rw_w0"), rw_w2=_pad_rows(P["rw_w2"][l], 128),
        rw_a0=row("rw_a0"), rw_a2=_pad_rows(P["rw_a2"][l], 128), rw_g2=_pad_rows(P["rw_g2"][l], 128),
        rw_kk=row("rw_kk"), rw_ka=row("rw_ka"), rw_rk=P["rw_rk"][l].reshape(1, GW),
        rw_lnx_g=P["rw_lnx_g"][l].reshape(HEADS, HEAD_DIM), rw_lnx_b=P["rw_lnx_b"][l].reshape(HEADS, HEAD_DIM),
        router_w=router_w, router_b=router_b)


def _mix_rwkv(z, shift, s0, lp, t_real):
    B, T, _ = z.shape
    if t_real == T and T % _RW_CHUNK == 0:
        return _rwkv_chunked(z, shift, s0, lp)
    r, w, k, v, kk, b, bonus, g = _rwkv_prep(z, shift, lp["rw_mu"], lp["rw_w0"], lp["rw_w2"], lp["rw_a0"],
                                             lp["rw_a2"], lp["rw_g2"], lp["rw_kk"], lp["rw_ka"], lp["rw_rk"], t_real)
    y, s_new = _rwkv_scan(r, w, k, v, kk, b, bonus, s0, lp["rw_lnx_g"], lp["rw_lnx_b"], t_real)
    yd = jnp.transpose(y, (0, 2, 1, 3)).reshape(B, T, GW)
    return yd, g, s_new


def _nsa_prompt_group(z, lp):
    B, T, _ = z.shape
    kv6 = z[:, :, C_ROWS:C_GATES].reshape(B, T, 6, KV_HEADS, HEAD_DIM)
    nc, ns = T // CMP_STRIDE, T // SLC_BLOCK
    cmp_in = kv6[:, :, 0:2].reshape(B, nc, CMP_STRIDE, 2, KV_HEADS, HEAD_DIM)
    cmp_in = jnp.transpose(cmp_in, (0, 3, 4, 1, 2, 5)).reshape(B, 2, KV_HEADS, nc, CMP_STRIDE * HEAD_DIM)
    ckv = _compress_prompt(cmp_in, lp["wcat"], lp["pe2"])
    rows_of = lambda x: jnp.transpose(x.astype(BF16), (0, 2, 1, 3))
    block_id = jnp.arange(T, dtype=jnp.int32) // SLC_BLOCK
    onehot = (block_id[:, None] == jnp.arange(ns, dtype=jnp.int32)[None, :]).astype(BF16)
    ks = jnp.concatenate([rows_of(kv6[:, :, 2]), jnp.broadcast_to(onehot, (B, KV_HEADS, T, ns))], axis=-1)

    def value_tiles(x, tile):
        vT = jnp.transpose(x.astype(BF16), (0, 2, 3, 1))
        aug = jnp.concatenate([vT, jnp.ones((B, KV_HEADS, 1, T), BF16), jnp.zeros((B, KV_HEADS, HEAD_DIM - 1, T), BF16)], axis=2)
        return jnp.transpose(aug.reshape(B, KV_HEADS, 2 * HEAD_DIM, T // tile, tile), (0, 1, 3, 2, 4))

    qT = jnp.transpose(z[:, :, C_Q:C_Q + GW], (0, 2, 1))
    gT = jnp.transpose(z[:, :, C_GATES:C_GATES + 32], (0, 2, 1))
    ybT = _nsa_prompt(qT, gT, ckv, ks, value_tiles(kv6[:, :, 3], _KT), rows_of(kv6[:, :, 4]), value_tiles(kv6[:, :, 5], Q_BLOCK))
    rows = kv6[:, :, 0:4]
    win = kv6[:, T - min(WINDOW, T):, 4:6]
    return jnp.transpose(ybT, (0, 2, 1)), rows, win


def _cache_views(cache_nsa, cache_win):
    L, n_pool = cache_nsa.shape[0], cache_nsa.shape[1]
    cpp = PAGE // CMP_STRIDE
    cache_t = jnp.transpose(cache_nsa, (0, 1, 3, 4, 5, 2))
    win_t = jnp.transpose(cache_win, (0, 1, 3, 4, 5, 2))
    cmp_pages = cache_t[:, :, 0:2].reshape(L, n_pool, 2, KV_HEADS, HEAD_DIM, cpp, CMP_STRIDE)
    cmp_pages = jnp.transpose(cmp_pages, (0, 1, 2, 3, 5, 6, 4)).reshape(L, n_pool, 2 * KV_HEADS, cpp, CMP_STRIDE * HEAD_DIM)
    return cache_t, win_t, cmp_pages


def _nsa_sample_group(z, views, layer, page_table, win_l, lp, t_real):
    B, t_pad, _ = z.shape
    cache_t, win_t, cmp_pages = views
    past_len = page_table.shape[1] * PAGE
    ocmp, selm = _nsa_sample_a(page_table, cmp_pages, layer, lp["wcat"], lp["pe2"], z, past_len)
    yb = _nsa_sample_b(page_table, cache_t, win_t, layer, z, selm, ocmp, past_len, t_real)
    zr = z[:, :t_real]
    rows = zr[:, :, C_ROWS:C_KW].reshape(B, t_real, 4, KV_HEADS, HEAD_DIM)
    new_win = zr[:, :, C_KW:C_GATES].reshape(B, t_real, 2, KV_HEADS, HEAD_DIM)
    win = jnp.concatenate([win_l, new_win], axis=1)[:, t_real:]
    return yb, rows, win


def _group_layer(x, lp, alpha, nsa_fn, conv_a_buf, conv_c_buf, rw_state, rw_shift, t_real):
    B, T, _ = x.shape
    z = _proj_in(x.reshape(B * T, D_MODEL), lp["w_in"], lp["b_in"]).reshape(B, T, N_Z)
    ya, yc, conv_a_new, conv_c_new = _conv_mix(z, conv_a_buf, conv_c_buf, lp["conv_a_w"], lp["conv_a_b"],
                                               lp["ln_a_g"], lp["ln_a_b"], lp["conv_c_w"], t_real)
    yb, nsa_rows, win_new = nsa_fn(z)
    yd, g, rw_state_new = _mix_rwkv(z, _relayout_d(rw_shift)[:, None, :], rw_state, lp, t_real)
    flat = lambda a: a.reshape(B * T, a.shape[-1])
    h, routed = _proj_out(flat(x), flat(ya), flat(yb), flat(yc), flat(yd), flat(g), lp, alpha)
    shift_new = _shift_cols(z[:, t_real - 1])
    return (h, routed), (nsa_rows, win_new, conv_a_new, conv_c_new, rw_state_new, shift_new)


def _moe_ln2(groups, lp, P, layer, alpha):
    h_list = [g[0] for g in groups]
    routed = [g[1] for g in groups]
    eid = jnp.concatenate([r[:, 0:2] for r in routed], axis=0).astype(jnp.int32)
    h_all = jnp.concatenate(h_list, axis=0)
    sched, pos = _moe_schedule(eid)
    ys = _moe_experts(sched, h_all, P["moe_w_gate"], P["moe_w_up"], P["moe_w_down"], layer)
    outs = []
    off = 0
    for h, rt in zip(h_list, routed):
        n = h.shape[0]
        outs.append(_combine_ln2(h, rt, pos[off:off + n], ys, lp["ln2_g"], lp["ln2_b"], alpha))
        off += n
    return outs


def kernel(x_prompt, x_sample, cache_nsa, cache_win, state_conv_a, state_conv_c, state_rwkv, state_rwkv_shift, page_table, w_in, b_in, w_out, b_out, ln1_g, ln1_b, ln2_g, ln2_b, conv_a_w, conv_a_b, ln_a_g, ln_a_b, conv_c_w, nsa_wc, nsa_pe, rw_mu, rw_w0, rw_w2, rw_a0, rw_a2, rw_g2, rw_kk, rw_ka, rw_rk, rw_lnx_g, rw_lnx_b, moe_w_grp, moe_b_grp, moe_w_rte, moe_b_rte, moe_w_gate, moe_w_up, moe_w_down):
    P = dict(w_in=w_in, b_in=b_in, w_out=w_out, b_out=b_out, ln1_g=ln1_g, ln1_b=ln1_b, ln2_g=ln2_g, ln2_b=ln2_b,
             conv_a_w=conv_a_w, conv_a_b=conv_a_b, ln_a_g=ln_a_g, ln_a_b=ln_a_b, conv_c_w=conv_c_w,
             nsa_wc=nsa_wc, nsa_pe=nsa_pe, rw_mu=rw_mu, rw_w0=rw_w0, rw_w2=rw_w2, rw_a0=rw_a0, rw_a2=rw_a2,
             rw_g2=rw_g2, rw_kk=rw_kk, rw_ka=rw_ka, rw_rk=rw_rk, rw_lnx_g=rw_lnx_g, rw_lnx_b=rw_lnx_b,
             moe_w_grp=moe_w_grp, moe_b_grp=moe_b_grp, moe_w_rte=moe_w_rte, moe_b_rte=moe_b_rte,
             moe_w_gate=moe_w_gate, moe_w_up=moe_w_up, moe_w_down=moe_w_down)
    depth = w_in.shape[0]
    alpha = (2.0 * depth) ** 0.25
    bp, tp, _ = x_prompt.shape
    bs, ts, _ = x_sample.shape
    ts_pad = 8
    xp = x_prompt
    xs = jnp.concatenate([x_sample, jnp.zeros((bs, ts_pad - ts, D_MODEL), F32)], axis=1)
    outs_p, outs_s = [], []
    views = _cache_views(cache_nsa, cache_win)
    for l in range(depth):
        lp = _layer_params(l, P)
        h_p, st_p = _group_layer(xp, lp, alpha, functools.partial(_nsa_prompt_group, lp=lp),
                                 jnp.zeros((bp, CONV_A_WIDTH - 1, GW), F32), jnp.zeros((bp, SHORT_CONV_WIDTH - 1, GW), F32),
                                 jnp.zeros((bp, HEADS, HEAD_DIM, HEAD_DIM), F32), jnp.zeros((bp, N_COLS_D), F32), tp)
        h_s, st_s = _group_layer(xs, lp, alpha,
                                 functools.partial(_nsa_sample_group, views=views, layer=l, page_table=page_table,
                                                   win_l=cache_win[l], lp=lp, t_real=ts),
                                 state_conv_a[l], state_conv_c[l], state_rwkv[l], state_rwkv_shift[l], ts)
        o_p, o_s = _moe_ln2([h_p, h_s], lp, P, l, alpha)
        xp = o_p.reshape(bp, tp, D_MODEL)
        xs = o_s.reshape(bs, ts_pad, D_MODEL)
        outs_p.append(st_p)
        outs_s.append(st_s)
    stk = lambda outs, i: jnp.stack([o[i] for o in outs], axis=0)
    return (xp, xs[:, :ts], stk(outs_p, 0), stk(outs_s, 0), stk(outs_p, 1), stk(outs_s, 1), stk(outs_p, 2), stk(outs_s, 2),
            stk(outs_p, 3), stk(outs_s, 3), stk(outs_p, 4), stk(outs_s, 4), stk(outs_p, 5), stk(outs_s, 5))
```
